```python
import math
import jax, jax.numpy as jnp
from jax import lax
import numpy as np

D_MODEL = 1024
BATCH = 16
SEQ = 256
DEPTH = 1
DEC_BATCH = 8
DEC_SEQ = 1024
PAST_LEN = 512

GRID_W = 64
ATT_WIDTH = D_MODEL // 2
RNN_WIDTH = D_MODEL - ATT_WIDTH
N_ATT_HEADS = 4
V_HEAD_DIM = ATT_WIDTH // N_ATT_HEADS
QK_HEAD_DIM = V_HEAD_DIM // 2
N_RNN_HEADS = 4
RNN_HEAD_DIM = RNN_WIDTH // N_RNN_HEADS
D_FF = 2816
CONV_W = 3
ROPE_BASE = 10000.0
Q_BLOCK = 128
CHUNK = 16
N_MOD = 6
SPLIT_WIDTHS = (ATT_WIDTH, ATT_WIDTH, ATT_WIDTH, RNN_WIDTH, RNN_WIDTH, RNN_WIDTH, RNN_WIDTH, RNN_WIDTH)
IN_COLS = 3 * ATT_WIDTH + 5 * RNN_WIDTH
DEEPNORM_ALPHA = (2.0 * DEPTH) ** 0.25
DEEPNORM_BETA = (8.0 * DEPTH) ** -0.25
NORM_EPS = 1e-5

kernel_name = 'hybrid_diffattn_hgrn2_dit_step'


def layer_norm(x, g, b):
    xf = x.astype(jnp.float32)
    mu = jnp.mean(xf, axis=-1, keepdims=True)
    var = jnp.mean(jnp.square(xf - mu), axis=-1, keepdims=True)
    y = (xf - mu) * lax.rsqrt(var + NORM_EPS) * g.astype(jnp.float32) + b.astype(jnp.float32)
    return y.astype(x.dtype)


def head_rms_norm(x, g):
    xf = x.astype(jnp.float32)
    y = xf * lax.rsqrt(jnp.mean(xf * xf, axis=-1, keepdims=True) + NORM_EPS)
    return y * g.astype(jnp.float32).reshape(x.shape[-2], x.shape[-1])


def axial_rope(x, rows, cols):
    half = QK_HEAD_DIM // 2
    quarter = half // 2
    freqs = 1.0 / (ROPE_BASE ** (jnp.arange(quarter, dtype=jnp.float32) / quarter))

    def rotate(xa, pos):
        ang = pos.astype(jnp.float32)[:, None] * freqs
        cos = jnp.cos(ang)[None, :, None, None, :].astype(x.dtype)
        sin = jnp.sin(ang)[None, :, None, None, :].astype(x.dtype)
        x1, x2 = xa[..., :quarter], xa[..., quarter:]
        return jnp.concatenate([x1 * cos - x2 * sin, x1 * sin + x2 * cos], axis=-1)

    return jnp.concatenate([rotate(x[..., :half], rows), rotate(x[..., half:], cols)], axis=-1)


def diff_attention(q, k, v, lam):
    b, lq = q.shape[0], q.shape[1]
    nblk = lq // Q_BLOCK
    qb = jnp.moveaxis(q.reshape(b, nblk, Q_BLOCK, N_ATT_HEADS, 2, QK_HEAD_DIM), 1, 0)
    scale = QK_HEAD_DIM ** -0.5

    def block(qblk):
        s = jnp.einsum('bqhmd,bkhmd->bmhqk', qblk, k).astype(jnp.float32) * scale
        p = jax.nn.softmax(s, axis=-1)
        pd = p[:, 0] - lam * p[:, 1]
        return jnp.einsum('bhqk,bkhe->bqhe', pd.astype(v.dtype), v)

    out = lax.map(block, qb)
    return jnp.moveaxis(out, 0, 1).reshape(b, lq, N_ATT_HEADS, V_HEAD_DIM)


def gla_chunk_scan(q, k, v, logf, s0):
    b, l, h, dk = q.shape
    dv = v.shape[-1]
    n = l // CHUNK
    q = q.reshape(b, n, CHUNK, h, dk)
    k = k.reshape(b, n, CHUNK, h, dk)
    logf = logf.reshape(b, n, CHUNK, h, dk)
    v = v.reshape(b, n, CHUNK, h, dv)
    cum = jnp.cumsum(logf, axis=2)
    last = cum[:, :, -1]
    causal = jnp.tril(jnp.ones((CHUNK, CHUNK), dtype=bool))[None, None, :, :, None, None]
    rel = cum[:, :, :, None] - cum[:, :, None, :]
    decay = jnp.where(causal, jnp.exp(jnp.where(causal, rel, 0.0)), 0.0)
    scores = jnp.einsum('bnthd,bntshd,bnshd->bnhts', q, decay, k)
    o_intra = jnp.einsum('bnhts,bnshe->bnthe', scores, v)
    k_end = k * jnp.exp(last[:, :, None] - cum)
    ds = jnp.einsum('bnshd,bnshe->bnhde', k_end, v)
    chunk_decay = jnp.exp(last)

    def step(s, inp):
        dec, d = inp
        return dec[..., None] * s + d, s

    s_final, s_in = lax.scan(step, s0, (jnp.moveaxis(chunk_decay, 1, 0), jnp.moveaxis(ds, 1, 0)))
    s_in = jnp.moveaxis(s_in, 0, 1)
    o_inter = jnp.einsum('bnthd,bnhde->bnthe', q * jnp.exp(cum), s_in)
    return (o_intra + o_inter).reshape(b, l, h, dv), s_final


def hgrn2_bidir(q, i, f_fwd, f_bwd, lb_fwd, lb_bwd, s0_fwd, s0_bwd):
    b, l, _ = q.shape

    def heads(t):
        return t.astype(jnp.float32).reshape(b, l, N_RNN_HEADS, RNN_HEAD_DIM)

    def gates(f_logit, lb):
        f = lb + (1.0 - lb) * jax.nn.sigmoid(f_logit.astype(jnp.float32))
        return heads(1.0 - f), heads(jnp.log(f))

    qh, vh = heads(q), heads(i)
    k_f, logf_f = gates(f_fwd, lb_fwd)
    o_f, s_f = gla_chunk_scan(qh, k_f, vh, logf_f, s0_fwd)
    k_b, logf_b = gates(f_bwd, lb_bwd)
    o_b, s_b = gla_chunk_scan(jnp.flip(qh, 1), jnp.flip(k_b, 1), jnp.flip(vh, 1), jnp.flip(logf_b, 1), s0_bwd)
    return o_f + jnp.flip(o_b, 1), s_f, s_b


def conv_ffn(x, w_up, conv_w, conv_b, w_down):
    h = x @ w_up
    pad = CONV_W // 2
    l = h.shape[1]
    hp = jnp.pad(h, ((0, 0), (pad, pad), (0, 0)))
    h = sum(hp[:, j:j + l] * conv_w[j] for j in range(CONV_W)) + conv_b
    a, u = jnp.split(h, 2, axis=-1)
    return (jax.nn.silu(a) * u) @ w_down


def trunk_layer(x, mod, p, rows=None, cols=None, ctx_k=None, ctx_v=None, s0_fwd=None, s0_bwd=None):
    b, l, _ = x.shape
    shift1, scale1, gate1, shift2, scale2, gate2 = jnp.split(mod, N_MOD, axis=-1)
    xm = x * (1 + scale1) + shift1
    offsets = np.cumsum(SPLIT_WIDTHS)[:-1].tolist()
    aq, ak, av, rq, rf, rb, ri, rg = jnp.split(xm @ p['w_in'], offsets, axis=-1)
    aq = aq.reshape(b, l, N_ATT_HEADS, 2, QK_HEAD_DIM)
    ak = ak.reshape(b, l, N_ATT_HEADS, 2, QK_HEAD_DIM)
    av = av.reshape(b, l, N_ATT_HEADS, V_HEAD_DIM)
    if rows is None:
        keys, vals = ak, av
    else:
        aq = axial_rope(aq, rows, cols)
        keys = jnp.concatenate([ctx_k, axial_rope(ak, rows, cols)], axis=1)
        vals = jnp.concatenate([ctx_v, av], axis=1)
    att = diff_attention(aq, keys, vals, p['lam'])
    att = head_rms_norm(att, p['att_g']) * (1.0 - p['lam_init'])
    if s0_fwd is None:
        s0_fwd = jnp.zeros((b, N_RNN_HEADS, RNN_HEAD_DIM, RNN_HEAD_DIM), jnp.float32)
        s0_bwd = s0_fwd
    rnn, s_fwd, s_bwd = hgrn2_bidir(rq, ri, rf, rb, p['lb_fwd'], p['lb_bwd'], s0_fwd, s0_bwd)
    out_gate = jax.nn.silu(rg.astype(jnp.float32)).reshape(b, l, N_RNN_HEADS, RNN_HEAD_DIM)
    rnn = head_rms_norm(rnn, p['rnn_g']) * out_gate
    mixed = jnp.concatenate([att.reshape(b, l, ATT_WIDTH), rnn.reshape(b, l, RNN_WIDTH)], axis=-1)
    mix = mixed.astype(x.dtype) @ p['w_out']
    x = layer_norm(DEEPNORM_ALPHA * x + gate1 * mix, p['ln1_g'], p['ln1_b'])
    xm2 = x * (1 + scale2) + shift2
    ffn = conv_ffn(xm2, p['w_up'], p['conv_w'], p['conv_b'], p['w_down'])
    x = layer_norm(DEEPNORM_ALPHA * x + gate2 * ffn, p['ln2_g'], p['ln2_b'])
    return x, ak, av, s_fwd, s_bwd


def setup_inputs(seed: int = 0) -> dict:
    key = jax.random.key(seed)
    ks = jax.random.split(key, 27)
    D = D_MODEL

    def nrm(i, shape, s):
        return jax.random.normal(ks[i], shape, jnp.float32) * s

    col_scale = jnp.concatenate([
        jnp.ones((2 * ATT_WIDTH,), jnp.float32),
        jnp.full((ATT_WIDTH,), DEEPNORM_BETA, jnp.float32),
        jnp.ones((3 * RNN_WIDTH,), jnp.float32),
        jnp.full((RNN_WIDTH,), DEEPNORM_BETA, jnp.float32),
        jnp.ones((RNN_WIDTH,), jnp.float32)])
    return {
        'x_prompt': nrm(0, (BATCH, SEQ, D), 1.0),
        'x_sample': nrm(1, (DEC_BATCH, DEC_SEQ, D), 1.0),
        'cache_k': nrm(2, (DEC_BATCH, DEPTH, PAST_LEN, N_ATT_HEADS, 2, QK_HEAD_DIM), 1.0),
        'cache_v': nrm(3, (DEC_BATCH, DEPTH, PAST_LEN, N_ATT_HEADS, V_HEAD_DIM), DEEPNORM_BETA),
        'state_rnn': nrm(4, (DEC_BATCH, DEPTH, 2, N_RNN_HEADS, RNN_HEAD_DIM, RNN_HEAD_DIM), 1.0),
        'c': nrm(5, (DEC_BATCH, D), 1.0),
        'c_ctx': nrm(6, (D,), 1.0),
        'w_ada': nrm(7, (DEPTH, D, N_MOD * D), 0.5 * D ** -0.5),
        'b_ada': nrm(8, (DEPTH, N_MOD * D), 0.01),
        'w_in': nrm(9, (DEPTH, D, IN_COLS), D ** -0.5) * col_scale,
        'lambda_q1': nrm(10, (DEPTH, QK_HEAD_DIM), 0.1),
        'lambda_k1': nrm(11, (DEPTH, QK_HEAD_DIM), 0.1),
        'lambda_q2': nrm(12, (DEPTH, QK_HEAD_DIM), 0.1),
        'lambda_k2': nrm(13, (DEPTH, QK_HEAD_DIM), 0.1),
        'lb_fwd_logits': nrm(14, (DEPTH + 1, RNN_WIDTH), 0.5),
        'lb_bwd_logits': nrm(15, (DEPTH + 1, RNN_WIDTH), 0.5),
        'att_norm_g': 1.0 + nrm(16, (DEPTH, ATT_WIDTH), 0.02),
        'rnn_norm_g': 1.0 + nrm(17, (DEPTH, RNN_WIDTH), 0.02),
        'w_out': nrm(18, (DEPTH, D, D), D ** -0.5 * DEEPNORM_BETA),
        'ln1_g': 1.0 + nrm(19, (DEPTH, D), 0.02),
        'ln1_b': nrm(20, (DEPTH, D), 0.02),
        'w_up': nrm(21, (DEPTH, D, 2 * D_FF), D ** -0.5),
        'conv_w': nrm(22, (DEPTH, CONV_W, 2 * D_FF), 0.5),
        'conv_b': nrm(23, (DEPTH, 2 * D_FF), 0.02),
        'w_down': nrm(24, (DEPTH, D_FF, D), D_FF ** -0.5 * DEEPNORM_BETA),
        'ln2_g': 1.0 + nrm(25, (DEPTH, D), 0.02),
        'ln2_b': nrm(26, (DEPTH, D), 0.02),
    }


def reference(x_prompt, x_sample, cache_k, cache_v, state_rnn, c, c_ctx, w_ada, b_ada, w_in,
              lambda_q1, lambda_k1, lambda_q2, lambda_k2, lb_fwd_logits, lb_bwd_logits,
              att_norm_g, rnn_norm_g, w_out, ln1_g, ln1_b, w_up, conv_w, conv_b, w_down,
              ln2_g, ln2_b):
    n_rows = x_sample.shape[1] // GRID_W
    t = jnp.arange(n_rows * GRID_W)
    rows = t // GRID_W
    cols = t % GRID_W
    lb_fwd = jnp.cumsum(jax.nn.softmax(lb_fwd_logits.astype(jnp.float32), axis=0), axis=0)
    lb_bwd = jnp.cumsum(jax.nn.softmax(lb_bwd_logits.astype(jnp.float32), axis=0), axis=0)

    y_p, y_s = x_prompt, x_sample
    new_k, new_v, new_s = [], [], []
    for l in range(DEPTH):
        lam_init = 0.8 - 0.6 * math.exp(-0.3 * l)
        lam = (jnp.exp(jnp.sum(lambda_q1[l].astype(jnp.float32) * lambda_k1[l].astype(jnp.float32)))
               - jnp.exp(jnp.sum(lambda_q2[l].astype(jnp.float32) * lambda_k2[l].astype(jnp.float32)))
               + lam_init)
        p = {'w_in': w_in[l], 'lam': lam, 'lam_init': lam_init, 'lb_fwd': lb_fwd[l], 'lb_bwd': lb_bwd[l],
             'att_g': att_norm_g[l], 'rnn_g': rnn_norm_g[l], 'w_out': w_out[l],
             'ln1_g': ln1_g[l], 'ln1_b': ln1_b[l], 'w_up': w_up[l], 'conv_w': conv_w[l],
             'conv_b': conv_b[l], 'w_down': w_down[l], 'ln2_g': ln2_g[l], 'ln2_b': ln2_b[l]}
        mod_ctx = (jax.nn.silu(c_ctx) @ w_ada[l] + b_ada[l])[None, None, :]
        mod_lat = (jax.nn.silu(c) @ w_ada[l] + b_ada[l])[:, None, :]
        y_p, k_l, v_l, sf_l, sb_l = trunk_layer(y_p, mod_ctx, p)
        new_k.append(k_l)
        new_v.append(v_l)
        new_s.append(jnp.stack([sf_l, sb_l], axis=1).astype(x_prompt.dtype))
        y_s, _, _, _, _ = trunk_layer(y_s, mod_lat, p, rows, cols, cache_k[:, l], cache_v[:, l],
                                      state_rnn[:, l, 0].astype(jnp.float32),
                                      state_rnn[:, l, 1].astype(jnp.float32))
    new_cache_k = jnp.stack(new_k, axis=1)
    new_cache_v = jnp.stack(new_v, axis=1)
    new_state_rnn = jnp.stack(new_s, axis=1)
    return (y_p, y_s, new_cache_k, new_cache_v, new_state_rnn)
```

```python
import functools
import math

import jax
import jax.numpy as jnp
import numpy as np
from jax import lax
from jax.experimental import pallas as pl
from jax.experimental.pallas import tpu as pltpu

D_MODEL = 1024
GRID_W = 64
N_HEADS = 4
HEAD_DIM = 128
QK_DIM = 64
GROUP_W = 512
N_GROUPS = 8
D_FF = 2816
N_MOD = 6
ROPE_BASE = 10000.0
DEPTH = 1
DEEPNORM_ALPHA = (2.0 * DEPTH) ** 0.25
NORM_EPS = 1e-5
LAM_INIT = 0.8 - 0.6 * math.exp(-0.3 * 0)

V7X_VMEM_BYTES = 64 * 1024 * 1024
VMEM_LIMIT = 56 * 1024 * 1024

ROWS_PROJ = 512
ROWS_FFN = 1024
FF_TILE = 256
Q_ROWS = 256
HGRN_ROWS = 256
HGRN_LEVELS = int(math.log2(HGRN_ROWS))

F32 = jnp.float32
BF16 = jnp.bfloat16


def _params(semantics):
    return pltpu.CompilerParams(dimension_semantics=semantics, vmem_limit_bytes=VMEM_LIMIT)


def _dot(a, b):
    return jnp.dot(a, b, preferred_element_type=F32)


def _dot_nt(a, b):
    return lax.dot_general(a, b, (((1,), (1,)), ((), ())), preferred_element_type=F32)


def _dot_tn(a, b):
    return lax.dot_general(a, b, (((0,), (0,)), ((), ())), preferred_element_type=F32)


def _silu(x):
    return x * jax.nn.sigmoid(x)


def _layer_norm(y, g, b):
    mu = jnp.mean(y, axis=-1, keepdims=True)
    d = y - mu
    var = jnp.mean(d * d, axis=-1, keepdims=True)
    return d * lax.rsqrt(var + NORM_EPS) * g + b


def _mod_kernel(c_ref, w_ref, b_ref, o_ref):
    s = _silu(c_ref[...]).astype(BF16)
    o_ref[...] = _dot(s, w_ref[...].astype(BF16)) + b_ref[...]


def _mod_call(cond, w_ada, b_ada):
    n, d = cond.shape
    cols = w_ada.shape[1]
    tile = 512
    return pl.pallas_call(
        _mod_kernel,
        grid=(cols // tile,),
        in_specs=[pl.BlockSpec((n, d), lambda j: (0, 0)),
                  pl.BlockSpec((d, tile), lambda j: (0, j)),
                  pl.BlockSpec((1, tile), lambda j: (0, j))],
        out_specs=pl.BlockSpec((n, tile), lambda j: (0, j)),
        out_shape=jax.ShapeDtypeStruct((n, cols), F32),
        compiler_params=_params(("arbitrary",)),
        name="mod",
    )(cond, w_ada, b_ada)


def _rope(x, cos_ref, sin_a_ref, sin_b_ref):
    parts = []
    for c in range(GROUP_W // 128):
        sl = slice(c * 128, (c + 1) * 128)
        xc = x[:, sl]
        parts.append(xc * cos_ref[:, sl]
                     + pltpu.roll(xc, 128 - 16, axis=1) * sin_a_ref[:, sl]
                     + pltpu.roll(xc, 16, axis=1) * sin_b_ref[:, sl])
    return jnp.concatenate(parts, axis=1)


def _inproj_kernel(*refs, latent):
    if latent:
        (x_ref, mod_ref, w_ref, cos_ref, sin_a_ref, sin_b_ref,
         q_ref, k_ref, v_ref, rq_ref, xf_ref, xb_ref, ri_ref, rg_ref) = refs
    else:
        (x_ref, mod_ref, w_ref,
         q_ref, k_ref, v_ref, rq_ref, xf_ref, xb_ref, ri_ref, rg_ref, kraw_ref, vraw_ref) = refs
    m = mod_ref[0]
    xm = (x_ref[...] * (1.0 + m[1:2, :]) + m[0:1, :]).astype(BF16)

    def proj(g):
        return _dot(xm, w_ref[:, g * GROUP_W:(g + 1) * GROUP_W])

    aq = proj(0)
    ak = proj(1)
    av = proj(2)
    if latent:
        aq = _rope(aq, cos_ref, sin_a_ref, sin_b_ref)
        ak = _rope(ak, cos_ref, sin_a_ref, sin_b_ref)
    else:
        kraw_ref[...] = ak
        vraw_ref[...] = av
    q_ref[...] = (aq * (QK_DIM ** -0.5)).astype(BF16)
    k_ref[...] = ak.astype(BF16)
    v_ref[...] = av.astype(BF16)
    rq_ref[...] = proj(3)
    xf_ref[...] = proj(4)
    xb_ref[...] = proj(5)
    ri_ref[...] = proj(6).astype(BF16)
    rg_ref[...] = _silu(proj(7))


def _inproj_call(x2d, mod3, w_in, rope_tabs, *, latent, seq_len):
    t = x2d.shape[0]
    rb = ROWS_PROJ
    blocks_per_seq = seq_len // rb if latent else 1

    def mod_idx(i):
        return ((1 + i // blocks_per_seq) if latent else 0, 0, 0)

    row_spec = lambda w: pl.BlockSpec((rb, w), lambda i: (i, 0))
    in_specs = [row_spec(D_MODEL),
                pl.BlockSpec((1, N_MOD, D_MODEL), mod_idx),
                pl.BlockSpec(w_in.shape, lambda i: (0, 0))]
    args = [x2d, mod3, w_in]
    if latent:
        in_specs += [pl.BlockSpec((rb, GROUP_W), lambda i: (i % blocks_per_seq, 0))] * 3
        args += list(rope_tabs)
    dts = [BF16, BF16, BF16, F32, F32, F32, BF16, F32]
    if not latent:
        dts += [F32, F32]
    return pl.pallas_call(
        functools.partial(_inproj_kernel, latent=latent),
        grid=(t // rb,),
        in_specs=in_specs,
        out_specs=[row_spec(GROUP_W) for _ in dts],
        out_shape=[jax.ShapeDtypeStruct((t, GROUP_W), dt) for dt in dts],
        compiler_params=_params(("arbitrary",)),
        name="inproj_lat" if latent else "inproj_ctx",
    )(*args)


def _attn_kernel(*refs, has_cache):
    if has_cache:
        (q_ref, k_ref, v_ref, ck_ref, cv_ref, lq1, lk1, lq2, lk2, g_ref, o_ref) = refs
    else:
        (q_ref, k_ref, v_ref, lq1, lk1, lq2, lk2, g_ref, o_ref) = refs
    lam = (jnp.exp(jnp.sum(lq1[...] * lk1[...], axis=-1, keepdims=True))
           - jnp.exp(jnp.sum(lq2[...] * lk2[...], axis=-1, keepdims=True)) + LAM_INIT)
    qb = q_ref.shape[1]
    lane = lax.broadcasted_iota(jnp.int32, (qb, HEAD_DIM), 1)
    first_map = lane < QK_DIM
    zero = jnp.zeros((), BF16)
    for h in range(N_HEADS):
        sl = slice(h * HEAD_DIM, (h + 1) * HEAD_DIM)
        qh = q_ref[0, :, sl]
        qq = jnp.concatenate([jnp.where(first_map, qh, zero), jnp.where(first_map, zero, qh)], axis=0)
        kn = k_ref[0, :, sl]
        vn = v_ref[0, :, sl]
        s_n = _dot_nt(qq, kn)
        mx = jnp.max(s_n, axis=-1, keepdims=True)
        if has_cache:
            kc = ck_ref[0, :, sl].astype(BF16)
            vc = cv_ref[0, :, sl].astype(BF16)
            s_c = _dot_nt(qq, kc)
            mx = jnp.maximum(mx, jnp.max(s_c, axis=-1, keepdims=True))
        e_n = jnp.exp(s_n - mx)
        den = jnp.sum(e_n, axis=-1, keepdims=True)
        if has_cache:
            e_c = jnp.exp(s_c - mx)
            den = den + jnp.sum(e_c, axis=-1, keepdims=True)
        inv = 1.0 / den
        w1 = inv[:qb]
        w2 = inv[qb:] * lam
        o = _dot((e_n[:qb] * w1 - e_n[qb:] * w2).astype(BF16), vn)
        if has_cache:
            o = o + _dot((e_c[:qb] * w1 - e_c[qb:] * w2).astype(BF16), vc)
        o = o * lax.rsqrt(jnp.mean(o * o, axis=-1, keepdims=True) + NORM_EPS)
        o_ref[0, :, sl] = (o * g_ref[:, sl] * (1.0 - LAM_INIT)).astype(BF16)


def _attn_call(q, k, v, cache, lams, att_g):
    b, l, _ = q.shape
    qb = Q_ROWS
    full = lambda a: pl.BlockSpec((1,) + a.shape[1:], lambda i, j: (i, 0, 0))
    in_specs = [pl.BlockSpec((1, qb, GROUP_W), lambda i, j: (i, j, 0)), full(k), full(v)]
    args = [q, k, v]
    if cache is not None:
        in_specs += [full(cache[0]), full(cache[1])]
        args += list(cache)
    in_specs += [pl.BlockSpec((1, QK_DIM), lambda i, j: (0, 0))] * 4
    in_specs += [pl.BlockSpec((1, GROUP_W), lambda i, j: (0, 0))]
    args += list(lams) + [att_g]
    return pl.pallas_call(
        functools.partial(_attn_kernel, has_cache=cache is not None),
        grid=(b, l // qb),
        in_specs=in_specs,
        out_specs=pl.BlockSpec((1, qb, GROUP_W), lambda i, j: (i, j, 0)),
        out_shape=jax.ShapeDtypeStruct((b, l, GROUP_W), BF16),
        compiler_params=_params(("arbitrary", "arbitrary")),
        name="attn_lat" if cache is not None else "attn_ctx",
    )(*args)


def _hgrn_masks():
    r = np.arange(HGRN_ROWS)[:, None]
    c = np.arange(HGRN_ROWS)[None, :]
    levels = []
    for lv in range(HGRN_LEVELS):
        same = (r >> (lv + 1)) == (c >> (lv + 1))
        levels.append(same & (((r >> lv) & 1) == 1) & (((c >> lv) & 1) == 0))
    fwd = [r == c] + levels
    bwd = [m.T for m in levels]
    return np.stack(fwd + bwd).astype(np.float32)


def _hgrn_block(q, k, g, v, st, mask_ref, row, *, reverse):
    n = HGRN_ROWS
    if reverse:
        pre, suf = jnp.zeros_like(g), g
    else:
        pre, suf = g, jnp.zeros_like(g)
    tot = g
    acc = _dot_nt(q.astype(BF16), k.astype(BF16)) * mask_ref[0]
    for lv in range(HGRN_LEVELS):
        b = 1 << lv
        upper = ((row >> lv) & 1) == 1
        e = jnp.exp(jnp.where(upper, pre, suf))
        a = _dot_nt((q * e).astype(BF16), (k * e).astype(BF16))
        acc = acc + a * mask_ref[(1 + HGRN_LEVELS + lv) if reverse else (1 + lv)]
        sib = jnp.where(upper, pltpu.roll(tot, b, axis=0), pltpu.roll(tot, n - b, axis=0))
        pre = pre + jnp.where(upper, sib, 0.0)
        suf = suf + jnp.where(upper, 0.0, sib)
        tot = tot + sib
    q_dec, k_dec = (suf, pre) if reverse else (pre, suf)
    o = _dot(acc.astype(BF16), v) + _dot_nt((q * jnp.exp(q_dec)).astype(BF16), st.astype(BF16))
    st_new = st * jnp.exp(tot[0:1, :]) + _dot_tn(v, (k * jnp.exp(k_dec)).astype(BF16))
    return o, st_new


def _hgrn_kernel(*refs, n_blocks, has_state):
    if has_state:
        (q_ref, v_ref, xf_ref, xb_ref, gate_ref, lbf_ref, lbb_ref, g_ref, mask_ref, s0_ref,
         o_ref, acc_ref) = refs
    else:
        (q_ref, v_ref, xf_ref, xb_ref, gate_ref, lbf_ref, lbb_ref, g_ref, mask_ref,
         o_ref, sout_ref, acc_ref) = refs
    n = HGRN_ROWS
    row = lax.broadcasted_iota(jnp.int32, (n, HEAD_DIM), 0)

    def lower_bound(ref):
        l0, l1 = ref[0:1, :], ref[1:2, :]
        mx = jnp.maximum(l0, l1)
        e0, e1 = jnp.exp(l0 - mx), jnp.exp(l1 - mx)
        return e0 / (e0 + e1)

    for d, (x_ref, lb_ref) in enumerate(((xf_ref, lbf_ref), (xb_ref, lbb_ref))):
        reverse = d == 1
        lb = lower_bound(lb_ref)
        st0 = s0_ref[0, d, 0].T if has_state else jnp.zeros((HEAD_DIM, HEAD_DIM), F32)

        def body(j, st, x_ref=x_ref, lb=lb, reverse=reverse):
            blk = (n_blocks - 1 - j) if reverse else j
            rows = pl.ds(pl.multiple_of(blk * n, n), n)
            f = lb + (1.0 - lb) * jax.nn.sigmoid(x_ref[0, rows, :])
            o, st = _hgrn_block(q_ref[0, rows, :], 1.0 - f, jnp.log(f), v_ref[0, rows, :], st,
                                mask_ref, row, reverse=reverse)
            if reverse:
                acc_ref[rows, :] += o
            else:
                acc_ref[rows, :] = o
            return st

        st = lax.fori_loop(0, n_blocks, body, st0)
        if not has_state:
            sout_ref[0, d, 0] = st.T
    o = acc_ref[...]
    o = o * lax.rsqrt(jnp.mean(o * o, axis=-1, keepdims=True) + NORM_EPS)
    o_ref[0] = (o * g_ref[...] * gate_ref[0]).astype(BF16)


def _hgrn_call(rq, ri, xf, xb, gate, lbf, lbb, rnn_g, masks, s0):
    b, l, _ = rq.shape
    head = lambda: pl.BlockSpec((1, l, HEAD_DIM), lambda i, h: (i, 0, h))
    per_head = lambda rows: pl.BlockSpec((rows, HEAD_DIM), lambda i, h: (0, h))
    state_spec = pl.BlockSpec((1, 2, 1, HEAD_DIM, HEAD_DIM), lambda i, h: (i, 0, h, 0, 0))
    in_specs = [head(), head(), head(), head(), head(), per_head(2), per_head(2), per_head(1),
                pl.BlockSpec(masks.shape, lambda i, h: (0, 0, 0))]
    args = [rq, ri, xf, xb, gate, lbf, lbb, rnn_g, masks]
    out_specs = [head()]
    out_shape = [jax.ShapeDtypeStruct((b, l, GROUP_W), BF16)]
    if s0 is not None:
        in_specs.append(state_spec)
        args.append(s0)
    else:
        out_specs.append(state_spec)
        out_shape.append(jax.ShapeDtypeStruct((b, 2, N_HEADS, HEAD_DIM, HEAD_DIM), F32))
    return pl.pallas_call(
        functools.partial(_hgrn_kernel, n_blocks=l // HGRN_ROWS, has_state=s0 is not None),
        grid=(b, N_HEADS),
        in_specs=in_specs,
        out_specs=out_specs,
        out_shape=out_shape,
        scratch_shapes=[pltpu.VMEM((l, HEAD_DIM), F32)],
        compiler_params=_params(("arbitrary", "arbitrary")),
        name="hgrn_lat" if s0 is not None else "hgrn_ctx",
    )(*args)


def _outproj_kernel(att_ref, rnn_ref, x_ref, mod_ref, w_ref, g_ref, b_ref, x1_ref, xm2_ref):
    m = mod_ref[0]
    mix = _dot(att_ref[...], w_ref[0:GROUP_W, :]) + _dot(rnn_ref[...], w_ref[GROUP_W:2 * GROUP_W, :])
    x1 = _layer_norm(DEEPNORM_ALPHA * x_ref[...] + m[2:3, :] * mix, g_ref[...], b_ref[...])
    x1_ref[...] = x1
    xm2_ref[...] = (x1 * (1.0 + m[4:5, :]) + m[3:4, :]).astype(BF16)


def _outproj_call(att, rnn, x2d, mod3, w_out, ln_g, ln_b, *, latent, seq_len):
    t = x2d.shape[0]
    rb = ROWS_PROJ
    blocks_per_seq = seq_len // rb if latent else 1

    def mod_idx(i):
        return ((1 + i // blocks_per_seq) if latent else 0, 0, 0)

    row_spec = lambda w: pl.BlockSpec((rb, w), lambda i: (i, 0))
    vec = pl.BlockSpec((1, D_MODEL), lambda i: (0, 0))
    return pl.pallas_call(
        _outproj_kernel,
        grid=(t // rb,),
        in_specs=[row_spec(GROUP_W), row_spec(GROUP_W), row_spec(D_MODEL),
                  pl.BlockSpec((1, N_MOD, D_MODEL), mod_idx),
                  pl.BlockSpec(w_out.shape, lambda i: (0, 0)), vec, vec],
        out_specs=[row_spec(D_MODEL), row_spec(D_MODEL)],
        out_shape=[jax.ShapeDtypeStruct((t, D_MODEL), F32), jax.ShapeDtypeStruct((t, D_MODEL), BF16)],
        compiler_params=_params(("arbitrary",)),
        name="outproj_lat" if latent else "outproj_ctx",
    )(att, rnn, x2d, mod3, w_out, ln_g, ln_b)


def _ffn_kernel(xm2_ref, x1_ref, mod_ref, wa_ref, wu_ref, cwa_ref, cwu_ref, cba_ref, cbu_ref, wd_ref,
                g_ref, b_ref, o_ref, acc_ref, *, seq_len):
    j = pl.program_id(1)
    rows = xm2_ref.shape[0]
    assert seq_len & (seq_len - 1) == 0
    pos = lax.broadcasted_iota(jnp.int32, (rows, FF_TILE), 0) & (seq_len - 1)
    seq_first = pos == 0
    seq_last = pos == seq_len - 1

    def conv(h, cw_ref, cb_ref):
        prev = jnp.where(seq_first, 0.0, pltpu.roll(h, 1, axis=0))
        nxt = jnp.where(seq_last, 0.0, pltpu.roll(h, rows - 1, axis=0))
        return prev * cw_ref[0:1, :] + h * cw_ref[1:2, :] + nxt * cw_ref[2:3, :] + cb_ref[...]

    xm2 = xm2_ref[...]
    a = conv(_dot(xm2, wa_ref[...]), cwa_ref, cba_ref)
    u = conv(_dot(xm2, wu_ref[...]), cwu_ref, cbu_ref)
    part = _dot((_silu(a) * u).astype(BF16), wd_ref[...])

    @pl.when(j == 0)
    def _():
        acc_ref[...] = part

    @pl.when(j > 0)
    def _():
        acc_ref[...] += part

    @pl.when(j == pl.num_programs(1) - 1)
    def _():
        m = mod_ref[0]
        y = DEEPNORM_ALPHA * x1_ref[...] + m[5:6, :] * acc_ref[...]
        o_ref[...] = _layer_norm(y, g_ref[...], b_ref[...])


def _ffn_call(xm2, x1, mod3, w_up, conv_w, conv_b, w_down, ln_g, ln_b, *, latent, seq_len):
    t = xm2.shape[0]
    rb = ROWS_FFN
    n_ff = D_FF // FF_TILE

    def mod_idx(i, j):
        return ((1 + i * rb // seq_len) if latent else 0, 0, 0)

    row_spec = lambda: pl.BlockSpec((rb, D_MODEL), lambda i, j: (i, 0))
    col_a = lambda r: pl.BlockSpec((r, FF_TILE), lambda i, j: (0, j))
    col_u = lambda r: pl.BlockSpec((r, FF_TILE), lambda i, j: (0, n_ff + j))
    vec = pl.BlockSpec((1, D_MODEL), lambda i, j: (0, 0))
    return pl.pallas_call(
        functools.partial(_ffn_kernel, seq_len=seq_len),
        grid=(t // rb, n_ff),
        in_specs=[row_spec(), row_spec(), pl.BlockSpec((1, N_MOD, D_MODEL), mod_idx),
                  col_a(D_MODEL), col_u(D_MODEL), col_a(3), col_u(3), col_a(1), col_u(1),
                  pl.BlockSpec((FF_TILE, D_MODEL), lambda i, j: (j, 0)), vec, vec],
        out_specs=row_spec(),
        out_shape=jax.ShapeDtypeStruct((t, D_MODEL), F32),
        scratch_shapes=[pltpu.VMEM((rb, D_MODEL), F32)],
        compiler_params=_params(("arbitrary", "arbitrary")),
        name="ffn_lat" if latent else "ffn_ctx",
    )(xm2, x1, mod3, w_up, w_up, conv_w, conv_w, conv_b, conv_b, w_down, ln_g, ln_b)


def _rope_tables(seq_len):
    quarter = QK_DIM // 4
    freqs = 1.0 / (ROPE_BASE ** (jnp.arange(quarter, dtype=F32) / quarter))
    t = jnp.arange(seq_len)
    ang_r = (t // GRID_W).astype(F32)[:, None] * freqs
    ang_c = (t % GRID_W).astype(F32)[:, None] * freqs
    zeros = jnp.zeros_like(ang_r)

    def tile(parts):
        return jnp.tile(jnp.concatenate(parts, axis=-1), (1, GROUP_W // QK_DIM))

    cos = tile([jnp.cos(ang_r), jnp.cos(ang_r), jnp.cos(ang_c), jnp.cos(ang_c)])
    sin_a = tile([-jnp.sin(ang_r), zeros, -jnp.sin(ang_c), zeros])
    sin_b = tile([zeros, jnp.sin(ang_r), zeros, jnp.sin(ang_c)])
    return cos, sin_a, sin_b


def kernel(x_prompt, x_sample, cache_k, cache_v, state_rnn, c, c_ctx, w_ada, b_ada, w_in, lambda_q1, lambda_k1, lambda_q2, lambda_k2, lb_fwd_logits, lb_bwd_logits, att_norm_g, rnn_norm_g, w_out, ln1_g, ln1_b, w_up, conv_w, conv_b, w_down, ln2_g, ln2_b):
    assert w_ada.shape[0] == DEPTH
    bp, lp, d = x_prompt.shape
    bs, ls, _ = x_sample.shape
    past = cache_k.shape[2]

    cond = jnp.zeros((16, d), F32).at[0].set(c_ctx).at[1:1 + bs].set(c)
    mod3 = _mod_call(cond, w_ada[0], b_ada).reshape(16, N_MOD, d)

    w_in_b = w_in[0].astype(BF16)
    w_out_b = w_out[0].astype(BF16)
    w_up_b = w_up[0].astype(BF16)
    w_down_b = w_down[0].astype(BF16)
    lams = (lambda_q1, lambda_k1, lambda_q2, lambda_k2)
    masks = jnp.asarray(_hgrn_masks())
    row = lambda a: a.reshape(1, -1)

    def trunk(x, *, latent, cache, s0):
        b, l, _ = x.shape
        x2d = x.reshape(b * l, d)
        outs = _inproj_call(x2d, mod3, w_in_b, _rope_tables(l) if latent else None,
                            latent=latent, seq_len=l)
        q, k, v, rq, xf, xb, ri, rg = (o.reshape(b, l, GROUP_W) for o in outs[:8])
        att = _attn_call(q, k, v, cache, lams, att_norm_g)
        hg = _hgrn_call(rq, ri, xf, xb, rg, lb_fwd_logits, lb_bwd_logits, rnn_norm_g, masks, s0)
        rnn = hg[0]
        x1, xm2 = _outproj_call(att.reshape(b * l, GROUP_W), rnn.reshape(b * l, GROUP_W), x2d, mod3,
                                w_out_b, ln1_g, ln1_b, latent=latent, seq_len=l)
        y = _ffn_call(xm2, x1, mod3, w_up_b, conv_w[0], conv_b, w_down_b, ln2_g, ln2_b,
                      latent=latent, seq_len=l)
        return y.reshape(b, l, d), outs[8:], hg[1:]

    y_p, (k_raw, v_raw), (s_new,) = trunk(x_prompt, latent=False, cache=None, s0=None)
    cache = (cache_k.reshape(bs, past, GROUP_W), cache_v.reshape(bs, past, GROUP_W))
    y_s, _, _ = trunk(x_sample, latent=True, cache=cache,
                      s0=state_rnn.reshape(bs, 2, N_HEADS, HEAD_DIM, HEAD_DIM))

    new_cache_k = k_raw.reshape(bp, DEPTH, lp, N_HEADS, 2, QK_DIM)
    new_cache_v = v_raw.reshape(bp, DEPTH, lp, N_HEADS, HEAD_DIM)
    new_state = s_new.reshape(bp, DEPTH, 2, N_HEADS, HEAD_DIM, HEAD_DIM)
    return (y_p, y_s, new_cache_k, new_cache_v, new_state)
```

```python
import functools
import math

import jax
import jax.numpy as jnp
import numpy as np
from jax import lax
from jax.experimental import pallas as pl
from jax.experimental.pallas import tpu as pltpu

D_MODEL = 1024
GRID_W = 64
N_HEADS = 4
HEAD_DIM = 128
QK_DIM = 64
GROUP_W = 512
N_GROUPS = 8
D_FF = 2816
N_MOD = 6
ROPE_BASE = 10000.0
DEPTH = 1
DEEPNORM_ALPHA = (2.0 * DEPTH) ** 0.25
NORM_EPS = 1e-5
LAM_INIT = 0.8 - 0.6 * math.exp(-0.3 * 0)

V7X_VMEM_BYTES = 64 * 1024 * 1024
VMEM_LIMIT = V7X_VMEM_BYTES * 15 // 16

ROWS_PROJ = 512
ROWS_FFN = 1024
FF_TILE = 256
Q_ROWS = 256
HGRN_ROWS = 256
HGRN_LEVELS = int(math.log2(HGRN_ROWS))

F32 = jnp.float32
BF16 = jnp.bfloat16


def _params(semantics):
    return pltpu.CompilerParams(dimension_semantics=semantics, vmem_limit_bytes=VMEM_LIMIT)


def _dot(a, b):
    return jnp.dot(a, b, preferred_element_type=F32)


def _dot_nt(a, b):
    return lax.dot_general(a, b, (((1,), (1,)), ((), ())), preferred_element_type=F32)


def _dot_tn(a, b):
    return lax.dot_general(a, b, (((0,), (0,)), ((), ())), preferred_element_type=F32)


def _silu(x):
    return x * jax.nn.sigmoid(x)


def _layer_norm(y, g, b):
    mu = jnp.mean(y, axis=-1, keepdims=True)
    d = y - mu
    var = jnp.mean(d * d, axis=-1, keepdims=True)
    return d * lax.rsqrt(var + NORM_EPS) * g + b


def _mod_kernel(c_ref, w_ref, b_ref, o_ref):
    s = _silu(c_ref[...]).astype(BF16)
    o_ref[...] = _dot(s, w_ref[...].astype(BF16)) + b_ref[...]


def _mod_call(cond, w_ada, b_ada):
    n, d = cond.shape
    cols = w_ada.shape[1]
    tile = 512
    return pl.pallas_call(
        _mod_kernel,
        grid=(cols // tile,),
        in_specs=[pl.BlockSpec((n, d), lambda j: (0, 0)),
                  pl.BlockSpec((d, tile), lambda j: (0, j)),
                  pl.BlockSpec((1, tile), lambda j: (0, j))],
        out_specs=pl.BlockSpec((n, tile), lambda j: (0, j)),
        out_shape=jax.ShapeDtypeStruct((n, cols), F32),
        compiler_params=_params(("arbitrary",)),
        name="mod",
    )(cond, w_ada, b_ada)


def _rope(x, cos_ref, sin_a_ref, sin_b_ref):
    parts = []
    for c in range(GROUP_W // 128):
        sl = slice(c * 128, (c + 1) * 128)
        xc = x[:, sl]
        parts.append(xc * cos_ref[:, sl]
                     + pltpu.roll(xc, 128 - 16, axis=1) * sin_a_ref[:, sl]
                     + pltpu.roll(xc, 16, axis=1) * sin_b_ref[:, sl])
    return jnp.concatenate(parts, axis=1)


def _inproj_kernel(*refs, latent):
    if latent:
        (x_ref, mod_ref, w_ref, cos_ref, sin_a_ref, sin_b_ref,
         q_ref, k_ref, v_ref, rq_ref, xf_ref, xb_ref, ri_ref, rg_ref) = refs
    else:
        (x_ref, mod_ref, w_ref,
         q_ref, k_ref, v_ref, rq_ref, xf_ref, xb_ref, ri_ref, rg_ref, kraw_ref, vraw_ref) = refs
    m = mod_ref[0]
    xm = (x_ref[...] * (1.0 + m[1:2, :]) + m[0:1, :]).astype(BF16)

    def proj(g):
        return _dot(xm, w_ref[:, g * GROUP_W:(g + 1) * GROUP_W])

    aq = proj(0)
    ak = proj(1)
    av = proj(2)
    if latent:
        aq = _rope(aq, cos_ref, sin_a_ref, sin_b_ref)
        ak = _rope(ak, cos_ref, sin_a_ref, sin_b_ref)
    else:
        kraw_ref[...] = ak
        vraw_ref[...] = av
    q_ref[...] = (aq * (QK_DIM ** -0.5)).astype(BF16)
    k_ref[...] = ak.astype(BF16)
    v_ref[...] = av.astype(BF16)
    rq_ref[...] = proj(3)
    xf_ref[...] = proj(4)
    xb_ref[...] = proj(5)
    ri_ref[...] = proj(6).astype(BF16)
    rg_ref[...] = _silu(proj(7))


def _inproj_call(x2d, mod3, w_in, rope_tabs, *, latent, seq_len):
    t = x2d.shape[0]
    rb = ROWS_PROJ
    blocks_per_seq = seq_len // rb if latent else 1

    def mod_idx(i):
        return ((1 + i // blocks_per_seq) if latent else 0, 0, 0)

    row_spec = lambda w: pl.BlockSpec((rb, w), lambda i: (i, 0))
    in_specs = [row_spec(D_MODEL),
                pl.BlockSpec((1, N_MOD, D_MODEL), mod_idx),
                pl.BlockSpec(w_in.shape, lambda i: (0, 0))]
    args = [x2d, mod3, w_in]
    if latent:
        in_specs += [pl.BlockSpec((rb, GROUP_W), lambda i: (i % blocks_per_seq, 0))] * 3
        args += list(rope_tabs)
    dts = [BF16, BF16, BF16, F32, F32, F32, BF16, F32]
    if not latent:
        dts += [F32, F32]
    return pl.pallas_call(
        functools.partial(_inproj_kernel, latent=latent),
        grid=(t // rb,),
        in_specs=in_specs,
        out_specs=[row_spec(GROUP_W) for _ in dts],
        out_shape=[jax.ShapeDtypeStruct((t, GROUP_W), dt) for dt in dts],
        compiler_params=_params(("arbitrary",)),
        name="inproj_lat" if latent else "inproj_ctx",
    )(*args)


def _attn_kernel(*refs, has_cache):
    if has_cache:
        (q_ref, k_ref, v_ref, ck_ref, cv_ref, lq1, lk1, lq2, lk2, g_ref, o_ref) = refs
    else:
        (q_ref, k_ref, v_ref, lq1, lk1, lq2, lk2, g_ref, o_ref) = refs
    lam = (jnp.exp(jnp.sum(lq1[...] * lk1[...], axis=-1, keepdims=True))
           - jnp.exp(jnp.sum(lq2[...] * lk2[...], axis=-1, keepdims=True)) + LAM_INIT)
    qb = q_ref.shape[1]
    lane = lax.broadcasted_iota(jnp.int32, (qb, HEAD_DIM), 1)
    first_map = lane < QK_DIM
    zero = jnp.zeros((), BF16)
    for h in range(N_HEADS):
        sl = slice(h * HEAD_DIM, (h + 1) * HEAD_DIM)
        qh = q_ref[0, :, sl]
        qq = jnp.concatenate([jnp.where(first_map, qh, zero), jnp.where(first_map, zero, qh)], axis=0)
        kn = k_ref[0, :, sl]
        vn = v_ref[0, :, sl]
        s_n = _dot_nt(qq, kn)
        mx = jnp.max(s_n, axis=-1, keepdims=True)
        if has_cache:
            kc = ck_ref[0, :, sl].astype(BF16)
            vc = cv_ref[0, :, sl].astype(BF16)
            s_c = _dot_nt(qq, kc)
            mx = jnp.maximum(mx, jnp.max(s_c, axis=-1, keepdims=True))
        e_n = jnp.exp(s_n - mx)
        den = jnp.sum(e_n, axis=-1, keepdims=True)
        if has_cache:
            e_c = jnp.exp(s_c - mx)
            den = den + jnp.sum(e_c, axis=-1, keepdims=True)
        inv = 1.0 / den
        w1 = inv[:qb]
        w2 = inv[qb:] * lam
        o = _dot((e_n[:qb] * w1 - e_n[qb:] * w2).astype(BF16), vn)
        if has_cache:
            o = o + _dot((e_c[:qb] * w1 - e_c[qb:] * w2).astype(BF16), vc)
        o = o * lax.rsqrt(jnp.mean(o * o, axis=-1, keepdims=True) + NORM_EPS)
        o_ref[0, :, sl] = (o * g_ref[:, sl] * (1.0 - LAM_INIT)).astype(BF16)


def _attn_call(q, k, v, cache, lams, att_g):
    b, l, _ = q.shape
    qb = Q_ROWS
    full = lambda a: pl.BlockSpec((1,) + a.shape[1:], lambda i, j: (i, 0, 0))
    in_specs = [pl.BlockSpec((1, qb, GROUP_W), lambda i, j: (i, j, 0)), full(k), full(v)]
    args = [q, k, v]
    if cache is not None:
        in_specs += [full(cache[0]), full(cache[1])]
        args += list(cache)
    in_specs += [pl.BlockSpec((1, QK_DIM), lambda i, j: (0, 0))] * 4
    in_specs += [pl.BlockSpec((1, GROUP_W), lambda i, j: (0, 0))]
    args += list(lams) + [att_g]
    return pl.pallas_call(
        functools.partial(_attn_kernel, has_cache=cache is not None),
        grid=(b, l // qb),
        in_specs=in_specs,
        out_specs=pl.BlockSpec((1, qb, GROUP_W), lambda i, j: (i, j, 0)),
        out_shape=jax.ShapeDtypeStruct((b, l, GROUP_W), BF16),
        compiler_params=_params(("arbitrary", "arbitrary")),
        name="attn_lat" if cache is not None else "attn_ctx",
    )(*args)


HGRN_DIAG = 128
HGRN_MASKED_LEVELS = 5
HGRN_VREG_LEVELS = 3


def _hgrn_masks():
    r = np.arange(HGRN_DIAG)[:, None]
    c = np.arange(HGRN_DIAG)[None, :]
    levels = []
    for lv in range(HGRN_MASKED_LEVELS):
        same = (r >> (lv + 1)) == (c >> (lv + 1))
        levels.append(same & (((r >> lv) & 1) == 1) & (((c >> lv) & 1) == 0))
    fwd = [r == c] + levels
    bwd = [m.T for m in levels]
    return np.stack(fwd + bwd).astype(np.float32)


def _hgrn_block(q, k, g, v, st, mask_ref, sub, *, reverse):
    n = HGRN_ROWS
    n_diag = n // HGRN_DIAG

    def diag_scores(qe, ke, m):
        return [_dot_nt(qe[i * HGRN_DIAG:(i + 1) * HGRN_DIAG], ke[i * HGRN_DIAG:(i + 1) * HGRN_DIAG]) * m
                for i in range(n_diag)]

    def level_mask(lv):
        return mask_ref[(1 + HGRN_MASKED_LEVELS + lv) if reverse else (1 + lv)]

    g3 = g.reshape(n // 8, 8, HEAD_DIM)
    zero3 = jnp.zeros_like(g3)
    pre3, suf3 = (zero3, g3) if reverse else (g3, zero3)
    tot3 = g3
    acc = diag_scores(q.astype(BF16), k.astype(BF16), mask_ref[0])
    for lv in range(HGRN_VREG_LEVELS):
        b = 1 << lv
        upper = ((sub >> lv) & 1) == 1
        e = jnp.exp(jnp.where(upper, pre3, suf3)).reshape(n, HEAD_DIM)
        part = diag_scores((q * e).astype(BF16), (k * e).astype(BF16), level_mask(lv))
        acc = [a + p for a, p in zip(acc, part)]
        sib = jnp.where(upper, pltpu.roll(tot3, b, axis=1), pltpu.roll(tot3, 8 - b, axis=1))
        pre3 = pre3 + jnp.where(upper, sib, 0.0)
        suf3 = suf3 + jnp.where(upper, 0.0, sib)
        tot3 = tot3 + sib

    pieces = lambda x3: [x3[i] for i in range(n // 8)]
    pre8, suf8, tot8 = pieces(pre3), pieces(suf3), pieces(tot3)
    rows = lambda xs, lo, hi: jnp.concatenate(xs[lo // 8:hi // 8], axis=0) if hi - lo > 8 else xs[lo // 8]
    o_levels = []
    for lv in range(HGRN_VREG_LEVELS, HGRN_LEVELS):
        b = 1 << lv
        pb = b // 8
        if lv < HGRN_MASKED_LEVELS:
            u = jnp.concatenate([(pre8 if (i // pb) % 2 else suf8)[i] for i in range(n // 8)], axis=0)
            e = jnp.exp(u)
            part = diag_scores((q * e).astype(BF16), (k * e).astype(BF16), level_mask(lv))
            acc = [a + p for a, p in zip(acc, part)]
        else:
            contrib = []
            for j in range(n // (2 * b)):
                lo, mid, hi = 2 * b * j, 2 * b * j + b, 2 * b * (j + 1)
                e_lo = jnp.exp(rows(suf8, lo, mid))
                e_up = jnp.exp(rows(pre8, mid, hi))
                x_lo = ((q if reverse else k)[lo:mid] * e_lo).astype(BF16)
                x_up = ((k if reverse else q)[mid:hi] * e_up).astype(BF16)
                zeros = jnp.zeros((b, HEAD_DIM), F32)
                if reverse:
                    contrib += [_dot(_dot_nt(x_lo, x_up).astype(BF16), v[mid:hi]), zeros]
                else:
                    contrib += [zeros, _dot(_dot_nt(x_up, x_lo).astype(BF16), v[lo:mid])]
            o_levels.append(jnp.concatenate(contrib, axis=0))
        for j in range(n // (2 * b)):
            lo_p, mid_p, hi_p = 2 * pb * j, 2 * pb * j + pb, 2 * pb * (j + 1)
            t_lo, t_up = tot8[lo_p], tot8[mid_p]
            t_new = t_lo + t_up
            for i in range(lo_p, mid_p):
                suf8[i] = suf8[i] + t_up
                tot8[i] = t_new
            for i in range(mid_p, hi_p):
                pre8[i] = pre8[i] + t_lo
                tot8[i] = t_new
    pre = jnp.concatenate(pre8, axis=0)
    suf = jnp.concatenate(suf8, axis=0)
    q_dec, k_dec = (suf, pre) if reverse else (pre, suf)
    o = jnp.concatenate([_dot(acc[i].astype(BF16), v[i * HGRN_DIAG:(i + 1) * HGRN_DIAG])
                         for i in range(n_diag)], axis=0)
    o = o + _dot_nt((q * jnp.exp(q_dec)).astype(BF16), st.astype(BF16))
    for o_lv in o_levels:
        o = o + o_lv
    st_new = st * jnp.exp(tot8[0][0:1, :]) + _dot_tn(v, (k * jnp.exp(k_dec)).astype(BF16))
    return o, st_new


def _hgrn_kernel(*refs, n_blocks, heads, has_state):
    if has_state:
        (q_ref, v_ref, xf_ref, xb_ref, gate_ref, lbf_ref, lbb_ref, g_ref, mask_ref, s0_ref,
         o_ref, acc_ref) = refs
    else:
        (q_ref, v_ref, xf_ref, xb_ref, gate_ref, lbf_ref, lbb_ref, g_ref, mask_ref,
         o_ref, sout_ref, acc_ref) = refs
    n = HGRN_ROWS
    sub = lax.broadcasted_iota(jnp.int32, (n // 8, 8, HEAD_DIM), 1)

    def lower_bound(ref, lanes):
        l0, l1 = ref[0:1, lanes], ref[1:2, lanes]
        mx = jnp.maximum(l0, l1)
        e0, e1 = jnp.exp(l0 - mx), jnp.exp(l1 - mx)
        return e0 / (e0 + e1)

    x_refs = (xf_ref, xb_ref)
    for h in range(heads):
        lanes = slice(h * HEAD_DIM, (h + 1) * HEAD_DIM)
        lbs = [lower_bound(lbf_ref, lanes), lower_bound(lbb_ref, lanes)]
        sts = [s0_ref[0, d, h].T if has_state else jnp.zeros((HEAD_DIM, HEAD_DIM), F32) for d in range(2)]
        for j in range(n_blocks):
            for d in range(2):
                blk = (n_blocks - 1 - j) if d else j
                rows = slice(blk * n, (blk + 1) * n)
                f = lbs[d] + (1.0 - lbs[d]) * jax.nn.sigmoid(x_refs[d][0, rows, lanes])
                o, sts[d] = _hgrn_block(q_ref[0, rows, lanes], 1.0 - f, jnp.log(f), v_ref[0, rows, lanes],
                                        sts[d], mask_ref, sub, reverse=bool(d))
                acc_ref[d, rows, lanes] = o
        if not has_state:
            for d in range(2):
                sout_ref[0, d, h] = sts[d].T
    for h in range(heads):
        lanes = slice(h * HEAD_DIM, (h + 1) * HEAD_DIM)
        o = acc_ref[0, :, lanes] + acc_ref[1, :, lanes]
        o = o * lax.rsqrt(jnp.mean(o * o, axis=-1, keepdims=True) + NORM_EPS)
        o_ref[0, :, lanes] = (o * g_ref[:, lanes] * gate_ref[0, :, lanes]).astype(BF16)


def _hgrn_call(rq, ri, xf, xb, gate, lbf, lbb, rnn_g, masks, s0):
    b, l, _ = rq.shape
    n_blocks = l // HGRN_ROWS
    heads = max(1, min(N_HEADS, 4 // n_blocks))
    w = heads * HEAD_DIM
    head = lambda: pl.BlockSpec((1, l, w), lambda i, h: (i, 0, h))
    per_head = lambda rows: pl.BlockSpec((rows, w), lambda i, h: (0, h))
    state_spec = pl.BlockSpec((1, 2, heads, HEAD_DIM, HEAD_DIM), lambda i, h: (i, 0, h, 0, 0))
    in_specs = [head(), head(), head(), head(), head(), per_head(2), per_head(2), per_head(1),
                pl.BlockSpec(masks.shape, lambda i, h: (0, 0, 0))]
    args = [rq, ri, xf, xb, gate, lbf, lbb, rnn_g, masks]
    out_specs = [head()]
    out_shape = [jax.ShapeDtypeStruct((b, l, GROUP_W), BF16)]
    if s0 is not None:
        in_specs.append(state_spec)
        args.append(s0)
    else:
        out_specs.append(state_spec)
        out_shape.append(jax.ShapeDtypeStruct((b, 2, N_HEADS, HEAD_DIM, HEAD_DIM), F32))
    return pl.pallas_call(
        functools.partial(_hgrn_kernel, n_blocks=n_blocks, heads=heads, has_state=s0 is not None),
        grid=(b, N_HEADS // heads),
        in_specs=in_specs,
        out_specs=out_specs,
        out_shape=out_shape,
        scratch_shapes=[pltpu.VMEM((2, l, w), F32)],
        compiler_params=_params(("arbitrary", "arbitrary")),
        name="hgrn_lat" if s0 is not None else "hgrn_ctx",
    )(*args)


def _outproj_kernel(att_ref, rnn_ref, x_ref, mod_ref, w_ref, g_ref, b_ref, x1_ref, xm2_ref):
    m = mod_ref[0]
    mix = _dot(att_ref[...], w_ref[0:GROUP_W, :]) + _dot(rnn_ref[...], w_ref[GROUP_W:2 * GROUP_W, :])
    x1 = _layer_norm(DEEPNORM_ALPHA * x_ref[...] + m[2:3, :] * mix, g_ref[...], b_ref[...])
    x1_ref[...] = x1
    xm2_ref[...] = (x1 * (1.0 + m[4:5, :]) + m[3:4, :]).astype(BF16)


def _outproj_call(att, rnn, x2d, mod3, w_out, ln_g, ln_b, *, latent, seq_len):
    t = x2d.shape[0]
    rb = ROWS_PROJ
    blocks_per_seq = seq_len // rb if latent else 1

    def mod_idx(i):
        return ((1 + i // blocks_per_seq) if latent else 0, 0, 0)

    row_spec = lambda w: pl.BlockSpec((rb, w), lambda i: (i, 0))
    vec = pl.BlockSpec((1, D_MODEL), lambda i: (0, 0))
    return pl.pallas_call(
        _outproj_kernel,
        grid=(t // rb,),
        in_specs=[row_spec(GROUP_W), row_spec(GROUP_W), row_spec(D_MODEL),
                  pl.BlockSpec((1, N_MOD, D_MODEL), mod_idx),
                  pl.BlockSpec(w_out.shape, lambda i: (0, 0)), vec, vec],
        out_specs=[row_spec(D_MODEL), row_spec(D_MODEL)],
        out_shape=[jax.ShapeDtypeStruct((t, D_MODEL), F32), jax.ShapeDtypeStruct((t, D_MODEL), BF16)],
        compiler_params=_params(("arbitrary",)),
        name="outproj_lat" if latent else "outproj_ctx",
    )(att, rnn, x2d, mod3, w_out, ln_g, ln_b)


def _ffn_kernel(xm2_ref, x1_ref, mod_ref, wu_ref, cw_ref, cb_ref, wd_ref, g_ref, b_ref, o_ref, hid_ref,
                *, seq_len):
    rows = xm2_ref.shape[0]
    sub = lax.broadcasted_iota(jnp.int32, (8, FF_TILE), 0)

    def shifted(h, shift, edge_row):
        r = pltpu.roll(h, shift, axis=0)
        parts = []
        for s in range(rows // seq_len):
            edge = s * seq_len + (edge_row // 8) * 8
            fixed = jnp.where(sub == edge_row % 8, 0.0, r[edge:edge + 8])
            parts += [r[s * seq_len:edge], fixed, r[edge + 8:(s + 1) * seq_len]]
        return jnp.concatenate([p for p in parts if p.shape[0]], axis=0)

    def conv(h, cols):
        prev = shifted(h, 1, 0)
        nxt = shifted(h, rows - 1, seq_len - 1)
        return prev * cw_ref[0:1, cols] + h * cw_ref[1:2, cols] + nxt * cw_ref[2:3, cols] + cb_ref[:, cols]

    for j in range(D_FF // FF_TILE):
        cols_a = slice(j * FF_TILE, (j + 1) * FF_TILE)
        cols_u = slice(D_FF + j * FF_TILE, D_FF + (j + 1) * FF_TILE)
        a = conv(_dot(xm2_ref[...], wu_ref[:, cols_a]), cols_a)
        u = conv(_dot(xm2_ref[...], wu_ref[:, cols_u]), cols_u)
        hid_ref[:, cols_a] = (_silu(a) * u).astype(BF16)
    m = mod_ref[0]
    y = DEEPNORM_ALPHA * x1_ref[...] + m[5:6, :] * _dot(hid_ref[...], wd_ref[...])
    o_ref[...] = _layer_norm(y, g_ref[...], b_ref[...])


def _ffn_call(xm2, x1, mod3, w_up, conv_w, conv_b, w_down, ln_g, ln_b, *, latent, seq_len):
    t = xm2.shape[0]
    rb = ROWS_FFN
    assert rb % seq_len == 0

    def mod_idx(i):
        return ((1 + i * rb // seq_len) if latent else 0, 0, 0)

    row_spec = lambda: pl.BlockSpec((rb, D_MODEL), lambda i: (i, 0))
    resident = lambda a: pl.BlockSpec(a.shape, lambda i: (0, 0), pipeline_mode=pl.Buffered(1))
    return pl.pallas_call(
        functools.partial(_ffn_kernel, seq_len=seq_len),
        grid=(t // rb,),
        in_specs=[row_spec(), row_spec(), pl.BlockSpec((1, N_MOD, D_MODEL), mod_idx),
                  resident(w_up), resident(conv_w), resident(conv_b), resident(w_down),
                  resident(ln_g), resident(ln_b)],
        out_specs=row_spec(),
        out_shape=jax.ShapeDtypeStruct((t, D_MODEL), F32),
        scratch_shapes=[pltpu.VMEM((rb, D_FF), BF16)],
        compiler_params=_params(("arbitrary",)),
        name="ffn_lat" if latent else "ffn_ctx",
    )(xm2, x1, mod3, w_up, conv_w, conv_b, w_down, ln_g, ln_b)


def _rope_tables(seq_len):
    quarter = QK_DIM // 4
    freqs = 1.0 / (ROPE_BASE ** (jnp.arange(quarter, dtype=F32) / quarter))
    t = jnp.arange(seq_len)
    ang_r = (t // GRID_W).astype(F32)[:, None] * freqs
    ang_c = (t % GRID_W).astype(F32)[:, None] * freqs
    zeros = jnp.zeros_like(ang_r)

    def tile(parts):
        return jnp.tile(jnp.concatenate(parts, axis=-1), (1, GROUP_W // QK_DIM))

    cos = tile([jnp.cos(ang_r), jnp.cos(ang_r), jnp.cos(ang_c), jnp.cos(ang_c)])
    sin_a = tile([-jnp.sin(ang_r), zeros, -jnp.sin(ang_c), zeros])
    sin_b = tile([zeros, jnp.sin(ang_r), zeros, jnp.sin(ang_c)])
    return cos, sin_a, sin_b


def kernel(x_prompt, x_sample, cache_k, cache_v, state_rnn, c, c_ctx, w_ada, b_ada, w_in, lambda_q1, lambda_k1, lambda_q2, lambda_k2, lb_fwd_logits, lb_bwd_logits, att_norm_g, rnn_norm_g, w_out, ln1_g, ln1_b, w_up, conv_w, conv_b, w_down, ln2_g, ln2_b):
    assert w_ada.shape[0] == DEPTH
    bp, lp, d = x_prompt.shape
    bs, ls, _ = x_sample.shape
    past = cache_k.shape[2]

    cond = jnp.zeros((16, d), F32).at[0].set(c_ctx).at[1:1 + bs].set(c)
    mod3 = _mod_call(cond, w_ada[0], b_ada).reshape(16, N_MOD, d)

    w_in_b = w_in[0].astype(BF16)
    w_out_b = w_out[0].astype(BF16)
    w_up_b = w_up[0].astype(BF16)
    w_down_b = w_down[0].astype(BF16)
    lams = (lambda_q1, lambda_k1, lambda_q2, lambda_k2)
    masks = jnp.asarray(_hgrn_masks())
    row = lambda a: a.reshape(1, -1)

    def trunk(x, *, latent, cache, s0):
        b, l, _ = x.shape
        x2d = x.reshape(b * l, d)
        outs = _inproj_call(x2d, mod3, w_in_b, _rope_tables(l) if latent else None,
                            latent=latent, seq_len=l)
        q, k, v, rq, xf, xb, ri, rg = (o.reshape(b, l, GROUP_W) for o in outs[:8])
        att = _attn_call(q, k, v, cache, lams, att_norm_g)
        hg = _hgrn_call(rq, ri, xf, xb, rg, lb_fwd_logits, lb_bwd_logits, rnn_norm_g, masks, s0)
        rnn = hg[0]
        x1, xm2 = _outproj_call(att.reshape(b * l, GROUP_W), rnn.reshape(b * l, GROUP_W), x2d, mod3,
                                w_out_b, ln1_g, ln1_b, latent=latent, seq_len=l)
        y = _ffn_call(xm2, x1, mod3, w_up_b, conv_w[0], conv_b, w_down_b, ln2_g, ln2_b,
                      latent=latent, seq_len=l)
        return y.reshape(b, l, d), outs[8:], hg[1:]

    y_p, (k_raw, v_raw), (s_new,) = trunk(x_prompt, latent=False, cache=None, s0=None)
    cache = (cache_k.reshape(bs, past, GROUP_W), cache_v.reshape(bs, past, GROUP_W))
    y_s, _, _ = trunk(x_sample, latent=True, cache=cache,
                      s0=state_rnn.reshape(bs, 2, N_HEADS, HEAD_DIM, HEAD_DIM))

    new_cache_k = k_raw.reshape(bp, DEPTH, lp, N_HEADS, 2, QK_DIM)
    new_cache_v = v_raw.reshape(bp, DEPTH, lp, N_HEADS, HEAD_DIM)
    new_state = s_new.reshape(bp, DEPTH, 2, N_HEADS, HEAD_DIM, HEAD_DIM)
    return (y_p, y_s, new_cache_k, new_cache_v, new_state)
```

```python
import functools
import math

import jax
import jax.numpy as jnp
import numpy as np
from jax import lax
from jax.experimental import pallas as pl
from jax.experimental.pallas import tpu as pltpu

D_MODEL = 1024
GRID_W = 64
N_HEADS = 4
HEAD_DIM = 128
QK_DIM = 64
GROUP_W = 512
N_GROUPS = 8
D_FF = 2816
N_MOD = 6
ROPE_BASE = 10000.0
DEPTH = 1
DEEPNORM_ALPHA = (2.0 * DEPTH) ** 0.25
NORM_EPS = 1e-5
LAM_INIT = 0.8 - 0.6 * math.exp(-0.3 * 0)
LOG2_E = math.log2(math.e)

V7X_VMEM_BYTES = 64 * 1024 * 1024
VMEM_LIMIT = V7X_VMEM_BYTES * 15 // 16

ROWS_PROJ = 512
ROWS_FFN = 1024
FF_TILE = 256
Q_ROWS = 256
HGRN_ROWS = 256
HGRN_LEVELS = int(math.log2(HGRN_ROWS))

F32 = jnp.float32
BF16 = jnp.bfloat16


def _params(semantics):
    return pltpu.CompilerParams(dimension_semantics=semantics, vmem_limit_bytes=VMEM_LIMIT)


def _dot(a, b):
    return jnp.dot(a, b, preferred_element_type=F32)


def _dot_nt(a, b):
    return lax.dot_general(a, b, (((1,), (1,)), ((), ())), preferred_element_type=F32)


def _dot_tn(a, b):
    return lax.dot_general(a, b, (((0,), (0,)), ((), ())), preferred_element_type=F32)


def _silu(x):
    return x * jax.nn.sigmoid(x)


def _layer_norm(y, g, b):
    mu = jnp.mean(y, axis=-1, keepdims=True)
    d = y - mu
    var = jnp.mean(d * d, axis=-1, keepdims=True)
    return d * lax.rsqrt(var + NORM_EPS) * g + b


def _mod_kernel(c_ref, w_ref, b_ref, o_ref):
    s = _silu(c_ref[...]).astype(BF16)
    o_ref[...] = _dot(s, w_ref[...].astype(BF16)) + b_ref[...]


def _mod_call(cond, w_ada, b_ada):
    n, d = cond.shape
    cols = w_ada.shape[1]
    tile = 512
    return pl.pallas_call(
        _mod_kernel,
        grid=(cols // tile,),
        in_specs=[pl.BlockSpec((n, d), lambda j: (0, 0)),
                  pl.BlockSpec((d, tile), lambda j: (0, j)),
                  pl.BlockSpec((1, tile), lambda j: (0, j))],
        out_specs=pl.BlockSpec((n, tile), lambda j: (0, j)),
        out_shape=jax.ShapeDtypeStruct((n, cols), F32),
        compiler_params=_params(("arbitrary",)),
        name="mod",
    )(cond, w_ada, b_ada)


def _rope(x, cos_ref, sin_a_ref, sin_b_ref):
    parts = []
    for c in range(GROUP_W // 128):
        sl = slice(c * 128, (c + 1) * 128)
        xc = x[:, sl]
        parts.append(xc * cos_ref[:, sl]
                     + pltpu.roll(xc, 128 - 16, axis=1) * sin_a_ref[:, sl]
                     + pltpu.roll(xc, 16, axis=1) * sin_b_ref[:, sl])
    return jnp.concatenate(parts, axis=1)


def _inproj_kernel(*refs, latent):
    if latent:
        (x_ref, mod_ref, w_ref, cos_ref, sin_a_ref, sin_b_ref,
         q_ref, k_ref, v_ref, rq_ref, xf_ref, xb_ref, ri_ref, rg_ref) = refs
    else:
        (x_ref, mod_ref, w_ref,
         q_ref, k_ref, v_ref, rq_ref, xf_ref, xb_ref, ri_ref, rg_ref, kraw_ref, vraw_ref) = refs
    m = mod_ref[0]
    xm = (x_ref[...] * (1.0 + m[1:2, :]) + m[0:1, :]).astype(BF16)

    def proj(g):
        return _dot(xm, w_ref[:, g * GROUP_W:(g + 1) * GROUP_W])

    aq = proj(0)
    ak = proj(1)
    av = proj(2)
    if latent:
        aq = _rope(aq, cos_ref, sin_a_ref, sin_b_ref)
        ak = _rope(ak, cos_ref, sin_a_ref, sin_b_ref)
    else:
        kraw_ref[...] = ak
        vraw_ref[...] = av
    q_ref[...] = (aq * (QK_DIM ** -0.5 * LOG2_E)).astype(BF16)
    k_ref[...] = ak.astype(BF16)
    v_ref[...] = av.astype(BF16)
    rq_ref[...] = proj(3)
    xf_ref[...] = proj(4)
    xb_ref[...] = proj(5)
    ri_ref[...] = proj(6).astype(BF16)
    rg_ref[...] = _silu(proj(7))


def _inproj_call(x2d, mod3, w_in, rope_tabs, *, latent, seq_len):
    t = x2d.shape[0]
    rb = ROWS_PROJ
    blocks_per_seq = seq_len // rb if latent else 1

    def mod_idx(i):
        return ((1 + i // blocks_per_seq) if latent else 0, 0, 0)

    row_spec = lambda w: pl.BlockSpec((rb, w), lambda i: (i, 0))
    in_specs = [row_spec(D_MODEL),
                pl.BlockSpec((1, N_MOD, D_MODEL), mod_idx),
                pl.BlockSpec(w_in.shape, lambda i: (0, 0))]
    args = [x2d, mod3, w_in]
    if latent:
        in_specs += [pl.BlockSpec((rb, GROUP_W), lambda i: (i % blocks_per_seq, 0))] * 3
        args += list(rope_tabs)
    dts = [BF16, BF16, BF16, F32, F32, F32, BF16, F32]
    if not latent:
        dts += [F32, F32]
    return pl.pallas_call(
        functools.partial(_inproj_kernel, latent=latent),
        grid=(t // rb,),
        in_specs=in_specs,
        out_specs=[row_spec(GROUP_W) for _ in dts],
        out_shape=[jax.ShapeDtypeStruct((t, GROUP_W), dt) for dt in dts],
        compiler_params=_params(("arbitrary",)),
        name="inproj_lat" if latent else "inproj_ctx",
    )(*args)


def _attn_kernel(*refs, has_cache):
    if has_cache:
        (q_ref, k_ref, v_ref, ck_ref, cv_ref, lq1, lk1, lq2, lk2, g_ref, o_ref) = refs
    else:
        (q_ref, k_ref, v_ref, lq1, lk1, lq2, lk2, g_ref, o_ref) = refs
    lam = (jnp.exp(jnp.sum(lq1[...] * lk1[...], axis=-1, keepdims=True))
           - jnp.exp(jnp.sum(lq2[...] * lk2[...], axis=-1, keepdims=True)) + LAM_INIT)
    qb = q_ref.shape[1]
    lane = lax.broadcasted_iota(jnp.int32, (qb, HEAD_DIM), 1)
    first_map = lane < QK_DIM
    zero = jnp.zeros((), BF16)

    def scores(h):
        sl = slice(h * HEAD_DIM, (h + 1) * HEAD_DIM)
        qh = q_ref[0, :, sl]
        qq = jnp.concatenate([jnp.where(first_map, qh, zero), jnp.where(first_map, zero, qh)], axis=0)
        s_n = _dot_nt(k_ref[0, :, sl], qq)
        s_c = _dot_nt(ck_ref[0, :, sl].astype(BF16), qq) if has_cache else None
        return s_n, s_c

    ahead = 2
    pending = [scores(h) for h in range(ahead)]
    for h in range(N_HEADS):
        sl = slice(h * HEAD_DIM, (h + 1) * HEAD_DIM)
        s_n, s_c = pending.pop(0)
        if h + ahead < N_HEADS:
            pending.append(scores(h + ahead))
        mx = jnp.max(s_n, axis=0, keepdims=True)
        if has_cache:
            mx = jnp.maximum(mx, jnp.max(s_c, axis=0, keepdims=True))
        e_n = jnp.exp2(s_n - mx)
        den = jnp.sum(e_n, axis=0, keepdims=True)
        ev = _dot_tn(v_ref[0, :, sl], e_n.astype(BF16))
        if has_cache:
            e_c = jnp.exp2(s_c - mx)
            den = den + jnp.sum(e_c, axis=0, keepdims=True)
            ev = ev + _dot_tn(cv_ref[0, :, sl].astype(BF16), e_c.astype(BF16))
        inv = 1.0 / den
        o = ev[:, :qb] * inv[:, :qb] - ev[:, qb:] * (inv[:, qb:] * lam)
        o = o * lax.rsqrt(jnp.mean(o * o, axis=0, keepdims=True) + NORM_EPS)
        o_ref[0, :, sl] = (o.T * g_ref[:, sl] * (1.0 - LAM_INIT)).astype(BF16)


def _attn_call(q, k, v, cache, lams, att_g):
    b, l, _ = q.shape
    qb = Q_ROWS
    full = lambda a: pl.BlockSpec((1,) + a.shape[1:], lambda i, j: (i, 0, 0))
    in_specs = [pl.BlockSpec((1, qb, GROUP_W), lambda i, j: (i, j, 0)), full(k), full(v)]
    args = [q, k, v]
    if cache is not None:
        in_specs += [full(cache[0]), full(cache[1])]
        args += list(cache)
    in_specs += [pl.BlockSpec((1, QK_DIM), lambda i, j: (0, 0))] * 4
    in_specs += [pl.BlockSpec((1, GROUP_W), lambda i, j: (0, 0))]
    args += list(lams) + [att_g]
    return pl.pallas_call(
        functools.partial(_attn_kernel, has_cache=cache is not None),
        grid=(b, l // qb),
        in_specs=in_specs,
        out_specs=pl.BlockSpec((1, qb, GROUP_W), lambda i, j: (i, j, 0)),
        out_shape=jax.ShapeDtypeStruct((b, l, GROUP_W), BF16),
        compiler_params=_params(("arbitrary", "arbitrary")),
        name="attn_lat" if cache is not None else "attn_ctx",
    )(*args)


HGRN_DIAG = 128
HGRN_MASKED_LEVELS = 5
HGRN_VREG_LEVELS = 3


def _hgrn_masks():
    r = np.arange(HGRN_DIAG)[:, None]
    c = np.arange(HGRN_DIAG)[None, :]
    levels = []
    for lv in range(HGRN_MASKED_LEVELS):
        same = (r >> (lv + 1)) == (c >> (lv + 1))
        levels.append(same & (((r >> lv) & 1) == 1) & (((c >> lv) & 1) == 0))
    fwd = [r == c] + levels
    bwd = [m.T for m in levels]
    return np.stack(fwd + bwd).astype(np.float32)


def _hgrn_block(q, k, g, v, st, mask_ref, sub, *, reverse):
    n = HGRN_ROWS
    n_diag = n // HGRN_DIAG

    def diag_scores(qe, ke, m):
        return [_dot_nt(qe[i * HGRN_DIAG:(i + 1) * HGRN_DIAG], ke[i * HGRN_DIAG:(i + 1) * HGRN_DIAG]) * m
                for i in range(n_diag)]

    def level_mask(lv):
        return mask_ref[(1 + HGRN_MASKED_LEVELS + lv) if reverse else (1 + lv)]

    g3 = g.reshape(n // 8, 8, HEAD_DIM)
    zero3 = jnp.zeros_like(g3)
    pre3, suf3 = (zero3, g3) if reverse else (g3, zero3)
    tot3 = g3
    acc = diag_scores(q.astype(BF16), k.astype(BF16), mask_ref[0])
    for lv in range(HGRN_VREG_LEVELS):
        b = 1 << lv
        upper = ((sub >> lv) & 1) == 1
        e = jnp.exp2(jnp.where(upper, pre3, suf3)).reshape(n, HEAD_DIM)
        part = diag_scores((q * e).astype(BF16), (k * e).astype(BF16), level_mask(lv))
        acc = [a + p for a, p in zip(acc, part)]
        sib = jnp.where(upper, pltpu.roll(tot3, b, axis=1), pltpu.roll(tot3, 8 - b, axis=1))
        pre3 = pre3 + jnp.where(upper, sib, 0.0)
        suf3 = suf3 + jnp.where(upper, 0.0, sib)
        tot3 = tot3 + sib

    pieces = lambda x3: [x3[i] for i in range(n // 8)]
    pre8, suf8, tot8 = pieces(pre3), pieces(suf3), pieces(tot3)
    rows = lambda xs, lo, hi: jnp.concatenate(xs[lo // 8:hi // 8], axis=0) if hi - lo > 8 else xs[lo // 8]
    big_scores = {}
    for lv in range(HGRN_VREG_LEVELS, HGRN_LEVELS):
        b = 1 << lv
        pb = b // 8
        if lv < HGRN_MASKED_LEVELS:
            u = jnp.concatenate([(pre8 if (i // pb) % 2 else suf8)[i] for i in range(n // 8)], axis=0)
            e = jnp.exp2(u)
            part = diag_scores((q * e).astype(BF16), (k * e).astype(BF16), level_mask(lv))
            acc = [a + p for a, p in zip(acc, part)]
        else:
            for j in range(n // (2 * b)):
                lo, mid, hi = 2 * b * j, 2 * b * j + b, 2 * b * (j + 1)
                e_lo = jnp.exp2(rows(suf8, lo, mid))
                e_up = jnp.exp2(rows(pre8, mid, hi))
                x_lo = ((q if reverse else k)[lo:mid] * e_lo).astype(BF16)
                x_up = ((k if reverse else q)[mid:hi] * e_up).astype(BF16)
                big_scores[lv, j] = _dot_nt(x_lo, x_up) if reverse else _dot_nt(x_up, x_lo)
        for j in range(n // (2 * b)):
            lo_p, mid_p, hi_p = 2 * pb * j, 2 * pb * j + pb, 2 * pb * (j + 1)
            t_lo, t_up = tot8[lo_p], tot8[mid_p]
            t_new = t_lo + t_up
            for i in range(lo_p, mid_p):
                suf8[i] = suf8[i] + t_up
                tot8[i] = t_new
            for i in range(mid_p, hi_p):
                pre8[i] = pre8[i] + t_lo
                tot8[i] = t_new
    pre = jnp.concatenate(pre8, axis=0)
    suf = jnp.concatenate(suf8, axis=0)
    q_dec, k_dec = (suf, pre) if reverse else (pre, suf)
    q_in = (q * jnp.exp2(q_dec)).astype(BF16)
    k_out = (k * jnp.exp2(k_dec)).astype(BF16)
    decay = jnp.exp2(tot8[0][0:1, :])

    o_inter = _dot_nt(q_in, st.astype(BF16))
    ds = _dot_tn(v, k_out)
    o_diag = [_dot(acc[i].astype(BF16), v[i * HGRN_DIAG:(i + 1) * HGRN_DIAG]) for i in range(n_diag)]
    o_big = {}
    for (lv, j), a in big_scores.items():
        lo, mid, hi = (2 * j) << lv, (2 * j + 1) << lv, (2 * j + 2) << lv
        o_big[lv, j] = _dot(a.astype(BF16), v[mid:hi] if reverse else v[lo:mid])
    st_new = st * decay + ds
    o = jnp.concatenate(o_diag, axis=0) + o_inter
    for lv in range(HGRN_MASKED_LEVELS, HGRN_LEVELS):
        contrib = []
        for j in range(n >> (lv + 1)):
            zeros = jnp.zeros((1 << lv, HEAD_DIM), F32)
            contrib += [o_big[lv, j], zeros] if reverse else [zeros, o_big[lv, j]]
        o = o + jnp.concatenate(contrib, axis=0)
    return o, st_new


def _hgrn_kernel(*refs, n_blocks, heads, has_state):
    if has_state:
        (q_ref, v_ref, xf_ref, xb_ref, gate_ref, lbf_ref, lbb_ref, g_ref, mask_ref, s0_ref,
         o_ref, acc_ref) = refs
    else:
        (q_ref, v_ref, xf_ref, xb_ref, gate_ref, lbf_ref, lbb_ref, g_ref, mask_ref,
         o_ref, sout_ref, acc_ref) = refs
    n = HGRN_ROWS
    sub = lax.broadcasted_iota(jnp.int32, (n // 8, 8, HEAD_DIM), 1)

    def lower_bound(ref, lanes):
        l0, l1 = ref[0:1, lanes], ref[1:2, lanes]
        mx = jnp.maximum(l0, l1)
        e0, e1 = jnp.exp(l0 - mx), jnp.exp(l1 - mx)
        return e0 / (e0 + e1)

    x_refs = (xf_ref, xb_ref)
    for h in range(heads):
        lanes = slice(h * HEAD_DIM, (h + 1) * HEAD_DIM)
        lbs = [lower_bound(lbf_ref, lanes), lower_bound(lbb_ref, lanes)]
        sts = [s0_ref[0, d, h].T if has_state else jnp.zeros((HEAD_DIM, HEAD_DIM), F32) for d in range(2)]
        for j in range(n_blocks):
            for d in range(2):
                blk = (n_blocks - 1 - j) if d else j
                rows = slice(blk * n, (blk + 1) * n)
                f = lbs[d] + (1.0 - lbs[d]) * jax.nn.sigmoid(x_refs[d][0, rows, lanes])
                o, sts[d] = _hgrn_block(q_ref[0, rows, lanes], 1.0 - f, jnp.log2(f), v_ref[0, rows, lanes],
                                        sts[d], mask_ref, sub, reverse=bool(d))
                acc_ref[d, rows, lanes] = o
        if not has_state:
            for d in range(2):
                sout_ref[0, d, h] = sts[d].T
    for h in range(heads):
        lanes = slice(h * HEAD_DIM, (h + 1) * HEAD_DIM)
        o = acc_ref[0, :, lanes] + acc_ref[1, :, lanes]
        o = o * lax.rsqrt(jnp.mean(o * o, axis=-1, keepdims=True) + NORM_EPS)
        o_ref[0, :, lanes] = (o * g_ref[:, lanes] * gate_ref[0, :, lanes]).astype(BF16)


def _hgrn_call(rq, ri, xf, xb, gate, lbf, lbb, rnn_g, masks, s0):
    b, l, _ = rq.shape
    n_blocks = l // HGRN_ROWS
    heads = max(1, min(N_HEADS, 4 // n_blocks))
    w = heads * HEAD_DIM
    head = lambda: pl.BlockSpec((1, l, w), lambda i, h: (i, 0, h))
    per_head = lambda rows: pl.BlockSpec((rows, w), lambda i, h: (0, h))
    state_spec = pl.BlockSpec((1, 2, heads, HEAD_DIM, HEAD_DIM), lambda i, h: (i, 0, h, 0, 0))
    in_specs = [head(), head(), head(), head(), head(), per_head(2), per_head(2), per_head(1),
                pl.BlockSpec(masks.shape, lambda i, h: (0, 0, 0))]
    args = [rq, ri, xf, xb, gate, lbf, lbb, rnn_g, masks]
    out_specs = [head()]
    out_shape = [jax.ShapeDtypeStruct((b, l, GROUP_W), BF16)]
    if s0 is not None:
        in_specs.append(state_spec)
        args.append(s0)
    else:
        out_specs.append(state_spec)
        out_shape.append(jax.ShapeDtypeStruct((b, 2, N_HEADS, HEAD_DIM, HEAD_DIM), F32))
    return pl.pallas_call(
        functools.partial(_hgrn_kernel, n_blocks=n_blocks, heads=heads, has_state=s0 is not None),
        grid=(b, N_HEADS // heads),
        in_specs=in_specs,
        out_specs=out_specs,
        out_shape=out_shape,
        scratch_shapes=[pltpu.VMEM((2, l, w), F32)],
        compiler_params=_params(("arbitrary", "arbitrary")),
        name="hgrn_lat" if s0 is not None else "hgrn_ctx",
    )(*args)


def _tail_kernel(att_ref, rnn_ref, x_ref, mod_ref, wo_ref, g1_ref, b1_ref, wu_ref, cw_ref, cb_ref, wd_ref,
                 g2_ref, b2_ref, o_ref, x1_ref, xm2_ref, hid_ref, *, seq_len):
    rows = x_ref.shape[0]
    m = mod_ref[0]
    mix = _dot(att_ref[...], wo_ref[0:GROUP_W, :]) + _dot(rnn_ref[...], wo_ref[GROUP_W:2 * GROUP_W, :])
    x1 = _layer_norm(DEEPNORM_ALPHA * x_ref[...] + m[2:3, :] * mix, g1_ref[...], b1_ref[...])
    x1_ref[...] = x1
    xm2_ref[...] = (x1 * (1.0 + m[4:5, :]) + m[3:4, :]).astype(BF16)

    sub = lax.broadcasted_iota(jnp.int32, (8, FF_TILE), 0)

    def shifted(h, shift, edge_row):
        r = pltpu.roll(h, shift, axis=0)
        parts = []
        for s in range(rows // seq_len):
            edge = s * seq_len + (edge_row // 8) * 8
            fixed = jnp.where(sub == edge_row % 8, 0.0, r[edge:edge + 8])
            parts += [r[s * seq_len:edge], fixed, r[edge + 8:(s + 1) * seq_len]]
        return jnp.concatenate([p for p in parts if p.shape[0]], axis=0)

    def conv(h, cols):
        prev = shifted(h, 1, 0)
        nxt = shifted(h, rows - 1, seq_len - 1)
        return prev * cw_ref[0:1, cols] + h * cw_ref[1:2, cols] + nxt * cw_ref[2:3, cols] + cb_ref[:, cols]

    for j in range(D_FF // FF_TILE):
        cols_a = slice(j * FF_TILE, (j + 1) * FF_TILE)
        cols_u = slice(D_FF + j * FF_TILE, D_FF + (j + 1) * FF_TILE)
        a = conv(_dot(xm2_ref[...], wu_ref[:, cols_a]), cols_a)
        u = conv(_dot(xm2_ref[...], wu_ref[:, cols_u]), cols_u)
        hid_ref[:, cols_a] = (_silu(a) * u).astype(BF16)
    y = DEEPNORM_ALPHA * x1_ref[...] + m[5:6, :] * _dot(hid_ref[...], wd_ref[...])
    o_ref[...] = _layer_norm(y, g2_ref[...], b2_ref[...])


def _tail_call(att, rnn, x2d, mod3, w_out, ln1_g, ln1_b, w_up, conv_w, conv_b, w_down, ln2_g, ln2_b,
               *, latent, seq_len):
    t = x2d.shape[0]
    rb = ROWS_FFN
    assert rb % seq_len == 0

    def mod_idx(i):
        return ((1 + i * rb // seq_len) if latent else 0, 0, 0)

    row_spec = lambda w: pl.BlockSpec((rb, w), lambda i: (i, 0))
    resident = lambda a: pl.BlockSpec(a.shape, lambda i: (0, 0), pipeline_mode=pl.Buffered(1))
    return pl.pallas_call(
        functools.partial(_tail_kernel, seq_len=seq_len),
        grid=(t // rb,),
        in_specs=[row_spec(GROUP_W), row_spec(GROUP_W), row_spec(D_MODEL),
                  pl.BlockSpec((1, N_MOD, D_MODEL), mod_idx),
                  resident(w_out), resident(ln1_g), resident(ln1_b),
                  resident(w_up), resident(conv_w), resident(conv_b), resident(w_down),
                  resident(ln2_g), resident(ln2_b)],
        out_specs=row_spec(D_MODEL),
        out_shape=jax.ShapeDtypeStruct((t, D_MODEL), F32),
        scratch_shapes=[pltpu.VMEM((rb, D_MODEL), F32), pltpu.VMEM((rb, D_MODEL), BF16),
                        pltpu.VMEM((rb, D_FF), BF16)],
        compiler_params=_params(("arbitrary",)),
        name="tail_lat" if latent else "tail_ctx",
    )(att, rnn, x2d, mod3, w_out, ln1_g, ln1_b, w_up, conv_w, conv_b, w_down, ln2_g, ln2_b)


def _rope_tables(seq_len):
    quarter = QK_DIM // 4
    freqs = 1.0 / (ROPE_BASE ** (np.arange(quarter, dtype=np.float64) / quarter))
    t = np.arange(seq_len)
    ang_r = (t // GRID_W)[:, None] * freqs
    ang_c = (t % GRID_W)[:, None] * freqs
    zeros = np.zeros_like(ang_r)

    def tile(parts):
        return jnp.asarray(np.tile(np.concatenate(parts, axis=-1), (1, GROUP_W // QK_DIM)).astype(np.float32))

    cos = tile([np.cos(ang_r), np.cos(ang_r), np.cos(ang_c), np.cos(ang_c)])
    sin_a = tile([-np.sin(ang_r), zeros, -np.sin(ang_c), zeros])
    sin_b = tile([zeros, np.sin(ang_r), zeros, np.sin(ang_c)])
    return cos, sin_a, sin_b


def kernel(x_prompt, x_sample, cache_k, cache_v, state_rnn, c, c_ctx, w_ada, b_ada, w_in, lambda_q1, lambda_k1, lambda_q2, lambda_k2, lb_fwd_logits, lb_bwd_logits, att_norm_g, rnn_norm_g, w_out, ln1_g, ln1_b, w_up, conv_w, conv_b, w_down, ln2_g, ln2_b):
    assert w_ada.shape[0] == DEPTH
    bp, lp, d = x_prompt.shape
    bs, ls, _ = x_sample.shape
    past = cache_k.shape[2]

    cond = jnp.zeros((16, d), F32).at[0].set(c_ctx).at[1:1 + bs].set(c)
    mod3 = _mod_call(cond, w_ada[0], b_ada).reshape(16, N_MOD, d)

    w_in_b = w_in[0].astype(BF16)
    w_out_b = w_out[0].astype(BF16)
    w_up_b = w_up[0].astype(BF16)
    w_down_b = w_down[0].astype(BF16)
    lams = (lambda_q1, lambda_k1, lambda_q2, lambda_k2)
    masks = jnp.asarray(_hgrn_masks())
    row = lambda a: a.reshape(1, -1)

    def trunk(x, *, latent, cache, s0):
        b, l, _ = x.shape
        x2d = x.reshape(b * l, d)
        outs = _inproj_call(x2d, mod3, w_in_b, _rope_tables(l) if latent else None,
                            latent=latent, seq_len=l)
        q, k, v, rq, xf, xb, ri, rg = (o.reshape(b, l, GROUP_W) for o in outs[:8])
        att = _attn_call(q, k, v, cache, lams, att_norm_g)
        hg = _hgrn_call(rq, ri, xf, xb, rg, lb_fwd_logits, lb_bwd_logits, rnn_norm_g, masks, s0)
        rnn = hg[0]
        y = _tail_call(att.reshape(b * l, GROUP_W), rnn.reshape(b * l, GROUP_W), x2d, mod3, w_out_b, ln1_g,
                       ln1_b, w_up_b, conv_w[0], conv_b, w_down_b, ln2_g, ln2_b, latent=latent, seq_len=l)
        return y.reshape(b, l, d), outs[8:], hg[1:]

    y_p, (k_raw, v_raw), (s_new,) = trunk(x_prompt, latent=False, cache=None, s0=None)
    cache = (cache_k.reshape(bs, past, GROUP_W), cache_v.reshape(bs, past, GROUP_W))
    y_s, _, _ = trunk(x_sample, latent=True, cache=cache,
                      s0=state_rnn.reshape(bs, 2, N_HEADS, HEAD_DIM, HEAD_DIM))

    new_cache_k = k_raw.reshape(bp, DEPTH, lp, N_HEADS, 2, QK_DIM)
    new_cache_v = v_raw.reshape(bp, DEPTH, lp, N_HEADS, HEAD_DIM)
    new_state = s_new.reshape(bp, DEPTH, 2, N_HEADS, HEAD_DIM, HEAD_DIM)
    return (y_p, y_s, new_cache_k, new_cache_v, new_state)
```

```python
import functools
import math

import jax
import jax.numpy as jnp
import numpy as np
from jax import lax
from jax.experimental import pallas as pl
from jax.experimental.pallas import tpu as pltpu

D_MODEL = 1024
GRID_W = 64
N_HEADS = 4
HEAD_DIM = 128
QK_DIM = 64
GROUP_W = 512
N_GROUPS = 8
D_FF = 2816
N_MOD = 6
ROPE_BASE = 10000.0
DEPTH = 1
DEEPNORM_ALPHA = (2.0 * DEPTH) ** 0.25
NORM_EPS = 1e-5
LAM_INIT = 0.8 - 0.6 * math.exp(-0.3 * 0)
LOG2_E = math.log2(math.e)

V7X_VMEM_BYTES = 64 * 1024 * 1024
VMEM_LIMIT = V7X_VMEM_BYTES * 15 // 16

ROWS_PROJ = 512
ROWS_FFN = 1024
FF_TILE = 256
Q_ROWS = 256
HGRN_ROWS = 256
HGRN_LEVELS = int(math.log2(HGRN_ROWS))

F32 = jnp.float32
BF16 = jnp.bfloat16


def _params(semantics):
    return pltpu.CompilerParams(dimension_semantics=semantics, vmem_limit_bytes=VMEM_LIMIT)


def _dot(a, b):
    return jnp.dot(a, b, preferred_element_type=F32)


def _dot_nt(a, b):
    return lax.dot_general(a, b, (((1,), (1,)), ((), ())), preferred_element_type=F32)


def _dot_tn(a, b):
    return lax.dot_general(a, b, (((0,), (0,)), ((), ())), preferred_element_type=F32)


def _silu(x):
    return x * jax.nn.sigmoid(x)


def _layer_norm(y, g, b):
    mu = jnp.mean(y, axis=-1, keepdims=True)
    d = y - mu
    var = jnp.mean(d * d, axis=-1, keepdims=True)
    return d * lax.rsqrt(var + NORM_EPS) * g + b


def _mod_kernel(c_ref, w_ref, b_ref, o_ref):
    s = _silu(c_ref[...]).astype(BF16)
    o_ref[...] = _dot(s, w_ref[...].astype(BF16)) + b_ref[...]


def _mod_call(cond, w_ada, b_ada):
    n, d = cond.shape
    cols = w_ada.shape[1]
    tile = 512
    return pl.pallas_call(
        _mod_kernel,
        grid=(cols // tile,),
        in_specs=[pl.BlockSpec((n, d), lambda j: (0, 0)),
                  pl.BlockSpec((d, tile), lambda j: (0, j)),
                  pl.BlockSpec((1, tile), lambda j: (0, j))],
        out_specs=pl.BlockSpec((n, tile), lambda j: (0, j)),
        out_shape=jax.ShapeDtypeStruct((n, cols), F32),
        compiler_params=_params(("arbitrary",)),
        name="mod",
    )(cond, w_ada, b_ada)


def _rope(x, cos_ref, sin_a_ref, sin_b_ref):
    parts = []
    for c in range(GROUP_W // 128):
        sl = slice(c * 128, (c + 1) * 128)
        xc = x[:, sl]
        parts.append(xc * cos_ref[:, sl]
                     + pltpu.roll(xc, 128 - 16, axis=1) * sin_a_ref[:, sl]
                     + pltpu.roll(xc, 16, axis=1) * sin_b_ref[:, sl])
    return jnp.concatenate(parts, axis=1)


def _inproj_kernel(*refs, latent):
    if latent:
        (x_ref, mod_ref, w_ref, cos_ref, sin_a_ref, sin_b_ref,
         q_ref, k_ref, v_ref, rq_ref, xf_ref, xb_ref, ri_ref, rg_ref) = refs
    else:
        (x_ref, mod_ref, w_ref,
         q_ref, k_ref, v_ref, rq_ref, xf_ref, xb_ref, ri_ref, rg_ref, kraw_ref, vraw_ref) = refs
    m = mod_ref[0]
    xm = (x_ref[...] * (1.0 + m[1:2, :]) + m[0:1, :]).astype(BF16)

    def proj(g):
        return _dot(xm, w_ref[:, g * GROUP_W:(g + 1) * GROUP_W])

    aq = proj(0)
    ak = proj(1)
    av = proj(2)
    if latent:
        aq = _rope(aq, cos_ref, sin_a_ref, sin_b_ref)
        ak = _rope(ak, cos_ref, sin_a_ref, sin_b_ref)
    else:
        rows = ak.shape[0]
        kraw = kraw_ref.reshape(rows * 2 * N_HEADS, QK_DIM)
        for j in range(2 * N_HEADS):
            kraw[pl.ds(j, rows, stride=2 * N_HEADS), :] = ak[:, j * QK_DIM:(j + 1) * QK_DIM]
        for h in range(N_HEADS):
            vraw_ref[pl.ds(h, rows, stride=N_HEADS), :] = av[:, h * HEAD_DIM:(h + 1) * HEAD_DIM]
    q_ref[...] = (aq * (QK_DIM ** -0.5 * LOG2_E)).astype(BF16)
    k_ref[...] = ak.astype(BF16)
    v_ref[...] = av.astype(BF16)
    rq_ref[...] = proj(3)
    xf_ref[...] = proj(4)
    xb_ref[...] = proj(5)
    ri_ref[...] = proj(6).astype(BF16)
    rg_ref[...] = _silu(proj(7))


def _inproj_call(x2d, mod3, w_in, rope_tabs, *, latent, seq_len):
    t = x2d.shape[0]
    rb = ROWS_PROJ
    blocks_per_seq = seq_len // rb if latent else 1

    def mod_idx(i):
        return ((1 + i // blocks_per_seq) if latent else 0, 0, 0)

    row_spec = lambda w: pl.BlockSpec((rb, w), lambda i: (i, 0))
    in_specs = [row_spec(D_MODEL),
                pl.BlockSpec((1, N_MOD, D_MODEL), mod_idx),
                pl.BlockSpec(w_in.shape, lambda i: (0, 0))]
    args = [x2d, mod3, w_in]
    if latent:
        in_specs += [pl.BlockSpec((rb, GROUP_W), lambda i: (i % blocks_per_seq, 0))] * 3
        args += list(rope_tabs)
    dts = [BF16, BF16, BF16, F32, F32, F32, BF16, F32]
    out_specs = [row_spec(GROUP_W) for _ in dts]
    out_shape = [jax.ShapeDtypeStruct((t, GROUP_W), dt) for dt in dts]
    if not latent:
        out_specs += [pl.BlockSpec((rb, 2 * N_HEADS, QK_DIM), lambda i: (i, 0, 0)),
                      pl.BlockSpec((rb * N_HEADS, HEAD_DIM), lambda i: (i, 0))]
        out_shape += [jax.ShapeDtypeStruct((t, 2 * N_HEADS, QK_DIM), F32),
                      jax.ShapeDtypeStruct((t * N_HEADS, HEAD_DIM), F32)]
    return pl.pallas_call(
        functools.partial(_inproj_kernel, latent=latent),
        grid=(t // rb,),
        in_specs=in_specs,
        out_specs=out_specs,
        out_shape=out_shape,
        compiler_params=_params(("arbitrary",)),
        name="inproj_lat" if latent else "inproj_ctx",
    )(*args)


def _attn_kernel(*refs, has_cache):
    if has_cache:
        (q_ref, k_ref, v_ref, ck_ref, cv_ref, lq1, lk1, lq2, lk2, g_ref, o_ref) = refs
    else:
        (q_ref, k_ref, v_ref, lq1, lk1, lq2, lk2, g_ref, o_ref) = refs
    lam = (jnp.exp(jnp.sum(lq1[...] * lk1[...], axis=-1, keepdims=True))
           - jnp.exp(jnp.sum(lq2[...] * lk2[...], axis=-1, keepdims=True)) + LAM_INIT)
    qb = q_ref.shape[1]
    lane = lax.broadcasted_iota(jnp.int32, (qb, HEAD_DIM), 1)
    first_map = lane < QK_DIM
    zero = jnp.zeros((), BF16)

    def scores(h):
        sl = slice(h * HEAD_DIM, (h + 1) * HEAD_DIM)
        qh = q_ref[0, :, sl]
        qq = jnp.concatenate([jnp.where(first_map, qh, zero), jnp.where(first_map, zero, qh)], axis=0)
        s_n = _dot_nt(k_ref[0, :, sl], qq)
        s_c = None
        if has_cache:
            past = ck_ref.shape[1]
            ck = ck_ref.at[0].reshape(past * 2 * N_HEADS, QK_DIM)
            kc = jnp.concatenate([ck[pl.ds(2 * h + m, past, stride=2 * N_HEADS), :] for m in range(2)], axis=1)
            s_c = _dot_nt(kc.astype(BF16), qq)
        return s_n, s_c

    ahead = 2
    pending = [scores(h) for h in range(ahead)]
    for h in range(N_HEADS):
        sl = slice(h * HEAD_DIM, (h + 1) * HEAD_DIM)
        s_n, s_c = pending.pop(0)
        if h + ahead < N_HEADS:
            pending.append(scores(h + ahead))
        mx = jnp.max(s_n, axis=0, keepdims=True)
        if has_cache:
            mx = jnp.maximum(mx, jnp.max(s_c, axis=0, keepdims=True))
        e_n = jnp.exp2(s_n - mx)
        den = jnp.sum(e_n, axis=0, keepdims=True)
        ev = _dot_tn(v_ref[0, :, sl], e_n.astype(BF16))
        if has_cache:
            e_c = jnp.exp2(s_c - mx)
            den = den + jnp.sum(e_c, axis=0, keepdims=True)
            vc = cv_ref[0, pl.ds(h, e_c.shape[0], stride=N_HEADS), :]
            ev = ev + _dot_tn(vc.astype(BF16), e_c.astype(BF16))
        inv = 1.0 / den
        o = ev[:, :qb] * inv[:, :qb] - ev[:, qb:] * (inv[:, qb:] * lam)
        o = o * lax.rsqrt(jnp.mean(o * o, axis=0, keepdims=True) + NORM_EPS)
        o_ref[0, :, sl] = (o.T * g_ref[:, sl] * (1.0 - LAM_INIT)).astype(BF16)


def _attn_call(q, k, v, cache, lams, att_g):
    b, l, _ = q.shape
    qb = Q_ROWS
    full = lambda a: pl.BlockSpec((1,) + a.shape[1:], lambda i, j: (i,) + (0,) * (a.ndim - 1))
    in_specs = [pl.BlockSpec((1, qb, GROUP_W), lambda i, j: (i, j, 0)), full(k), full(v)]
    args = [q, k, v]
    if cache is not None:
        in_specs += [full(cache[0]), full(cache[1])]
        args += list(cache)
    in_specs += [pl.BlockSpec((1, QK_DIM), lambda i, j: (0, 0))] * 4
    in_specs += [pl.BlockSpec((1, GROUP_W), lambda i, j: (0, 0))]
    args += list(lams) + [att_g]
    return pl.pallas_call(
        functools.partial(_attn_kernel, has_cache=cache is not None),
        grid=(b, l // qb),
        in_specs=in_specs,
        out_specs=pl.BlockSpec((1, qb, GROUP_W), lambda i, j: (i, j, 0)),
        out_shape=jax.ShapeDtypeStruct((b, l, GROUP_W), BF16),
        compiler_params=_params(("arbitrary", "arbitrary")),
        name="attn_lat" if cache is not None else "attn_ctx",
    )(*args)


HGRN_DIAG = 128
HGRN_MASKED_LEVELS = 5
HGRN_VREG_LEVELS = 3


def _hgrn_masks():
    r = np.arange(HGRN_DIAG)[:, None]
    c = np.arange(HGRN_DIAG)[None, :]
    levels = []
    for lv in range(HGRN_MASKED_LEVELS):
        same = (r >> (lv + 1)) == (c >> (lv + 1))
        levels.append(same & (((r >> lv) & 1) == 1) & (((c >> lv) & 1) == 0))
    fwd = [r == c] + levels
    bwd = [m.T for m in levels]
    return np.stack(fwd + bwd).astype(np.float32)


def _hgrn_block(q, k, g, v, st, mask_ref, sub, *, reverse):
    n = HGRN_ROWS
    n_diag = n // HGRN_DIAG

    def diag_scores(qe, ke, m):
        return [_dot_nt(qe[i * HGRN_DIAG:(i + 1) * HGRN_DIAG], ke[i * HGRN_DIAG:(i + 1) * HGRN_DIAG]) * m
                for i in range(n_diag)]

    def level_mask(lv):
        return mask_ref[(1 + HGRN_MASKED_LEVELS + lv) if reverse else (1 + lv)]

    g3 = g.reshape(n // 8, 8, HEAD_DIM)
    zero3 = jnp.zeros_like(g3)
    pre3, suf3 = (zero3, g3) if reverse else (g3, zero3)
    tot3 = g3
    acc = diag_scores(q.astype(BF16), k.astype(BF16), mask_ref[0])
    for lv in range(HGRN_VREG_LEVELS):
        b = 1 << lv
        upper = ((sub >> lv) & 1) == 1
        e = jnp.exp2(jnp.where(upper, pre3, suf3)).reshape(n, HEAD_DIM)
        part = diag_scores((q * e).astype(BF16), (k * e).astype(BF16), level_mask(lv))
        acc = [a + p for a, p in zip(acc, part)]
        sib = jnp.where(upper, pltpu.roll(tot3, b, axis=1), pltpu.roll(tot3, 8 - b, axis=1))
        pre3 = pre3 + jnp.where(upper, sib, 0.0)
        suf3 = suf3 + jnp.where(upper, 0.0, sib)
        tot3 = tot3 + sib

    pieces = lambda x3: [x3[i] for i in range(n // 8)]
    pre8, suf8, tot8 = pieces(pre3), pieces(suf3), pieces(tot3)
    rows = lambda xs, lo, hi: jnp.concatenate(xs[lo // 8:hi // 8], axis=0) if hi - lo > 8 else xs[lo // 8]
    big_scores = {}
    for lv in range(HGRN_VREG_LEVELS, HGRN_LEVELS):
        b = 1 << lv
        pb = b // 8
        if lv < HGRN_MASKED_LEVELS:
            u = jnp.concatenate([(pre8 if (i // pb) % 2 else suf8)[i] for i in range(n // 8)], axis=0)
            e = jnp.exp2(u)
            part = diag_scores((q * e).astype(BF16), (k * e).astype(BF16), level_mask(lv))
            acc = [a + p for a, p in zip(acc, part)]
        else:
            for j in range(n // (2 * b)):
                lo, mid, hi = 2 * b * j, 2 * b * j + b, 2 * b * (j + 1)
                e_lo = jnp.exp2(rows(suf8, lo, mid))
                e_up = jnp.exp2(rows(pre8, mid, hi))
                x_lo = ((q if reverse else k)[lo:mid] * e_lo).astype(BF16)
                x_up = ((k if reverse else q)[mid:hi] * e_up).astype(BF16)
                big_scores[lv, j] = _dot_nt(x_lo, x_up) if reverse else _dot_nt(x_up, x_lo)
        for j in range(n // (2 * b)):
            lo_p, mid_p, hi_p = 2 * pb * j, 2 * pb * j + pb, 2 * pb * (j + 1)
            t_lo, t_up = tot8[lo_p], tot8[mid_p]
            t_new = t_lo + t_up
            for i in range(lo_p, mid_p):
                suf8[i] = suf8[i] + t_up
                tot8[i] = t_new
            for i in range(mid_p, hi_p):
                pre8[i] = pre8[i] + t_lo
                tot8[i] = t_new
    pre = jnp.concatenate(pre8, axis=0)
    suf = jnp.concatenate(suf8, axis=0)
    q_dec, k_dec = (suf, pre) if reverse else (pre, suf)
    q_in = (q * jnp.exp2(q_dec)).astype(BF16)
    k_out = (k * jnp.exp2(k_dec)).astype(BF16)
    decay = jnp.exp2(tot8[0][0:1, :])

    o_inter = _dot_nt(q_in, st.astype(BF16))
    ds = _dot_tn(v, k_out)
    o_diag = [_dot(acc[i].astype(BF16), v[i * HGRN_DIAG:(i + 1) * HGRN_DIAG]) for i in range(n_diag)]
    o_big = {}
    for (lv, j), a in big_scores.items():
        lo, mid, hi = (2 * j) << lv, (2 * j + 1) << lv, (2 * j + 2) << lv
        o_big[lv, j] = _dot(a.astype(BF16), v[mid:hi] if reverse else v[lo:mid])
    st_new = st * decay + ds
    o = jnp.concatenate(o_diag, axis=0) + o_inter
    for lv in range(HGRN_MASKED_LEVELS, HGRN_LEVELS):
        contrib = []
        for j in range(n >> (lv + 1)):
            zeros = jnp.zeros((1 << lv, HEAD_DIM), F32)
            contrib += [o_big[lv, j], zeros] if reverse else [zeros, o_big[lv, j]]
        o = o + jnp.concatenate(contrib, axis=0)
    return o, st_new


def _hgrn_kernel(*refs, n_blocks, heads, has_state):
    if has_state:
        (q_ref, v_ref, xf_ref, xb_ref, gate_ref, lbf_ref, lbb_ref, g_ref, mask_ref, s0_ref,
         o_ref, acc_ref) = refs
    else:
        (q_ref, v_ref, xf_ref, xb_ref, gate_ref, lbf_ref, lbb_ref, g_ref, mask_ref,
         o_ref, sout_ref, acc_ref) = refs
    n = HGRN_ROWS
    sub = lax.broadcasted_iota(jnp.int32, (n // 8, 8, HEAD_DIM), 1)

    def lower_bound(ref, lanes):
        l0, l1 = ref[0:1, lanes], ref[1:2, lanes]
        mx = jnp.maximum(l0, l1)
        e0, e1 = jnp.exp(l0 - mx), jnp.exp(l1 - mx)
        return e0 / (e0 + e1)

    x_refs = (xf_ref, xb_ref)
    for h in range(heads):
        lanes = slice(h * HEAD_DIM, (h + 1) * HEAD_DIM)
        lbs = [lower_bound(lbf_ref, lanes), lower_bound(lbb_ref, lanes)]
        sts = [s0_ref[0, d, h].T if has_state else jnp.zeros((HEAD_DIM, HEAD_DIM), F32) for d in range(2)]
        for j in range(n_blocks):
            for d in range(2):
                blk = (n_blocks - 1 - j) if d else j
                rows = slice(blk * n, (blk + 1) * n)
                f = lbs[d] + (1.0 - lbs[d]) * jax.nn.sigmoid(x_refs[d][0, rows, lanes])
                o, sts[d] = _hgrn_block(q_ref[0, rows, lanes], 1.0 - f, jnp.log2(f), v_ref[0, rows, lanes],
                                        sts[d], mask_ref, sub, reverse=bool(d))
                acc_ref[d, rows, lanes] = o
        if not has_state:
            for d in range(2):
                sout_ref[0, d, h] = sts[d].T
    for h in range(heads):
        lanes = slice(h * HEAD_DIM, (h + 1) * HEAD_DIM)
        o = acc_ref[0, :, lanes] + acc_ref[1, :, lanes]
        o = o * lax.rsqrt(jnp.mean(o * o, axis=-1, keepdims=True) + NORM_EPS)
        o_ref[0, :, lanes] = (o * g_ref[:, lanes] * gate_ref[0, :, lanes]).astype(BF16)


def _hgrn_call(rq, ri, xf, xb, gate, lbf, lbb, rnn_g, masks, s0):
    b, l, _ = rq.shape
    n_blocks = l // HGRN_ROWS
    heads = max(1, min(N_HEADS, 4 // n_blocks))
    w = heads * HEAD_DIM
    head = lambda: pl.BlockSpec((1, l, w), lambda i, h: (i, 0, h))
    per_head = lambda rows: pl.BlockSpec((rows, w), lambda i, h: (0, h))
    state_spec = pl.BlockSpec((1, 2, heads, HEAD_DIM, HEAD_DIM), lambda i, h: (i, 0, h, 0, 0))
    in_specs = [head(), head(), head(), head(), head(), per_head(2), per_head(2), per_head(1),
                pl.BlockSpec(masks.shape, lambda i, h: (0, 0, 0))]
    args = [rq, ri, xf, xb, gate, lbf, lbb, rnn_g, masks]
    out_specs = [head()]
    out_shape = [jax.ShapeDtypeStruct((b, l, GROUP_W), BF16)]
    if s0 is not None:
        in_specs.append(state_spec)
        args.append(s0)
    else:
        out_specs.append(state_spec)
        out_shape.append(jax.ShapeDtypeStruct((b, 2, N_HEADS, HEAD_DIM, HEAD_DIM), F32))
    return pl.pallas_call(
        functools.partial(_hgrn_kernel, n_blocks=n_blocks, heads=heads, has_state=s0 is not None),
        grid=(b, N_HEADS // heads),
        in_specs=in_specs,
        out_specs=out_specs,
        out_shape=out_shape,
        scratch_shapes=[pltpu.VMEM((2, l, w), F32)],
        compiler_params=_params(("arbitrary", "arbitrary")),
        name="hgrn_lat" if s0 is not None else "hgrn_ctx",
    )(*args)


def _tail_kernel(att_ref, rnn_ref, x_ref, mod_ref, wo_ref, g1_ref, b1_ref, wu_ref, cw_ref, cb_ref, wd_ref,
                 g2_ref, b2_ref, o_ref, x1_ref, xm2_ref, hid_ref, *, seq_len):
    rows = x_ref.shape[0]
    m = mod_ref[0]
    mix = _dot(att_ref[...], wo_ref[0:GROUP_W, :]) + _dot(rnn_ref[...], wo_ref[GROUP_W:2 * GROUP_W, :])
    x1 = _layer_norm(DEEPNORM_ALPHA * x_ref[...] + m[2:3, :] * mix, g1_ref[...], b1_ref[...])
    x1_ref[...] = x1
    xm2_ref[...] = (x1 * (1.0 + m[4:5, :]) + m[3:4, :]).astype(BF16)

    sub = lax.broadcasted_iota(jnp.int32, (8, FF_TILE), 0)

    def shifted(h, shift, edge_row):
        r = pltpu.roll(h, shift, axis=0)
        parts = []
        for s in range(rows // seq_len):
            edge = s * seq_len + (edge_row // 8) * 8
            fixed = jnp.where(sub == edge_row % 8, 0.0, r[edge:edge + 8])
            parts += [r[s * seq_len:edge], fixed, r[edge + 8:(s + 1) * seq_len]]
        return jnp.concatenate([p for p in parts if p.shape[0]], axis=0)

    def conv(h, cols):
        prev = shifted(h, 1, 0)
        nxt = shifted(h, rows - 1, seq_len - 1)
        return prev * cw_ref[0:1, cols] + h * cw_ref[1:2, cols] + nxt * cw_ref[2:3, cols] + cb_ref[:, cols]

    for j in range(D_FF // FF_TILE):
        cols_a = slice(j * FF_TILE, (j + 1) * FF_TILE)
        cols_u = slice(D_FF + j * FF_TILE, D_FF + (j + 1) * FF_TILE)
        a = conv(_dot(xm2_ref[...], wu_ref[:, cols_a]), cols_a)
        u = conv(_dot(xm2_ref[...], wu_ref[:, cols_u]), cols_u)
        hid_ref[:, cols_a] = (_silu(a) * u).astype(BF16)
    y = DEEPNORM_ALPHA * x1_ref[...] + m[5:6, :] * _dot(hid_ref[...], wd_ref[...])
    o_ref[...] = _layer_norm(y, g2_ref[...], b2_ref[...])


def _tail_call(att, rnn, x2d, mod3, w_out, ln1_g, ln1_b, w_up, conv_w, conv_b, w_down, ln2_g, ln2_b,
               *, latent, seq_len):
    t = x2d.shape[0]
    rb = ROWS_FFN
    assert rb % seq_len == 0

    def mod_idx(i):
        return ((1 + i * rb // seq_len) if latent else 0, 0, 0)

    row_spec = lambda w: pl.BlockSpec((rb, w), lambda i: (i, 0))
    resident = lambda a: pl.BlockSpec(a.shape, lambda i: (0, 0), pipeline_mode=pl.Buffered(1))
    return pl.pallas_call(
        functools.partial(_tail_kernel, seq_len=seq_len),
        grid=(t // rb,),
        in_specs=[row_spec(GROUP_W), row_spec(GROUP_W), row_spec(D_MODEL),
                  pl.BlockSpec((1, N_MOD, D_MODEL), mod_idx),
                  resident(w_out), resident(ln1_g), resident(ln1_b),
                  resident(w_up), resident(conv_w), resident(conv_b), resident(w_down),
                  resident(ln2_g), resident(ln2_b)],
        out_specs=row_spec(D_MODEL),
        out_shape=jax.ShapeDtypeStruct((t, D_MODEL), F32),
        scratch_shapes=[pltpu.VMEM((rb, D_MODEL), F32), pltpu.VMEM((rb, D_MODEL), BF16),
                        pltpu.VMEM((rb, D_FF), BF16)],
        compiler_params=_params(("arbitrary",)),
        name="tail_lat" if latent else "tail_ctx",
    )(att, rnn, x2d, mod3, w_out, ln1_g, ln1_b, w_up, conv_w, conv_b, w_down, ln2_g, ln2_b)


def _rope_tables(seq_len):
    quarter = QK_DIM // 4
    freqs = 1.0 / (ROPE_BASE ** (np.arange(quarter, dtype=np.float64) / quarter))
    t = np.arange(seq_len)
    ang_r = (t // GRID_W)[:, None] * freqs
    ang_c = (t % GRID_W)[:, None] * freqs
    zeros = np.zeros_like(ang_r)

    def tile(parts):
        return jnp.asarray(np.tile(np.concatenate(parts, axis=-1), (1, GROUP_W // QK_DIM)).astype(np.float32))

    cos = tile([np.cos(ang_r), np.cos(ang_r), np.cos(ang_c), np.cos(ang_c)])
    sin_a = tile([-np.sin(ang_r), zeros, -np.sin(ang_c), zeros])
    sin_b = tile([zeros, np.sin(ang_r), zeros, np.sin(ang_c)])
    return cos, sin_a, sin_b


def kernel(x_prompt, x_sample, cache_k, cache_v, state_rnn, c, c_ctx, w_ada, b_ada, w_in, lambda_q1, lambda_k1, lambda_q2, lambda_k2, lb_fwd_logits, lb_bwd_logits, att_norm_g, rnn_norm_g, w_out, ln1_g, ln1_b, w_up, conv_w, conv_b, w_down, ln2_g, ln2_b):
    assert w_ada.shape[0] == DEPTH
    bp, lp, d = x_prompt.shape
    bs, ls, _ = x_sample.shape
    past = cache_k.shape[2]

    cond = jnp.zeros((16, d), F32).at[0].set(c_ctx).at[1:1 + bs].set(c)
    mod3 = _mod_call(cond, w_ada[0], b_ada).reshape(16, N_MOD, d)

    w_in_b = w_in[0].astype(BF16)
    w_out_b = w_out[0].astype(BF16)
    w_up_b = w_up[0].astype(BF16)
    w_down_b = w_down[0].astype(BF16)
    lams = (lambda_q1, lambda_k1, lambda_q2, lambda_k2)
    masks = jnp.asarray(_hgrn_masks())
    row = lambda a: a.reshape(1, -1)

    def trunk(x, *, latent, cache, s0):
        b, l, _ = x.shape
        x2d = x.reshape(b * l, d)
        outs = _inproj_call(x2d, mod3, w_in_b, _rope_tables(l) if latent else None,
                            latent=latent, seq_len=l)
        q, k, v, rq, xf, xb, ri, rg = (o.reshape(b, l, GROUP_W) for o in outs[:8])
        att = _attn_call(q, k, v, cache, lams, att_norm_g)
        hg = _hgrn_call(rq, ri, xf, xb, rg, lb_fwd_logits, lb_bwd_logits, rnn_norm_g, masks, s0)
        rnn = hg[0]
        y = _tail_call(att.reshape(b * l, GROUP_W), rnn.reshape(b * l, GROUP_W), x2d, mod3, w_out_b, ln1_g,
                       ln1_b, w_up_b, conv_w[0], conv_b, w_down_b, ln2_g, ln2_b, latent=latent, seq_len=l)
        return y.reshape(b, l, d), outs[8:], hg[1:]

    y_p, (k_raw, v_raw), (s_new,) = trunk(x_prompt, latent=False, cache=None, s0=None)
    cache = (cache_k.reshape(bs, past, 2 * N_HEADS, QK_DIM), cache_v.reshape(bs, past * N_HEADS, HEAD_DIM))
    y_s, _, _ = trunk(x_sample, latent=True, cache=cache,
                      s0=state_rnn.reshape(bs, 2, N_HEADS, HEAD_DIM, HEAD_DIM))

    new_cache_k = k_raw.reshape(bp, DEPTH, lp, N_HEADS, 2, QK_DIM)
    new_cache_v = v_raw.reshape(bp, DEPTH, lp, N_HEADS, HEAD_DIM)
    new_state = s_new.reshape(bp, DEPTH, 2, N_HEADS, HEAD_DIM, HEAD_DIM)
    return (y_p, y_s, new_cache_k, new_cache_v, new_state)
```

```python
import functools
import math

import jax
import jax.numpy as jnp
import numpy as np
from jax import lax
from jax.experimental import pallas as pl
from jax.experimental.pallas import tpu as pltpu

D_MODEL = 1024
GRID_W = 64
N_HEADS = 4
HEAD_DIM = 128
QK_DIM = 64
GROUP_W = 512
N_GROUPS = 8
D_FF = 2816
N_MOD = 6
ROPE_BASE = 10000.0
DEPTH = 1
DEEPNORM_ALPHA = (2.0 * DEPTH) ** 0.25
NORM_EPS = 1e-5
LAM_INIT = 0.8 - 0.6 * math.exp(-0.3 * 0)
LOG2_E = math.log2(math.e)

V7X_VMEM_BYTES = 64 * 1024 * 1024
VMEM_LIMIT = V7X_VMEM_BYTES * 15 // 16

ROWS_PROJ = 512
ROWS_FFN = 1024
FF_TILE = 256
Q_ROWS = 256
HGRN_ROWS = 256
HGRN_LEVELS = int(math.log2(HGRN_ROWS))

F32 = jnp.float32
BF16 = jnp.bfloat16


def _params(semantics):
    return pltpu.CompilerParams(dimension_semantics=semantics, vmem_limit_bytes=VMEM_LIMIT)


def _dot(a, b):
    return jnp.dot(a, b, preferred_element_type=F32)


def _dot_nt(a, b):
    return lax.dot_general(a, b, (((1,), (1,)), ((), ())), preferred_element_type=F32)


def _dot_tn(a, b):
    return lax.dot_general(a, b, (((0,), (0,)), ((), ())), preferred_element_type=F32)


def _silu(x):
    return x * jax.nn.sigmoid(x)


def _layer_norm(y, g, b):
    mu = jnp.mean(y, axis=-1, keepdims=True)
    d = y - mu
    var = jnp.mean(d * d, axis=-1, keepdims=True)
    return d * lax.rsqrt(var + NORM_EPS) * g + b


def _mod_kernel(c_ref, w_ref, b_ref, o_ref):
    s = _silu(c_ref[...]).astype(BF16)
    o_ref[...] = _dot(s, w_ref[...].astype(BF16)) + b_ref[...]


def _mod_call(cond, w_ada, b_ada):
    n, d = cond.shape
    cols = w_ada.shape[1]
    tile = 512
    return pl.pallas_call(
        _mod_kernel,
        grid=(cols // tile,),
        in_specs=[pl.BlockSpec((n, d), lambda j: (0, 0)),
                  pl.BlockSpec((d, tile), lambda j: (0, j)),
                  pl.BlockSpec((1, tile), lambda j: (0, j))],
        out_specs=pl.BlockSpec((n, tile), lambda j: (0, j)),
        out_shape=jax.ShapeDtypeStruct((n, cols), F32),
        compiler_params=_params(("arbitrary",)),
        name="mod",
    )(cond, w_ada, b_ada)


def _rope(x, cos_ref, sin_a_ref, sin_b_ref):
    parts = []
    for c in range(GROUP_W // 128):
        sl = slice(c * 128, (c + 1) * 128)
        xc = x[:, sl]
        parts.append(xc * cos_ref[:, sl]
                     + pltpu.roll(xc, 128 - 16, axis=1) * sin_a_ref[:, sl]
                     + pltpu.roll(xc, 16, axis=1) * sin_b_ref[:, sl])
    return jnp.concatenate(parts, axis=1)


def _inproj_kernel(*refs, latent):
    if latent:
        (x_ref, mod_ref, w_ref, cos_ref, sin_a_ref, sin_b_ref,
         q_ref, k_ref, v_ref, rq_ref, xf_ref, xb_ref, ri_ref, rg_ref) = refs
    else:
        (x_ref, mod_ref, w_ref,
         q_ref, k_ref, v_ref, rq_ref, xf_ref, xb_ref, ri_ref, rg_ref, kraw_ref, vraw_ref) = refs
    m = mod_ref[0]
    xm = (x_ref[...] * (1.0 + m[1:2, :]) + m[0:1, :]).astype(BF16)

    def proj(g):
        return _dot(xm, w_ref[:, g * GROUP_W:(g + 1) * GROUP_W])

    aq = proj(0)
    ak = proj(1)
    av = proj(2)
    if latent:
        aq = _rope(aq, cos_ref, sin_a_ref, sin_b_ref)
        ak = _rope(ak, cos_ref, sin_a_ref, sin_b_ref)
    else:
        rows = ak.shape[0]
        for j in range(2 * N_HEADS):
            kraw_ref[pl.ds(j, rows, stride=2 * N_HEADS), :] = ak[:, j * QK_DIM:(j + 1) * QK_DIM]
        for h in range(N_HEADS):
            vraw_ref[pl.ds(h, rows, stride=N_HEADS), :] = av[:, h * HEAD_DIM:(h + 1) * HEAD_DIM]
    q_ref[...] = (aq * (QK_DIM ** -0.5 * LOG2_E)).astype(BF16)
    k_ref[...] = ak.astype(BF16)
    v_ref[...] = av.astype(BF16)
    rq_ref[...] = proj(3)
    xf_ref[...] = proj(4)
    xb_ref[...] = proj(5)
    ri_ref[...] = proj(6).astype(BF16)
    rg_ref[...] = _silu(proj(7))


def _inproj_call(x2d, mod3, w_in, rope_tabs, *, latent, seq_len):
    t = x2d.shape[0]
    rb = ROWS_PROJ
    blocks_per_seq = seq_len // rb if latent else 1

    def mod_idx(i):
        return ((1 + i // blocks_per_seq) if latent else 0, 0, 0)

    row_spec = lambda w: pl.BlockSpec((rb, w), lambda i: (i, 0))
    in_specs = [row_spec(D_MODEL),
                pl.BlockSpec((1, N_MOD, D_MODEL), mod_idx),
                pl.BlockSpec(w_in.shape, lambda i: (0, 0))]
    args = [x2d, mod3, w_in]
    if latent:
        in_specs += [pl.BlockSpec((rb, GROUP_W), lambda i: (i % blocks_per_seq, 0))] * 3
        args += list(rope_tabs)
    dts = [BF16, BF16, BF16, F32, F32, F32, BF16, F32]
    out_specs = [row_spec(GROUP_W) for _ in dts]
    out_shape = [jax.ShapeDtypeStruct((t, GROUP_W), dt) for dt in dts]
    if not latent:
        out_specs += [pl.BlockSpec((rb * 2 * N_HEADS, QK_DIM), lambda i: (i, 0)),
                      pl.BlockSpec((rb * N_HEADS, HEAD_DIM), lambda i: (i, 0))]
        out_shape += [jax.ShapeDtypeStruct((t * 2 * N_HEADS, QK_DIM), F32),
                      jax.ShapeDtypeStruct((t * N_HEADS, HEAD_DIM), F32)]
    return pl.pallas_call(
        functools.partial(_inproj_kernel, latent=latent),
        grid=(t // rb,),
        in_specs=in_specs,
        out_specs=out_specs,
        out_shape=out_shape,
        compiler_params=_params(("arbitrary",)),
        name="inproj_lat" if latent else "inproj_ctx",
    )(*args)


def _attn_kernel(*refs, has_cache):
    if has_cache:
        (q_ref, k_ref, v_ref, ck_ref, cv_ref, lq1, lk1, lq2, lk2, g_ref, o_ref) = refs
    else:
        (q_ref, k_ref, v_ref, lq1, lk1, lq2, lk2, g_ref, o_ref) = refs
    lam = (jnp.exp(jnp.sum(lq1[...] * lk1[...], axis=-1, keepdims=True))
           - jnp.exp(jnp.sum(lq2[...] * lk2[...], axis=-1, keepdims=True)) + LAM_INIT)
    qb = q_ref.shape[1]
    lane = lax.broadcasted_iota(jnp.int32, (qb, HEAD_DIM), 1)
    first_map = lane < QK_DIM
    zero = jnp.zeros((), BF16)

    def scores(h):
        sl = slice(h * HEAD_DIM, (h + 1) * HEAD_DIM)
        qh = q_ref[0, :, sl]
        qq = jnp.concatenate([jnp.where(first_map, qh, zero), jnp.where(first_map, zero, qh)], axis=0)
        s_n = _dot_nt(k_ref[0, :, sl], qq)
        s_c = None
        if has_cache:
            s_c = _dot_nt(ck_ref[0, sl, :].astype(BF16).T, qq)
        return s_n, s_c

    ahead = 3
    pending = [scores(h) for h in range(ahead)]
    for h in range(N_HEADS):
        sl = slice(h * HEAD_DIM, (h + 1) * HEAD_DIM)
        s_n, s_c = pending.pop(0)
        if h + ahead < N_HEADS:
            pending.append(scores(h + ahead))
        mx = jnp.max(s_n, axis=0, keepdims=True)
        if has_cache:
            mx = jnp.maximum(mx, jnp.max(s_c, axis=0, keepdims=True))
        e_n = jnp.exp2(s_n - mx)
        den = jnp.sum(e_n, axis=0, keepdims=True)
        ev = _dot_tn(v_ref[0, :, sl], e_n.astype(BF16))
        if has_cache:
            e_c = jnp.exp2(s_c - mx)
            den = den + jnp.sum(e_c, axis=0, keepdims=True)
            vc = cv_ref[0, pl.ds(h, e_c.shape[0], stride=N_HEADS), :]
            ev = ev + _dot_tn(vc.astype(BF16), e_c.astype(BF16))
        inv = 1.0 / den
        o = ev[:, :qb] * inv[:, :qb] - ev[:, qb:] * (inv[:, qb:] * lam)
        o = o * lax.rsqrt(jnp.mean(o * o, axis=0, keepdims=True) + NORM_EPS)
        o_ref[0, :, sl] = (o.T * g_ref[:, sl] * (1.0 - LAM_INIT)).astype(BF16)


def _attn_call(q, k, v, cache, lams, att_g):
    b, l, _ = q.shape
    qb = Q_ROWS
    full = lambda a: pl.BlockSpec((1,) + a.shape[1:], lambda i, j: (i,) + (0,) * (a.ndim - 1))
    in_specs = [pl.BlockSpec((1, qb, GROUP_W), lambda i, j: (i, j, 0)), full(k), full(v)]
    args = [q, k, v]
    if cache is not None:
        in_specs += [full(cache[0]), full(cache[1])]
        args += list(cache)
    in_specs += [pl.BlockSpec((1, QK_DIM), lambda i, j: (0, 0))] * 4
    in_specs += [pl.BlockSpec((1, GROUP_W), lambda i, j: (0, 0))]
    args += list(lams) + [att_g]
    return pl.pallas_call(
        functools.partial(_attn_kernel, has_cache=cache is not None),
        grid=(b, l // qb),
        in_specs=in_specs,
        out_specs=pl.BlockSpec((1, qb, GROUP_W), lambda i, j: (i, j, 0)),
        out_shape=jax.ShapeDtypeStruct((b, l, GROUP_W), BF16),
        compiler_params=_params(("arbitrary", "arbitrary")),
        name="attn_lat" if cache is not None else "attn_ctx",
    )(*args)


HGRN_DIAG = 128
HGRN_MASKED_LEVELS = 5
HGRN_VREG_LEVELS = 3


def _hgrn_masks():
    r = np.arange(HGRN_DIAG)[:, None]
    c = np.arange(HGRN_DIAG)[None, :]
    levels = []
    for lv in range(HGRN_MASKED_LEVELS):
        same = (r >> (lv + 1)) == (c >> (lv + 1))
        levels.append(same & (((r >> lv) & 1) == 1) & (((c >> lv) & 1) == 0))
    fwd = [r == c] + levels
    bwd = [m.T for m in levels]
    return np.stack(fwd + bwd).astype(np.float32)


def _hgrn_block(q, k, g, v, st, mask_ref, sub, *, reverse):
    n = HGRN_ROWS
    n_diag = n // HGRN_DIAG

    def diag_scores(qe, ke, m):
        return [_dot_nt(qe[i * HGRN_DIAG:(i + 1) * HGRN_DIAG], ke[i * HGRN_DIAG:(i + 1) * HGRN_DIAG]) * m
                for i in range(n_diag)]

    def level_mask(lv):
        return mask_ref[(1 + HGRN_MASKED_LEVELS + lv) if reverse else (1 + lv)]

    g3 = g.reshape(n // 8, 8, HEAD_DIM)
    zero3 = jnp.zeros_like(g3)
    pre3, suf3 = (zero3, g3) if reverse else (g3, zero3)
    tot3 = g3
    qb, kb = q.astype(BF16), k.astype(BF16)
    acc = diag_scores(qb, kb, mask_ref[0])
    for lv in range(HGRN_VREG_LEVELS):
        b = 1 << lv
        upper = ((sub >> lv) & 1) == 1
        e = jnp.exp2(jnp.where(upper, pre3, suf3)).reshape(n, HEAD_DIM)
        eb = e.astype(BF16)
        part = diag_scores(qb * eb, kb * eb, level_mask(lv))
        acc = [a + p for a, p in zip(acc, part)]
        sib = jnp.where(upper, pltpu.roll(tot3, b, axis=1), pltpu.roll(tot3, 8 - b, axis=1))
        pre3 = pre3 + jnp.where(upper, sib, 0.0)
        suf3 = suf3 + jnp.where(upper, 0.0, sib)
        tot3 = tot3 + sib

    pieces = lambda x3: [x3[i] for i in range(n // 8)]
    pre8, suf8, tot8 = pieces(pre3), pieces(suf3), pieces(tot3)
    rows = lambda xs, lo, hi: jnp.concatenate(xs[lo // 8:hi // 8], axis=0) if hi - lo > 8 else xs[lo // 8]
    big_scores = {}
    for lv in range(HGRN_VREG_LEVELS, HGRN_LEVELS):
        b = 1 << lv
        pb = b // 8
        if lv < HGRN_MASKED_LEVELS:
            u = jnp.concatenate([(pre8 if (i // pb) % 2 else suf8)[i] for i in range(n // 8)], axis=0)
            e = jnp.exp2(u)
            eb = e.astype(BF16)
            part = diag_scores(qb * eb, kb * eb, level_mask(lv))
            acc = [a + p for a, p in zip(acc, part)]
        else:
            for j in range(n // (2 * b)):
                lo, mid, hi = 2 * b * j, 2 * b * j + b, 2 * b * (j + 1)
                e_lo = jnp.exp2(rows(suf8, lo, mid))
                e_up = jnp.exp2(rows(pre8, mid, hi))
                x_lo = (qb if reverse else kb)[lo:mid] * e_lo.astype(BF16)
                x_up = (kb if reverse else qb)[mid:hi] * e_up.astype(BF16)
                big_scores[lv, j] = _dot_nt(x_lo, x_up) if reverse else _dot_nt(x_up, x_lo)
        for j in range(n // (2 * b)):
            lo_p, mid_p, hi_p = 2 * pb * j, 2 * pb * j + pb, 2 * pb * (j + 1)
            t_lo, t_up = tot8[lo_p], tot8[mid_p]
            t_new = t_lo + t_up
            for i in range(lo_p, mid_p):
                suf8[i] = suf8[i] + t_up
                tot8[i] = t_new
            for i in range(mid_p, hi_p):
                pre8[i] = pre8[i] + t_lo
                tot8[i] = t_new
    pre = jnp.concatenate(pre8, axis=0)
    suf = jnp.concatenate(suf8, axis=0)
    q_dec, k_dec = (suf, pre) if reverse else (pre, suf)
    q_in = qb * jnp.exp2(q_dec).astype(BF16)
    k_out = kb * jnp.exp2(k_dec).astype(BF16)
    decay = jnp.exp2(tot8[0][0:1, :])

    o_inter = _dot_nt(q_in, st.astype(BF16))
    ds = _dot_tn(v, k_out)
    o_diag = [_dot(acc[i].astype(BF16), v[i * HGRN_DIAG:(i + 1) * HGRN_DIAG]) for i in range(n_diag)]
    o_big = {}
    for (lv, j), a in big_scores.items():
        lo, mid, hi = (2 * j) << lv, (2 * j + 1) << lv, (2 * j + 2) << lv
        o_big[lv, j] = _dot(a.astype(BF16), v[mid:hi] if reverse else v[lo:mid])
    st_new = st * decay + ds
    o = jnp.concatenate(o_diag, axis=0) + o_inter
    for lv in range(HGRN_MASKED_LEVELS, HGRN_LEVELS):
        contrib = []
        for j in range(n >> (lv + 1)):
            zeros = jnp.zeros((1 << lv, HEAD_DIM), F32)
            contrib += [o_big[lv, j], zeros] if reverse else [zeros, o_big[lv, j]]
        o = o + jnp.concatenate(contrib, axis=0)
    return o, st_new


def _hgrn_kernel(*refs, n_blocks, heads, has_state):
    if has_state:
        (q_ref, v_ref, xf_ref, xb_ref, gate_ref, lbf_ref, lbb_ref, g_ref, mask_ref, s0_ref,
         o_ref, acc_ref) = refs
    else:
        (q_ref, v_ref, xf_ref, xb_ref, gate_ref, lbf_ref, lbb_ref, g_ref, mask_ref,
         o_ref, sout_ref, acc_ref) = refs
    n = HGRN_ROWS
    sub = lax.broadcasted_iota(jnp.int32, (n // 8, 8, HEAD_DIM), 1)

    def lower_bound(ref, lanes):
        l0, l1 = ref[0:1, lanes], ref[1:2, lanes]
        mx = jnp.maximum(l0, l1)
        e0, e1 = jnp.exp(l0 - mx), jnp.exp(l1 - mx)
        return e0 / (e0 + e1)

    x_refs = (xf_ref, xb_ref)
    for h in range(heads):
        lanes = slice(h * HEAD_DIM, (h + 1) * HEAD_DIM)
        lbs = [lower_bound(lbf_ref, lanes), lower_bound(lbb_ref, lanes)]
        sts = [s0_ref[0, d, h].T if has_state else jnp.zeros((HEAD_DIM, HEAD_DIM), F32) for d in range(2)]
        for j in range(n_blocks):
            for d in range(2):
                blk = (n_blocks - 1 - j) if d else j
                rows = slice(blk * n, (blk + 1) * n)
                f = lbs[d] + (1.0 - lbs[d]) * jax.nn.sigmoid(x_refs[d][0, rows, lanes])
                o, sts[d] = _hgrn_block(q_ref[0, rows, lanes], 1.0 - f, jnp.log2(f), v_ref[0, rows, lanes],
                                        sts[d], mask_ref, sub, reverse=bool(d))
                acc_ref[d, rows, lanes] = o
        if not has_state:
            for d in range(2):
                sout_ref[0, d, h] = sts[d].T
    for h in range(heads):
        lanes = slice(h * HEAD_DIM, (h + 1) * HEAD_DIM)
        o = acc_ref[0, :, lanes] + acc_ref[1, :, lanes]
        o = o * lax.rsqrt(jnp.mean(o * o, axis=-1, keepdims=True) + NORM_EPS)
        o_ref[0, :, lanes] = (o * g_ref[:, lanes] * gate_ref[0, :, lanes]).astype(BF16)


def _hgrn_call(rq, ri, xf, xb, gate, lbf, lbb, rnn_g, masks, s0):
    b, l, _ = rq.shape
    n_blocks = l // HGRN_ROWS
    heads = max(1, min(N_HEADS, 4 // n_blocks))
    w = heads * HEAD_DIM
    head = lambda: pl.BlockSpec((1, l, w), lambda i, h: (i, 0, h))
    per_head = lambda rows: pl.BlockSpec((rows, w), lambda i, h: (0, h))
    state_spec = pl.BlockSpec((1, 2, heads, HEAD_DIM, HEAD_DIM), lambda i, h: (i, 0, h, 0, 0))
    in_specs = [head(), head(), head(), head(), head(), per_head(2), per_head(2), per_head(1),
                pl.BlockSpec(masks.shape, lambda i, h: (0, 0, 0))]
    args = [rq, ri, xf, xb, gate, lbf, lbb, rnn_g, masks]
    out_specs = [head()]
    out_shape = [jax.ShapeDtypeStruct((b, l, GROUP_W), BF16)]
    if s0 is not None:
        in_specs.append(state_spec)
        args.append(s0)
    else:
        out_specs.append(state_spec)
        out_shape.append(jax.ShapeDtypeStruct((b, 2, N_HEADS, HEAD_DIM, HEAD_DIM), F32))
    return pl.pallas_call(
        functools.partial(_hgrn_kernel, n_blocks=n_blocks, heads=heads, has_state=s0 is not None),
        grid=(b, N_HEADS // heads),
        in_specs=in_specs,
        out_specs=out_specs,
        out_shape=out_shape,
        scratch_shapes=[pltpu.VMEM((2, l, w), F32)],
        compiler_params=_params(("arbitrary", "arbitrary")),
        name="hgrn_lat" if s0 is not None else "hgrn_ctx",
    )(*args)


def _tail_kernel(att_ref, rnn_ref, x_ref, mod_ref, wo_ref, g1_ref, b1_ref, wu_ref, cw_ref, cb_ref, wd_ref,
                 g2_ref, b2_ref, o_ref, x1_ref, xm2_ref, hid_ref, *, seq_len):
    rows = x_ref.shape[0]
    m = mod_ref[0]
    mix = _dot(att_ref[...], wo_ref[0:GROUP_W, :]) + _dot(rnn_ref[...], wo_ref[GROUP_W:2 * GROUP_W, :])
    x1 = _layer_norm(DEEPNORM_ALPHA * x_ref[...] + m[2:3, :] * mix, g1_ref[...], b1_ref[...])
    x1_ref[...] = x1
    xm2_ref[...] = (x1 * (1.0 + m[4:5, :]) + m[3:4, :]).astype(BF16)

    sub = lax.broadcasted_iota(jnp.int32, (8, FF_TILE), 0)

    def shifted(h, shift, edge_row):
        r = pltpu.roll(h, shift, axis=0)
        parts = []
        for s in range(rows // seq_len):
            edge = s * seq_len + (edge_row // 8) * 8
            fixed = jnp.where(sub == edge_row % 8, 0.0, r[edge:edge + 8])
            parts += [r[s * seq_len:edge], fixed, r[edge + 8:(s + 1) * seq_len]]
        return jnp.concatenate([p for p in parts if p.shape[0]], axis=0)

    def conv(h, cols):
        prev = shifted(h, 1, 0)
        nxt = shifted(h, rows - 1, seq_len - 1)
        return prev * cw_ref[0:1, cols] + h * cw_ref[1:2, cols] + nxt * cw_ref[2:3, cols] + cb_ref[:, cols]

    for j in range(D_FF // FF_TILE):
        cols_a = slice(j * FF_TILE, (j + 1) * FF_TILE)
        cols_u = slice(D_FF + j * FF_TILE, D_FF + (j + 1) * FF_TILE)
        a = conv(_dot(xm2_ref[...], wu_ref[:, cols_a]), cols_a)
        u = conv(_dot(xm2_ref[...], wu_ref[:, cols_u]), cols_u)
        hid_ref[:, cols_a] = (_silu(a) * u).astype(BF16)
    y = DEEPNORM_ALPHA * x1_ref[...] + m[5:6, :] * _dot(hid_ref[...], wd_ref[...])
    o_ref[...] = _layer_norm(y, g2_ref[...], b2_ref[...])


def _tail_call(att, rnn, x2d, mod3, w_out, ln1_g, ln1_b, w_up, conv_w, conv_b, w_down, ln2_g, ln2_b,
               *, latent, seq_len):
    t = x2d.shape[0]
    rb = ROWS_FFN
    assert rb % seq_len == 0

    def mod_idx(i):
        return ((1 + i * rb // seq_len) if latent else 0, 0, 0)

    row_spec = lambda w: pl.BlockSpec((rb, w), lambda i: (i, 0))
    resident = lambda a: pl.BlockSpec(a.shape, lambda i: (0, 0), pipeline_mode=pl.Buffered(1))
    return pl.pallas_call(
        functools.partial(_tail_kernel, seq_len=seq_len),
        grid=(t // rb,),
        in_specs=[row_spec(GROUP_W), row_spec(GROUP_W), row_spec(D_MODEL),
                  pl.BlockSpec((1, N_MOD, D_MODEL), mod_idx),
                  resident(w_out), resident(ln1_g), resident(ln1_b),
                  resident(w_up), resident(conv_w), resident(conv_b), resident(w_down),
                  resident(ln2_g), resident(ln2_b)],
        out_specs=row_spec(D_MODEL),
        out_shape=jax.ShapeDtypeStruct((t, D_MODEL), F32),
        scratch_shapes=[pltpu.VMEM((rb, D_MODEL), F32), pltpu.VMEM((rb, D_MODEL), BF16),
                        pltpu.VMEM((rb, D_FF), BF16)],
        compiler_params=_params(("arbitrary",)),
        name="tail_lat" if latent else "tail_ctx",
    )(att, rnn, x2d, mod3, w_out, ln1_g, ln1_b, w_up, conv_w, conv_b, w_down, ln2_g, ln2_b)


def _rope_tables(seq_len):
    quarter = QK_DIM // 4
    freqs = 1.0 / (ROPE_BASE ** (np.arange(quarter, dtype=np.float64) / quarter))
    t = np.arange(seq_len)
    ang_r = (t // GRID_W)[:, None] * freqs
    ang_c = (t % GRID_W)[:, None] * freqs
    zeros = np.zeros_like(ang_r)

    def tile(parts):
        return jnp.asarray(np.tile(np.concatenate(parts, axis=-1), (1, GROUP_W // QK_DIM)).astype(np.float32))

    cos = tile([np.cos(ang_r), np.cos(ang_r), np.cos(ang_c), np.cos(ang_c)])
    sin_a = tile([-np.sin(ang_r), zeros, -np.sin(ang_c), zeros])
    sin_b = tile([zeros, np.sin(ang_r), zeros, np.sin(ang_c)])
    return cos, sin_a, sin_b


def kernel(x_prompt, x_sample, cache_k, cache_v, state_rnn, c, c_ctx, w_ada, b_ada, w_in, lambda_q1, lambda_k1, lambda_q2, lambda_k2, lb_fwd_logits, lb_bwd_logits, att_norm_g, rnn_norm_g, w_out, ln1_g, ln1_b, w_up, conv_w, conv_b, w_down, ln2_g, ln2_b):
    assert w_ada.shape[0] == DEPTH
    bp, lp, d = x_prompt.shape
    bs, ls, _ = x_sample.shape
    past = cache_k.shape[2]

    cond = jnp.zeros((16, d), F32).at[0].set(c_ctx).at[1:1 + bs].set(c)
    mod3 = _mod_call(cond, w_ada[0], b_ada).reshape(16, N_MOD, d)

    w_in_b = w_in[0].astype(BF16)
    w_out_b = w_out[0].astype(BF16)
    w_up_b = w_up[0].astype(BF16)
    w_down_b = w_down[0].astype(BF16)
    lams = (lambda_q1, lambda_k1, lambda_q2, lambda_k2)
    masks = jnp.asarray(_hgrn_masks())
    row = lambda a: a.reshape(1, -1)

    def trunk(x, *, latent, cache, s0):
        b, l, _ = x.shape
        x2d = x.reshape(b * l, d)
        outs = _inproj_call(x2d, mod3, w_in_b, _rope_tables(l) if latent else None,
                            latent=latent, seq_len=l)
        q, k, v, rq, xf, xb, ri, rg = (o.reshape(b, l, GROUP_W) for o in outs[:8])
        att = _attn_call(q, k, v, cache, lams, att_norm_g)
        hg = _hgrn_call(rq, ri, xf, xb, rg, lb_fwd_logits, lb_bwd_logits, rnn_norm_g, masks, s0)
        rnn = hg[0]
        y = _tail_call(att.reshape(b * l, GROUP_W), rnn.reshape(b * l, GROUP_W), x2d, mod3, w_out_b, ln1_g,
                       ln1_b, w_up_b, conv_w[0], conv_b, w_down_b, ln2_g, ln2_b, latent=latent, seq_len=l)
        return y.reshape(b, l, d), outs[8:], hg[1:]

    y_p, (k_raw, v_raw), (s_new,) = trunk(x_prompt, latent=False, cache=None, s0=None)
    cache = (jnp.transpose(cache_k, (0, 1, 3, 4, 5, 2)).reshape(bs, GROUP_W, past),
             cache_v.reshape(bs, past * N_HEADS, HEAD_DIM))
    y_s, _, _ = trunk(x_sample, latent=True, cache=cache,
                      s0=state_rnn.reshape(bs, 2, N_HEADS, HEAD_DIM, HEAD_DIM))

    new_cache_k = k_raw.reshape(bp, DEPTH, lp, N_HEADS, 2, QK_DIM)
    new_cache_v = v_raw.reshape(bp, DEPTH, lp, N_HEADS, HEAD_DIM)
    new_state = s_new.reshape(bp, DEPTH, 2, N_HEADS, HEAD_DIM, HEAD_DIM)
    return (y_p, y_s, new_cache_k, new_cache_v, new_state)
```

```python
import functools
import math

import jax
import jax.numpy as jnp
import numpy as np
from jax import lax
from jax.experimental import pallas as pl
from jax.experimental.pallas import tpu as pltpu

D_MODEL = 1024
GRID_W = 64
N_HEADS = 4
HEAD_DIM = 128
QK_DIM = 64
GROUP_W = 512
N_GROUPS = 8
D_FF = 2816
N_MOD = 6
ROPE_BASE = 10000.0
DEPTH = 1
DEEPNORM_ALPHA = (2.0 * DEPTH) ** 0.25
NORM_EPS = 1e-5
LAM_INIT = 0.8 - 0.6 * math.exp(-0.3 * 0)
LOG2_E = math.log2(math.e)

V7X_VMEM_BYTES = 64 * 1024 * 1024
VMEM_LIMIT = V7X_VMEM_BYTES * 15 // 16

ROWS_PROJ = 512
ROWS_FFN = 1024
FF_TILE = 256
Q_ROWS = 256
HGRN_ROWS = 256
HGRN_LEVELS = int(math.log2(HGRN_ROWS))

F32 = jnp.float32
BF16 = jnp.bfloat16


def _params(semantics):
    return pltpu.CompilerParams(dimension_semantics=semantics, vmem_limit_bytes=VMEM_LIMIT)


def _dot(a, b):
    return jnp.dot(a, b, preferred_element_type=F32)


def _dot_nt(a, b):
    return lax.dot_general(a, b, (((1,), (1,)), ((), ())), preferred_element_type=F32)


def _dot_tn(a, b):
    return lax.dot_general(a, b, (((0,), (0,)), ((), ())), preferred_element_type=F32)


def _silu(x):
    return x * jax.nn.sigmoid(x)


def _layer_norm(y, g, b):
    mu = jnp.mean(y, axis=-1, keepdims=True)
    d = y - mu
    var = jnp.mean(d * d, axis=-1, keepdims=True)
    return d * lax.rsqrt(var + NORM_EPS) * g + b


def _mod_kernel(c_ref, w_ref, b_ref, o_ref):
    s = _silu(c_ref[...]).astype(BF16)
    o_ref[...] = _dot(s, w_ref[...].astype(BF16)) + b_ref[...]


def _mod_call(cond, w_ada, b_ada):
    n, d = cond.shape
    cols = w_ada.shape[1]
    tile = 512
    return pl.pallas_call(
        _mod_kernel,
        grid=(cols // tile,),
        in_specs=[pl.BlockSpec((n, d), lambda j: (0, 0)),
                  pl.BlockSpec((d, tile), lambda j: (0, j)),
                  pl.BlockSpec((1, tile), lambda j: (0, j))],
        out_specs=pl.BlockSpec((n, tile), lambda j: (0, j)),
        out_shape=jax.ShapeDtypeStruct((n, cols), F32),
        compiler_params=_params(("arbitrary",)),
        name="mod",
    )(cond, w_ada, b_ada)


def _rope(x, cos_ref, sin_a_ref, sin_b_ref):
    parts = []
    for c in range(GROUP_W // 128):
        sl = slice(c * 128, (c + 1) * 128)
        xc = x[:, sl]
        parts.append(xc * cos_ref[:, sl]
                     + pltpu.roll(xc, 128 - 16, axis=1) * sin_a_ref[:, sl]
                     + pltpu.roll(xc, 16, axis=1) * sin_b_ref[:, sl])
    return jnp.concatenate(parts, axis=1)


def _inproj_kernel(*refs, latent, n_casts):
    if latent:
        (x_ref, mod_ref, w_ref, cos_ref, sin_a_ref, sin_b_ref) = refs[:6]
        cast_in = refs[6:6 + n_casts]
        (q_ref, k_ref, v_ref, rq_ref, xf_ref, xb_ref, ri_ref, rg_ref) = refs[6 + n_casts:14 + n_casts]
        cast_out = refs[14 + n_casts:]
        for src, dst in zip(cast_in, cast_out):
            dst[...] = src[...].astype(BF16)
    else:
        (x_ref, mod_ref, w_ref,
         q_ref, k_ref, v_ref, rq_ref, xf_ref, xb_ref, ri_ref, rg_ref, kraw_ref, vraw_ref) = refs
    m = mod_ref[0]
    xm = (x_ref[...] * (1.0 + m[1:2, :]) + m[0:1, :]).astype(BF16)

    def proj(g):
        return _dot(xm, w_ref[:, g * GROUP_W:(g + 1) * GROUP_W])

    aq = proj(0)
    ak = proj(1)
    av = proj(2)
    if latent:
        aq = _rope(aq, cos_ref, sin_a_ref, sin_b_ref)
        ak = _rope(ak, cos_ref, sin_a_ref, sin_b_ref)
    else:
        rows = ak.shape[0]
        for j in range(2 * N_HEADS):
            kraw_ref[pl.ds(j, rows, stride=2 * N_HEADS), :] = ak[:, j * QK_DIM:(j + 1) * QK_DIM]
        for h in range(N_HEADS):
            vraw_ref[pl.ds(h, rows, stride=N_HEADS), :] = av[:, h * HEAD_DIM:(h + 1) * HEAD_DIM]
    q_ref[...] = (aq * (QK_DIM ** -0.5 * LOG2_E)).astype(BF16)
    k_ref[...] = ak.astype(BF16)
    v_ref[...] = av.astype(BF16)
    rq_ref[...] = proj(3)
    xf_ref[...] = proj(4)
    xb_ref[...] = proj(5)
    ri_ref[...] = proj(6).astype(BF16)
    rg_ref[...] = _silu(proj(7))


def _inproj_call(x2d, mod3, w_in, rope_tabs, *, latent, seq_len, casts=()):
    t = x2d.shape[0]
    rb = ROWS_PROJ
    blocks_per_seq = seq_len // rb if latent else 1

    def mod_idx(i):
        return ((1 + i // blocks_per_seq) if latent else 0, 0, 0)

    row_spec = lambda w: pl.BlockSpec((rb, w), lambda i: (i, 0))
    in_specs = [row_spec(D_MODEL),
                pl.BlockSpec((1, N_MOD, D_MODEL), mod_idx),
                pl.BlockSpec(w_in.shape, lambda i: (0, 0))]
    args = [x2d, mod3, w_in]
    n_steps = t // rb
    slab = lambda a: pl.BlockSpec((a.shape[0] // n_steps, a.shape[1]), lambda i: (i, 0))
    if latent:
        in_specs += [pl.BlockSpec((rb, GROUP_W), lambda i: (i % blocks_per_seq, 0))] * 3
        args += list(rope_tabs)
        in_specs += [slab(a) for a in casts]
        args += list(casts)
    dts = [BF16, BF16, BF16, F32, F32, F32, BF16, F32]
    out_specs = [row_spec(GROUP_W) for _ in dts]
    out_shape = [jax.ShapeDtypeStruct((t, GROUP_W), dt) for dt in dts]
    if latent:
        out_specs += [slab(a) for a in casts]
        out_shape += [jax.ShapeDtypeStruct(a.shape, BF16) for a in casts]
    else:
        out_specs += [pl.BlockSpec((rb * 2 * N_HEADS, QK_DIM), lambda i: (i, 0)),
                      pl.BlockSpec((rb * N_HEADS, HEAD_DIM), lambda i: (i, 0))]
        out_shape += [jax.ShapeDtypeStruct((t * 2 * N_HEADS, QK_DIM), F32),
                      jax.ShapeDtypeStruct((t * N_HEADS, HEAD_DIM), F32)]
    return pl.pallas_call(
        functools.partial(_inproj_kernel, latent=latent, n_casts=len(casts)),
        grid=(n_steps,),
        in_specs=in_specs,
        out_specs=out_specs,
        out_shape=out_shape,
        compiler_params=_params(("arbitrary",)),
        name="inproj_lat" if latent else "inproj_ctx",
    )(*args)


def _attn_kernel(*refs, has_cache):
    if has_cache:
        (q_ref, k_ref, v_ref, ck_ref, cv_ref, lq1, lk1, lq2, lk2, g_ref, o_ref) = refs
    else:
        (q_ref, k_ref, v_ref, lq1, lk1, lq2, lk2, g_ref, o_ref) = refs
    lam = (jnp.exp(jnp.sum(lq1[...] * lk1[...], axis=-1, keepdims=True))
           - jnp.exp(jnp.sum(lq2[...] * lk2[...], axis=-1, keepdims=True)) + LAM_INIT)
    qb = q_ref.shape[1]
    lane = lax.broadcasted_iota(jnp.int32, (qb, HEAD_DIM), 1)
    first_map = lane < QK_DIM
    zero = jnp.zeros((), BF16)

    def scores(h):
        sl = slice(h * HEAD_DIM, (h + 1) * HEAD_DIM)
        qh = q_ref[0, :, sl]
        qq = jnp.concatenate([jnp.where(first_map, qh, zero), jnp.where(first_map, zero, qh)], axis=0)
        s_n = _dot_nt(k_ref[0, :, sl], qq)
        s_c = None
        if has_cache:
            s_c = _dot_nt(ck_ref[0, sl, :].astype(BF16).T, qq)
        return s_n, s_c

    ahead = 3
    pending = [scores(h) for h in range(ahead)]
    for h in range(N_HEADS):
        sl = slice(h * HEAD_DIM, (h + 1) * HEAD_DIM)
        s_n, s_c = pending.pop(0)
        if h + ahead < N_HEADS:
            pending.append(scores(h + ahead))
        mx = jnp.max(s_n, axis=0, keepdims=True)
        if has_cache:
            mx = jnp.maximum(mx, jnp.max(s_c, axis=0, keepdims=True))
        e_n = jnp.exp2(s_n - mx)
        den = jnp.sum(e_n, axis=0, keepdims=True)
        ev = _dot_tn(v_ref[0, :, sl], e_n.astype(BF16))
        if has_cache:
            e_c = jnp.exp2(s_c - mx)
            den = den + jnp.sum(e_c, axis=0, keepdims=True)
            vc = cv_ref[0, pl.ds(h, e_c.shape[0], stride=N_HEADS), :]
            ev = ev + _dot_tn(vc.astype(BF16), e_c.astype(BF16))
        inv = 1.0 / den
        o = ev[:, :qb] * inv[:, :qb] - ev[:, qb:] * (inv[:, qb:] * lam)
        o = o * lax.rsqrt(jnp.mean(o * o, axis=0, keepdims=True) + NORM_EPS)
        o_ref[0, :, sl] = (o.T * g_ref[:, sl] * (1.0 - LAM_INIT)).astype(BF16)


def _attn_call(q, k, v, cache, lams, att_g):
    b, l, _ = q.shape
    qb = Q_ROWS
    full = lambda a: pl.BlockSpec((1,) + a.shape[1:], lambda i, j: (i,) + (0,) * (a.ndim - 1))
    in_specs = [pl.BlockSpec((1, qb, GROUP_W), lambda i, j: (i, j, 0)), full(k), full(v)]
    args = [q, k, v]
    if cache is not None:
        in_specs += [full(cache[0]), full(cache[1])]
        args += list(cache)
    in_specs += [pl.BlockSpec((1, QK_DIM), lambda i, j: (0, 0))] * 4
    in_specs += [pl.BlockSpec((1, GROUP_W), lambda i, j: (0, 0))]
    args += list(lams) + [att_g]
    return pl.pallas_call(
        functools.partial(_attn_kernel, has_cache=cache is not None),
        grid=(b, l // qb),
        in_specs=in_specs,
        out_specs=pl.BlockSpec((1, qb, GROUP_W), lambda i, j: (i, j, 0)),
        out_shape=jax.ShapeDtypeStruct((b, l, GROUP_W), BF16),
        compiler_params=_params(("arbitrary", "arbitrary")),
        name="attn_lat" if cache is not None else "attn_ctx",
    )(*args)


HGRN_DIAG = 128
HGRN_MASKED_LEVELS = 5
HGRN_VREG_LEVELS = 3


def _hgrn_masks():
    r = np.arange(HGRN_DIAG)[:, None]
    c = np.arange(HGRN_DIAG)[None, :]
    levels = []
    for lv in range(HGRN_MASKED_LEVELS):
        same = (r >> (lv + 1)) == (c >> (lv + 1))
        levels.append(same & (((r >> lv) & 1) == 1) & (((c >> lv) & 1) == 0))
    fwd = [r == c] + levels
    bwd = [m.T for m in levels]
    return np.stack(fwd + bwd).astype(np.float32)


def _hgrn_block(q, k, g, v, st, mask_ref, sub, *, reverse):
    n = HGRN_ROWS
    n_diag = n // HGRN_DIAG

    def diag_scores(qe, ke, m):
        return [_dot_nt(qe[i * HGRN_DIAG:(i + 1) * HGRN_DIAG], ke[i * HGRN_DIAG:(i + 1) * HGRN_DIAG]) * m
                for i in range(n_diag)]

    def level_mask(lv):
        return mask_ref[(1 + HGRN_MASKED_LEVELS + lv) if reverse else (1 + lv)]

    g3 = g.reshape(n // 8, 8, HEAD_DIM)
    zero3 = jnp.zeros_like(g3)
    pre3, suf3 = (zero3, g3) if reverse else (g3, zero3)
    tot3 = g3
    qb, kb = q.astype(BF16), k.astype(BF16)
    acc = diag_scores(qb, kb, mask_ref[0])
    for lv in range(HGRN_VREG_LEVELS):
        b = 1 << lv
        upper = ((sub >> lv) & 1) == 1
        e = jnp.exp2(jnp.where(upper, pre3, suf3)).reshape(n, HEAD_DIM)
        eb = e.astype(BF16)
        part = diag_scores(qb * eb, kb * eb, level_mask(lv))
        acc = [a + p for a, p in zip(acc, part)]
        sib = jnp.where(upper, pltpu.roll(tot3, b, axis=1), pltpu.roll(tot3, 8 - b, axis=1))
        pre3 = pre3 + jnp.where(upper, sib, 0.0)
        suf3 = suf3 + jnp.where(upper, 0.0, sib)
        tot3 = tot3 + sib

    pieces = lambda x3: [x3[i] for i in range(n // 8)]
    pre8, suf8, tot8 = pieces(pre3), pieces(suf3), pieces(tot3)
    rows = lambda xs, lo, hi: jnp.concatenate(xs[lo // 8:hi // 8], axis=0) if hi - lo > 8 else xs[lo // 8]
    big_scores = {}
    for lv in range(HGRN_VREG_LEVELS, HGRN_LEVELS):
        b = 1 << lv
        pb = b // 8
        if lv < HGRN_MASKED_LEVELS:
            u = jnp.concatenate([(pre8 if (i // pb) % 2 else suf8)[i] for i in range(n // 8)], axis=0)
            e = jnp.exp2(u)
            eb = e.astype(BF16)
            part = diag_scores(qb * eb, kb * eb, level_mask(lv))
            acc = [a + p for a, p in zip(acc, part)]
        else:
            for j in range(n // (2 * b)):
                lo, mid, hi = 2 * b * j, 2 * b * j + b, 2 * b * (j + 1)
                e_lo = jnp.exp2(rows(suf8, lo, mid))
                e_up = jnp.exp2(rows(pre8, mid, hi))
                x_lo = (qb if reverse else kb)[lo:mid] * e_lo.astype(BF16)
                x_up = (kb if reverse else qb)[mid:hi] * e_up.astype(BF16)
                big_scores[lv, j] = _dot_nt(x_lo, x_up) if reverse else _dot_nt(x_up, x_lo)
        for j in range(n // (2 * b)):
            lo_p, mid_p, hi_p = 2 * pb * j, 2 * pb * j + pb, 2 * pb * (j + 1)
            t_lo, t_up = tot8[lo_p], tot8[mid_p]
            t_new = t_lo + t_up
            for i in range(lo_p, mid_p):
                suf8[i] = suf8[i] + t_up
                tot8[i] = t_new
            for i in range(mid_p, hi_p):
                pre8[i] = pre8[i] + t_lo
                tot8[i] = t_new
    pre = jnp.concatenate(pre8, axis=0)
    suf = jnp.concatenate(suf8, axis=0)
    q_dec, k_dec = (suf, pre) if reverse else (pre, suf)
    q_in = qb * jnp.exp2(q_dec).astype(BF16)
    k_out = kb * jnp.exp2(k_dec).astype(BF16)
    decay = jnp.exp2(tot8[0][0:1, :])

    o_inter = _dot_nt(q_in, st.astype(BF16))
    ds = _dot_tn(v, k_out)
    o_diag = [_dot(acc[i].astype(BF16), v[i * HGRN_DIAG:(i + 1) * HGRN_DIAG]) for i in range(n_diag)]
    o_big = {}
    for (lv, j), a in big_scores.items():
        lo, mid, hi = (2 * j) << lv, (2 * j + 1) << lv, (2 * j + 2) << lv
        o_big[lv, j] = _dot(a.astype(BF16), v[mid:hi] if reverse else v[lo:mid])
    st_new = st * decay + ds
    o = jnp.concatenate(o_diag, axis=0) + o_inter
    for lv in range(HGRN_MASKED_LEVELS, HGRN_LEVELS):
        contrib = []
        for j in range(n >> (lv + 1)):
            zeros = jnp.zeros((1 << lv, HEAD_DIM), F32)
            contrib += [o_big[lv, j], zeros] if reverse else [zeros, o_big[lv, j]]
        o = o + jnp.concatenate(contrib, axis=0)
    return o, st_new


def _hgrn_kernel(*refs, n_blocks, heads, has_state):
    if has_state:
        (q_ref, v_ref, xf_ref, xb_ref, gate_ref, lbf_ref, lbb_ref, g_ref, mask_ref, s0_ref,
         o_ref, acc_ref) = refs
    else:
        (q_ref, v_ref, xf_ref, xb_ref, gate_ref, lbf_ref, lbb_ref, g_ref, mask_ref,
         o_ref, sout_ref, acc_ref) = refs
    n = HGRN_ROWS
    sub = lax.broadcasted_iota(jnp.int32, (n // 8, 8, HEAD_DIM), 1)

    def lower_bound(ref, lanes):
        l0, l1 = ref[0:1, lanes], ref[1:2, lanes]
        mx = jnp.maximum(l0, l1)
        e0, e1 = jnp.exp(l0 - mx), jnp.exp(l1 - mx)
        return e0 / (e0 + e1)

    x_refs = (xf_ref, xb_ref)
    for h in range(heads):
        lanes = slice(h * HEAD_DIM, (h + 1) * HEAD_DIM)
        lbs = [lower_bound(lbf_ref, lanes), lower_bound(lbb_ref, lanes)]
        sts = [s0_ref[0, d, h].T if has_state else jnp.zeros((HEAD_DIM, HEAD_DIM), F32) for d in range(2)]
        for j in range(n_blocks):
            for d in range(2):
                blk = (n_blocks - 1 - j) if d else j
                rows = slice(blk * n, (blk + 1) * n)
                f = lbs[d] + (1.0 - lbs[d]) * jax.nn.sigmoid(x_refs[d][0, rows, lanes])
                o, sts[d] = _hgrn_block(q_ref[0, rows, lanes], 1.0 - f, jnp.log2(f), v_ref[0, rows, lanes],
                                        sts[d], mask_ref, sub, reverse=bool(d))
                acc_ref[d, rows, lanes] = o
        if not has_state:
            for d in range(2):
                sout_ref[0, d, h] = sts[d].T
    for h in range(heads):
        lanes = slice(h * HEAD_DIM, (h + 1) * HEAD_DIM)
        o = acc_ref[0, :, lanes] + acc_ref[1, :, lanes]
        o = o * lax.rsqrt(jnp.mean(o * o, axis=-1, keepdims=True) + NORM_EPS)
        o_ref[0, :, lanes] = (o * g_ref[:, lanes] * gate_ref[0, :, lanes]).astype(BF16)


def _hgrn_call(rq, ri, xf, xb, gate, lbf, lbb, rnn_g, masks, s0):
    b, l, _ = rq.shape
    n_blocks = l // HGRN_ROWS
    heads = max(1, min(N_HEADS, 4 // n_blocks))
    w = heads * HEAD_DIM
    head = lambda: pl.BlockSpec((1, l, w), lambda i, h: (i, 0, h))
    per_head = lambda rows: pl.BlockSpec((rows, w), lambda i, h: (0, h))
    state_spec = pl.BlockSpec((1, 2, heads, HEAD_DIM, HEAD_DIM), lambda i, h: (i, 0, h, 0, 0))
    in_specs = [head(), head(), head(), head(), head(), per_head(2), per_head(2), per_head(1),
                pl.BlockSpec(masks.shape, lambda i, h: (0, 0, 0))]
    args = [rq, ri, xf, xb, gate, lbf, lbb, rnn_g, masks]
    out_specs = [head()]
    out_shape = [jax.ShapeDtypeStruct((b, l, GROUP_W), BF16)]
    if s0 is not None:
        in_specs.append(state_spec)
        args.append(s0)
    else:
        out_specs.append(state_spec)
        out_shape.append(jax.ShapeDtypeStruct((b, 2, N_HEADS, HEAD_DIM, HEAD_DIM), F32))
    return pl.pallas_call(
        functools.partial(_hgrn_kernel, n_blocks=n_blocks, heads=heads, has_state=s0 is not None),
        grid=(b, N_HEADS // heads),
        in_specs=in_specs,
        out_specs=out_specs,
        out_shape=out_shape,
        scratch_shapes=[pltpu.VMEM((2, l, w), F32)],
        compiler_params=_params(("arbitrary", "arbitrary")),
        name="hgrn_lat" if s0 is not None else "hgrn_ctx",
    )(*args)


def _tail_kernel(att_ref, rnn_ref, x_ref, mod_ref, wo_ref, g1_ref, b1_ref, wu_ref, cw_ref, cb_ref, wd_ref,
                 g2_ref, b2_ref, o_ref, x1_ref, xm2_ref, hid_ref, *, seq_len):
    rows = x_ref.shape[0]
    m = mod_ref[0]
    mix = _dot(att_ref[...], wo_ref[0:GROUP_W, :]) + _dot(rnn_ref[...], wo_ref[GROUP_W:2 * GROUP_W, :])
    x1 = _layer_norm(DEEPNORM_ALPHA * x_ref[...] + m[2:3, :] * mix, g1_ref[...], b1_ref[...])
    x1_ref[...] = x1
    xm2_ref[...] = (x1 * (1.0 + m[4:5, :]) + m[3:4, :]).astype(BF16)

    sub = lax.broadcasted_iota(jnp.int32, (8, FF_TILE), 0)

    def shifted(h, shift, edge_row):
        r = pltpu.roll(h, shift, axis=0)
        parts = []
        for s in range(rows // seq_len):
            edge = s * seq_len + (edge_row // 8) * 8
            fixed = jnp.where(sub == edge_row % 8, 0.0, r[edge:edge + 8])
            parts += [r[s * seq_len:edge], fixed, r[edge + 8:(s + 1) * seq_len]]
        return jnp.concatenate([p for p in parts if p.shape[0]], axis=0)

    def conv(h, cols):
        prev = shifted(h, 1, 0)
        nxt = shifted(h, rows - 1, seq_len - 1)
        return prev * cw_ref[0:1, cols] + h * cw_ref[1:2, cols] + nxt * cw_ref[2:3, cols] + cb_ref[:, cols]

    for j in range(D_FF // FF_TILE):
        cols_a = slice(j * FF_TILE, (j + 1) * FF_TILE)
        cols_u = slice(D_FF + j * FF_TILE, D_FF + (j + 1) * FF_TILE)
        a = conv(_dot(xm2_ref[...], wu_ref[:, cols_a]), cols_a)
        u = conv(_dot(xm2_ref[...], wu_ref[:, cols_u]), cols_u)
        hid_ref[:, cols_a] = (_silu(a) * u).astype(BF16)
    y = DEEPNORM_ALPHA * x1_ref[...] + m[5:6, :] * _dot(hid_ref[...], wd_ref[...])
    o_ref[...] = _layer_norm(y, g2_ref[...], b2_ref[...])


def _tail_call(att, rnn, x2d, mod3, w_out, ln1_g, ln1_b, w_up, conv_w, conv_b, w_down, ln2_g, ln2_b,
               *, latent, seq_len):
    t = x2d.shape[0]
    rb = ROWS_FFN
    assert rb % seq_len == 0

    def mod_idx(i):
        return ((1 + i * rb // seq_len) if latent else 0, 0, 0)

    row_spec = lambda w: pl.BlockSpec((rb, w), lambda i: (i, 0))
    resident = lambda a: pl.BlockSpec(a.shape, lambda i: (0, 0), pipeline_mode=pl.Buffered(1))
    return pl.pallas_call(
        functools.partial(_tail_kernel, seq_len=seq_len),
        grid=(t // rb,),
        in_specs=[row_spec(GROUP_W), row_spec(GROUP_W), row_spec(D_MODEL),
                  pl.BlockSpec((1, N_MOD, D_MODEL), mod_idx),
                  resident(w_out), resident(ln1_g), resident(ln1_b),
                  resident(w_up), resident(conv_w), resident(conv_b), resident(w_down),
                  resident(ln2_g), resident(ln2_b)],
        out_specs=row_spec(D_MODEL),
        out_shape=jax.ShapeDtypeStruct((t, D_MODEL), F32),
        scratch_shapes=[pltpu.VMEM((rb, D_MODEL), F32), pltpu.VMEM((rb, D_MODEL), BF16),
                        pltpu.VMEM((rb, D_FF), BF16)],
        compiler_params=_params(("arbitrary",)),
        name="tail_lat" if latent else "tail_ctx",
    )(att, rnn, x2d, mod3, w_out, ln1_g, ln1_b, w_up, conv_w, conv_b, w_down, ln2_g, ln2_b)


def _rope_tables(seq_len):
    quarter = QK_DIM // 4
    freqs = 1.0 / (ROPE_BASE ** (np.arange(quarter, dtype=np.float64) / quarter))
    t = np.arange(seq_len)
    ang_r = (t // GRID_W)[:, None] * freqs
    ang_c = (t % GRID_W)[:, None] * freqs
    zeros = np.zeros_like(ang_r)

    def tile(parts):
        return jnp.asarray(np.tile(np.concatenate(parts, axis=-1), (1, GROUP_W // QK_DIM)).astype(np.float32))

    cos = tile([np.cos(ang_r), np.cos(ang_r), np.cos(ang_c), np.cos(ang_c)])
    sin_a = tile([-np.sin(ang_r), zeros, -np.sin(ang_c), zeros])
    sin_b = tile([zeros, np.sin(ang_r), zeros, np.sin(ang_c)])
    return cos, sin_a, sin_b


def kernel(x_prompt, x_sample, cache_k, cache_v, state_rnn, c, c_ctx, w_ada, b_ada, w_in, lambda_q1, lambda_k1, lambda_q2, lambda_k2, lb_fwd_logits, lb_bwd_logits, att_norm_g, rnn_norm_g, w_out, ln1_g, ln1_b, w_up, conv_w, conv_b, w_down, ln2_g, ln2_b):
    assert w_ada.shape[0] == DEPTH
    bp, lp, d = x_prompt.shape
    bs, ls, _ = x_sample.shape
    past = cache_k.shape[2]

    cond = jnp.zeros((16, d), F32).at[0].set(c_ctx).at[1:1 + bs].set(c)
    mod3 = _mod_call(cond, w_ada[0], b_ada).reshape(16, N_MOD, d)

    w_in_b = w_in[0].astype(BF16)
    lams = (lambda_q1, lambda_k1, lambda_q2, lambda_k2)
    masks = jnp.asarray(_hgrn_masks())
    xp2d = x_prompt.reshape(bp * lp, d)
    xs2d = x_sample.reshape(bs * ls, d)

    ctx_in = _inproj_call(xp2d, mod3, w_in_b, None, latent=False, seq_len=lp)
    lat_in = _inproj_call(xs2d, mod3, w_in_b, _rope_tables(ls), latent=True, seq_len=ls,
                          casts=(w_out[0], w_up[0], w_down[0]))
    k_raw, v_raw = ctx_in[8:]
    w_out_b, w_up_b, w_down_b = lat_in[8:]

    def mix_and_tail(x2d, proj, b, l, *, latent, cache, s0):
        q, k, v, rq, xf, xb, ri, rg = (o.reshape(b, l, GROUP_W) for o in proj[:8])
        att = _attn_call(q, k, v, cache, lams, att_norm_g)
        hg = _hgrn_call(rq, ri, xf, xb, rg, lb_fwd_logits, lb_bwd_logits, rnn_norm_g, masks, s0)
        y = _tail_call(att.reshape(b * l, GROUP_W), hg[0].reshape(b * l, GROUP_W), x2d, mod3, w_out_b, ln1_g,
                       ln1_b, w_up_b, conv_w[0], conv_b, w_down_b, ln2_g, ln2_b, latent=latent, seq_len=l)
        return y.reshape(b, l, d), hg[1:]

    y_p, (s_new,) = mix_and_tail(xp2d, ctx_in, bp, lp, latent=False, cache=None, s0=None)
    cache = (jnp.transpose(cache_k, (0, 1, 3, 4, 5, 2)).reshape(bs, GROUP_W, past),
             cache_v.reshape(bs, past * N_HEADS, HEAD_DIM))
    y_s, _ = mix_and_tail(xs2d, lat_in, bs, ls, latent=True, cache=cache,
                          s0=state_rnn.reshape(bs, 2, N_HEADS, HEAD_DIM, HEAD_DIM))

    new_cache_k = k_raw.reshape(bp, DEPTH, lp, N_HEADS, 2, QK_DIM)
    new_cache_v = v_raw.reshape(bp, DEPTH, lp, N_HEADS, HEAD_DIM)
    new_state = s_new.reshape(bp, DEPTH, 2, N_HEADS, HEAD_DIM, HEAD_DIM)
    return (y_p, y_s, new_cache_k, new_cache_v, new_state)
```

```python
import functools
import math

import jax
import jax.numpy as jnp
import numpy as np
from jax import lax
from jax.experimental import pallas as pl
from jax.experimental.pallas import tpu as pltpu

D_MODEL = 1024
GRID_W = 64
N_HEADS = 4
HEAD_DIM = 128
QK_DIM = 64
GROUP_W = 512
N_GROUPS = 8
D_FF = 2816
N_MOD = 6
ROPE_BASE = 10000.0
DEPTH = 1
DEEPNORM_ALPHA = (2.0 * DEPTH) ** 0.25
NORM_EPS = 1e-5
LAM_INIT = 0.8 - 0.6 * math.exp(-0.3 * 0)
LOG2_E = math.log2(math.e)

V7X_VMEM_BYTES = 64 * 1024 * 1024
VMEM_LIMIT = V7X_VMEM_BYTES * 15 // 16

ROWS_PROJ = 512
ROWS_FFN = 1024
FF_TILE = 256
NORM_ROWS = 256
Q_ROWS = 256
HGRN_ROWS = 256
HGRN_LEVELS = int(math.log2(HGRN_ROWS))

F32 = jnp.float32
BF16 = jnp.bfloat16


def _params(semantics):
    return pltpu.CompilerParams(dimension_semantics=semantics, vmem_limit_bytes=VMEM_LIMIT)


def _dot(a, b):
    return jnp.dot(a, b, preferred_element_type=F32)


def _dot_nt(a, b):
    return lax.dot_general(a, b, (((1,), (1,)), ((), ())), preferred_element_type=F32)


def _dot_tn(a, b):
    return lax.dot_general(a, b, (((0,), (0,)), ((), ())), preferred_element_type=F32)


def _silu(x):
    return x * jax.nn.sigmoid(x)


def _layer_norm(y, g, b):
    mu = jnp.mean(y, axis=-1, keepdims=True)
    d = y - mu
    var = jnp.mean(d * d, axis=-1, keepdims=True)
    return d * lax.rsqrt(var + NORM_EPS) * g + b


def _mod_kernel(c_ref, w_ref, b_ref, o_ref):
    s = _silu(c_ref[...]).astype(BF16)
    o_ref[...] = _dot(s, w_ref[...].astype(BF16)) + b_ref[...]


def _mod_call(cond, w_ada, b_ada):
    n, d = cond.shape
    cols = w_ada.shape[1]
    tile = 512
    return pl.pallas_call(
        _mod_kernel,
        grid=(cols // tile,),
        in_specs=[pl.BlockSpec((n, d), lambda j: (0, 0)),
                  pl.BlockSpec((d, tile), lambda j: (0, j)),
                  pl.BlockSpec((1, tile), lambda j: (0, j))],
        out_specs=pl.BlockSpec((n, tile), lambda j: (0, j)),
        out_shape=jax.ShapeDtypeStruct((n, cols), F32),
        compiler_params=_params(("arbitrary",)),
        name="mod",
    )(cond, w_ada, b_ada)


def _rope(x, cos_ref, sin_a_ref, sin_b_ref):
    parts = []
    for c in range(GROUP_W // 128):
        sl = slice(c * 128, (c + 1) * 128)
        xc = x[:, sl]
        parts.append(xc * cos_ref[:, sl]
                     + pltpu.roll(xc, 128 - 16, axis=1) * sin_a_ref[:, sl]
                     + pltpu.roll(xc, 16, axis=1) * sin_b_ref[:, sl])
    return jnp.concatenate(parts, axis=1)


def _inproj_kernel(*refs, latent, n_casts):
    if latent:
        (x_ref, mod_ref, w_ref, lbf_ref, lbb_ref, cos_ref, sin_a_ref, sin_b_ref) = refs[:8]
        cast_in = refs[8:8 + n_casts]
        (q_ref, k_ref, v_ref, rq_ref, gf_ref, gb_ref, ri_ref, rg_ref) = refs[8 + n_casts:16 + n_casts]
        cast_out = refs[16 + n_casts:]
        for src, dst in zip(cast_in, cast_out):
            dst[...] = src[...].astype(BF16)
    else:
        (x_ref, mod_ref, w_ref, lbf_ref, lbb_ref,
         q_ref, k_ref, v_ref, rq_ref, gf_ref, gb_ref, ri_ref, rg_ref, kraw_ref, vraw_ref) = refs
    m = mod_ref[0]
    xm = (x_ref[...] * (1.0 + m[1:2, :]) + m[0:1, :]).astype(BF16)

    def proj(g):
        return _dot(xm, w_ref[:, g * GROUP_W:(g + 1) * GROUP_W])

    aq = proj(0)
    ak = proj(1)
    av = proj(2)
    if latent:
        aq = _rope(aq, cos_ref, sin_a_ref, sin_b_ref)
        ak = _rope(ak, cos_ref, sin_a_ref, sin_b_ref)
    else:
        rows = ak.shape[0]
        for j in range(2 * N_HEADS):
            kraw_ref[pl.ds(j, rows, stride=2 * N_HEADS), :] = ak[:, j * QK_DIM:(j + 1) * QK_DIM]
        for h in range(N_HEADS):
            vraw_ref[pl.ds(h, rows, stride=N_HEADS), :] = av[:, h * HEAD_DIM:(h + 1) * HEAD_DIM]
    q_ref[...] = (aq * (QK_DIM ** -0.5 * LOG2_E)).astype(BF16)
    k_ref[...] = ak.astype(BF16)
    v_ref[...] = av.astype(BF16)
    rq_ref[...] = proj(3)

    def log2_forget(x, lb_ref):
        l0, l1 = lb_ref[0:1, :], lb_ref[1:2, :]
        mx = jnp.maximum(l0, l1)
        e0, e1 = jnp.exp(l0 - mx), jnp.exp(l1 - mx)
        lb = e0 / (e0 + e1)
        return jnp.log2(lb + (1.0 - lb) * jax.nn.sigmoid(x))

    gf_ref[...] = log2_forget(proj(4), lbf_ref)
    gb_ref[...] = log2_forget(proj(5), lbb_ref)
    ri_ref[...] = proj(6).astype(BF16)
    rg_ref[...] = _silu(proj(7))


def _inproj_call(x2d, mod3, w_in, lbf, lbb, rope_tabs, *, latent, seq_len, casts=()):
    t = x2d.shape[0]
    rb = ROWS_PROJ
    blocks_per_seq = seq_len // rb if latent else 1

    def mod_idx(i):
        return ((1 + i // blocks_per_seq) if latent else 0, 0, 0)

    row_spec = lambda w: pl.BlockSpec((rb, w), lambda i: (i, 0))
    in_specs = [row_spec(D_MODEL),
                pl.BlockSpec((1, N_MOD, D_MODEL), mod_idx),
                pl.BlockSpec(w_in.shape, lambda i: (0, 0)),
                pl.BlockSpec(lbf.shape, lambda i: (0, 0)), pl.BlockSpec(lbb.shape, lambda i: (0, 0))]
    args = [x2d, mod3, w_in, lbf, lbb]
    n_steps = t // rb
    slab = lambda a: pl.BlockSpec((a.shape[0] // n_steps, a.shape[1]), lambda i: (i, 0))
    if latent:
        in_specs += [pl.BlockSpec((rb, GROUP_W), lambda i: (i % blocks_per_seq, 0))] * 3
        args += list(rope_tabs)
        in_specs += [slab(a) for a in casts]
        args += list(casts)
    dts = [BF16, BF16, BF16, F32, F32, F32, BF16, F32]
    out_specs = [row_spec(GROUP_W) for _ in dts]
    out_shape = [jax.ShapeDtypeStruct((t, GROUP_W), dt) for dt in dts]
    if latent:
        out_specs += [slab(a) for a in casts]
        out_shape += [jax.ShapeDtypeStruct(a.shape, BF16) for a in casts]
    else:
        out_specs += [pl.BlockSpec((rb * 2 * N_HEADS, QK_DIM), lambda i: (i, 0)),
                      pl.BlockSpec((rb * N_HEADS, HEAD_DIM), lambda i: (i, 0))]
        out_shape += [jax.ShapeDtypeStruct((t * 2 * N_HEADS, QK_DIM), F32),
                      jax.ShapeDtypeStruct((t * N_HEADS, HEAD_DIM), F32)]
    return pl.pallas_call(
        functools.partial(_inproj_kernel, latent=latent, n_casts=len(casts)),
        grid=(n_steps,),
        in_specs=in_specs,
        out_specs=out_specs,
        out_shape=out_shape,
        compiler_params=_params(("arbitrary",)),
        name="inproj_lat" if latent else "inproj_ctx",
    )(*args)


def _attn_kernel(*refs, has_cache):
    if has_cache:
        (q_ref, k_ref, v_ref, ck_ref, cv_ref, lq1, lk1, lq2, lk2, g_ref, o_ref) = refs
    else:
        (q_ref, k_ref, v_ref, lq1, lk1, lq2, lk2, g_ref, o_ref) = refs
    lam = (jnp.exp(jnp.sum(lq1[...] * lk1[...], axis=-1, keepdims=True))
           - jnp.exp(jnp.sum(lq2[...] * lk2[...], axis=-1, keepdims=True)) + LAM_INIT)
    qb = q_ref.shape[1]
    lane = lax.broadcasted_iota(jnp.int32, (qb, HEAD_DIM), 1)
    first_map = lane < QK_DIM
    zero = jnp.zeros((), BF16)

    def scores(h):
        sl = slice(h * HEAD_DIM, (h + 1) * HEAD_DIM)
        qh = q_ref[0, :, sl]
        qq = jnp.concatenate([jnp.where(first_map, qh, zero), jnp.where(first_map, zero, qh)], axis=0)
        s_n = _dot_nt(k_ref[0, :, sl], qq)
        s_c = None
        if has_cache:
            s_c = _dot_nt(ck_ref[0, sl, :].astype(BF16).T, qq)
        return s_n, s_c

    ahead = 3
    pending = [scores(h) for h in range(ahead)]
    for h in range(N_HEADS):
        sl = slice(h * HEAD_DIM, (h + 1) * HEAD_DIM)
        s_n, s_c = pending.pop(0)
        if h + ahead < N_HEADS:
            pending.append(scores(h + ahead))
        mx = jnp.max(s_n, axis=0, keepdims=True)
        if has_cache:
            mx = jnp.maximum(mx, jnp.max(s_c, axis=0, keepdims=True))
        e_n = jnp.exp2(s_n - mx)
        den = jnp.sum(e_n, axis=0, keepdims=True)
        ev = _dot_tn(v_ref[0, :, sl], e_n.astype(BF16))
        if has_cache:
            e_c = jnp.exp2(s_c - mx)
            den = den + jnp.sum(e_c, axis=0, keepdims=True)
            vc = cv_ref[0, pl.ds(h, e_c.shape[0], stride=N_HEADS), :]
            ev = ev + _dot_tn(vc.astype(BF16), e_c.astype(BF16))
        inv = 1.0 / den
        o = ev[:, :qb] * inv[:, :qb] - ev[:, qb:] * (inv[:, qb:] * lam)
        o = o * lax.rsqrt(jnp.mean(o * o, axis=0, keepdims=True) + NORM_EPS)
        o_ref[0, :, sl] = (o.T * g_ref[:, sl] * (1.0 - LAM_INIT)).astype(BF16)


def _attn_call(q, k, v, cache, lams, att_g):
    b, l, _ = q.shape
    qb = Q_ROWS
    full = lambda a: pl.BlockSpec((1,) + a.shape[1:], lambda i, j: (i,) + (0,) * (a.ndim - 1))
    in_specs = [pl.BlockSpec((1, qb, GROUP_W), lambda i, j: (i, j, 0)), full(k), full(v)]
    args = [q, k, v]
    if cache is not None:
        in_specs += [full(cache[0]), full(cache[1])]
        args += list(cache)
    in_specs += [pl.BlockSpec((1, QK_DIM), lambda i, j: (0, 0))] * 4
    in_specs += [pl.BlockSpec((1, GROUP_W), lambda i, j: (0, 0))]
    args += list(lams) + [att_g]
    return pl.pallas_call(
        functools.partial(_attn_kernel, has_cache=cache is not None),
        grid=(b, l // qb),
        in_specs=in_specs,
        out_specs=pl.BlockSpec((1, qb, GROUP_W), lambda i, j: (i, j, 0)),
        out_shape=jax.ShapeDtypeStruct((b, l, GROUP_W), BF16),
        compiler_params=_params(("arbitrary", "arbitrary")),
        name="attn_lat" if cache is not None else "attn_ctx",
    )(*args)


HGRN_DIAG = 128
HGRN_MASKED_LEVELS = 5
HGRN_VREG_LEVELS = 3


def _hgrn_masks():
    r = np.arange(HGRN_DIAG)[:, None]
    c = np.arange(HGRN_DIAG)[None, :]
    levels = []
    for lv in range(HGRN_MASKED_LEVELS):
        same = (r >> (lv + 1)) == (c >> (lv + 1))
        levels.append(same & (((r >> lv) & 1) == 1) & (((c >> lv) & 1) == 0))
    fwd = [r == c] + levels
    bwd = [m.T for m in levels]
    return np.stack(fwd + bwd).astype(np.float32)


def _hgrn_block(q, k, g, v, st, mask_ref, sub, *, reverse):
    n = HGRN_ROWS
    n_diag = n // HGRN_DIAG

    def diag_scores(qe, ke, m):
        return [_dot_nt(qe[i * HGRN_DIAG:(i + 1) * HGRN_DIAG], ke[i * HGRN_DIAG:(i + 1) * HGRN_DIAG]) * m
                for i in range(n_diag)]

    def level_mask(lv):
        return mask_ref[(1 + HGRN_MASKED_LEVELS + lv) if reverse else (1 + lv)]

    g3 = g.reshape(n // 8, 8, HEAD_DIM)
    zero3 = jnp.zeros_like(g3)
    pre3, suf3 = (zero3, g3) if reverse else (g3, zero3)
    tot3 = g3
    qb, kb = q.astype(BF16), k.astype(BF16)
    acc = diag_scores(qb, kb, mask_ref[0])
    for lv in range(HGRN_VREG_LEVELS):
        b = 1 << lv
        upper = ((sub >> lv) & 1) == 1
        e = jnp.exp2(jnp.where(upper, pre3, suf3)).reshape(n, HEAD_DIM)
        eb = e.astype(BF16)
        part = diag_scores(qb * eb, kb * eb, level_mask(lv))
        acc = [a + p for a, p in zip(acc, part)]
        sib = jnp.where(upper, pltpu.roll(tot3, b, axis=1), pltpu.roll(tot3, 8 - b, axis=1))
        pre3 = pre3 + jnp.where(upper, sib, 0.0)
        suf3 = suf3 + jnp.where(upper, 0.0, sib)
        tot3 = tot3 + sib

    pieces = lambda x3: [x3[i] for i in range(n // 8)]
    pre8, suf8, tot8 = pieces(pre3), pieces(suf3), pieces(tot3)
    rows = lambda xs, lo, hi: jnp.concatenate(xs[lo // 8:hi // 8], axis=0) if hi - lo > 8 else xs[lo // 8]
    big_scores = {}
    for lv in range(HGRN_VREG_LEVELS, HGRN_LEVELS):
        b = 1 << lv
        pb = b // 8
        if lv < HGRN_MASKED_LEVELS:
            u = jnp.concatenate([(pre8 if (i // pb) % 2 else suf8)[i] for i in range(n // 8)], axis=0)
            e = jnp.exp2(u)
            eb = e.astype(BF16)
            part = diag_scores(qb * eb, kb * eb, level_mask(lv))
            acc = [a + p for a, p in zip(acc, part)]
        else:
            for j in range(n // (2 * b)):
                lo, mid, hi = 2 * b * j, 2 * b * j + b, 2 * b * (j + 1)
                e_lo = jnp.exp2(rows(suf8, lo, mid))
                e_up = jnp.exp2(rows(pre8, mid, hi))
                x_lo = (qb if reverse else kb)[lo:mid] * e_lo.astype(BF16)
                x_up = (kb if reverse else qb)[mid:hi] * e_up.astype(BF16)
                big_scores[lv, j] = _dot_nt(x_lo, x_up) if reverse else _dot_nt(x_up, x_lo)
        for j in range(n // (2 * b)):
            lo_p, mid_p, hi_p = 2 * pb * j, 2 * pb * j + pb, 2 * pb * (j + 1)
            t_lo, t_up = tot8[lo_p], tot8[mid_p]
            t_new = t_lo + t_up
            for i in range(lo_p, mid_p):
                suf8[i] = suf8[i] + t_up
                tot8[i] = t_new
            for i in range(mid_p, hi_p):
                pre8[i] = pre8[i] + t_lo
                tot8[i] = t_new
    pre = jnp.concatenate(pre8, axis=0)
    suf = jnp.concatenate(suf8, axis=0)
    q_dec, k_dec = (suf, pre) if reverse else (pre, suf)
    q_in = qb * jnp.exp2(q_dec).astype(BF16)
    k_out = kb * jnp.exp2(k_dec).astype(BF16)
    decay = jnp.exp2(tot8[0][0:1, :])

    o_inter = _dot_nt(q_in, st.astype(BF16))
    ds = _dot_tn(v, k_out)
    o_diag = [_dot(acc[i].astype(BF16), v[i * HGRN_DIAG:(i + 1) * HGRN_DIAG]) for i in range(n_diag)]
    o_big = {}
    for (lv, j), a in big_scores.items():
        lo, mid, hi = (2 * j) << lv, (2 * j + 1) << lv, (2 * j + 2) << lv
        o_big[lv, j] = _dot(a.astype(BF16), v[mid:hi] if reverse else v[lo:mid])
    st_new = st * decay + ds
    o = jnp.concatenate(o_diag, axis=0) + o_inter
    for lv in range(HGRN_MASKED_LEVELS, HGRN_LEVELS):
        contrib = []
        for j in range(n >> (lv + 1)):
            zeros = jnp.zeros((1 << lv, HEAD_DIM), F32)
            contrib += [o_big[lv, j], zeros] if reverse else [zeros, o_big[lv, j]]
        o = o + jnp.concatenate(contrib, axis=0)
    return o, st_new


def _hgrn_kernel(*refs, n_blocks, heads, has_state):
    if has_state:
        (q_ref, v_ref, gf_ref, gb_ref, gate_ref, g_ref, mask_ref, s0_ref, o_ref, acc_ref) = refs
    else:
        (q_ref, v_ref, gf_ref, gb_ref, gate_ref, g_ref, mask_ref, o_ref, sout_ref, acc_ref) = refs
    n = HGRN_ROWS
    sub = lax.broadcasted_iota(jnp.int32, (n // 8, 8, HEAD_DIM), 1)
    g_refs = (gf_ref, gb_ref)
    for h in range(heads):
        lanes = slice(h * HEAD_DIM, (h + 1) * HEAD_DIM)
        sts = [s0_ref[0, d, h].T if has_state else jnp.zeros((HEAD_DIM, HEAD_DIM), F32) for d in range(2)]
        for j in range(n_blocks):
            for d in range(2):
                blk = (n_blocks - 1 - j) if d else j
                rows = slice(blk * n, (blk + 1) * n)
                g = g_refs[d][0, rows, lanes]
                o, sts[d] = _hgrn_block(q_ref[0, rows, lanes], 1.0 - jnp.exp2(g), g, v_ref[0, rows, lanes],
                                        sts[d], mask_ref, sub, reverse=bool(d))
                acc_ref[d, rows, lanes] = o
        if not has_state:
            for d in range(2):
                sout_ref[0, d, h] = sts[d].T
    for h in range(heads):
        lanes = slice(h * HEAD_DIM, (h + 1) * HEAD_DIM)
        o = acc_ref[0, :, lanes] + acc_ref[1, :, lanes]
        o = o * lax.rsqrt(jnp.mean(o * o, axis=-1, keepdims=True) + NORM_EPS)
        o_ref[0, :, lanes] = (o * g_ref[:, lanes] * gate_ref[0, :, lanes]).astype(BF16)


def _hgrn_call(rq, ri, gf, gb, gate, rnn_g, masks, s0):
    b, l, _ = rq.shape
    n_blocks = l // HGRN_ROWS
    heads = max(1, min(N_HEADS, 4 // n_blocks))
    w = heads * HEAD_DIM
    head = lambda: pl.BlockSpec((1, l, w), lambda i, h: (i, 0, h))
    per_head = lambda rows: pl.BlockSpec((rows, w), lambda i, h: (0, h))
    state_spec = pl.BlockSpec((1, 2, heads, HEAD_DIM, HEAD_DIM), lambda i, h: (i, 0, h, 0, 0))
    in_specs = [head(), head(), head(), head(), head(), per_head(1),
                pl.BlockSpec(masks.shape, lambda i, h: (0, 0, 0))]
    args = [rq, ri, gf, gb, gate, rnn_g, masks]
    out_specs = [head()]
    out_shape = [jax.ShapeDtypeStruct((b, l, GROUP_W), BF16)]
    if s0 is not None:
        in_specs.append(state_spec)
        args.append(s0)
    else:
        out_specs.append(state_spec)
        out_shape.append(jax.ShapeDtypeStruct((b, 2, N_HEADS, HEAD_DIM, HEAD_DIM), F32))
    return pl.pallas_call(
        functools.partial(_hgrn_kernel, n_blocks=n_blocks, heads=heads, has_state=s0 is not None),
        grid=(b, N_HEADS // heads),
        in_specs=in_specs,
        out_specs=out_specs,
        out_shape=out_shape,
        scratch_shapes=[pltpu.VMEM((2, l, w), F32)],
        compiler_params=_params(("arbitrary", "arbitrary")),
        name="hgrn_lat" if s0 is not None else "hgrn_ctx",
    )(*args)


def _tail_kernel(att_ref, rnn_ref, x_ref, mod_ref, wo_ref, g1_ref, b1_ref, wu_ref, cw_ref, cb_ref, wd_ref,
                 g2_ref, b2_ref, o_ref, x1_ref, xm2_ref, hid_ref, *, seq_len):
    rows = x_ref.shape[0]
    m = mod_ref[0]
    slabs = [slice(r, r + NORM_ROWS) for r in range(0, rows, NORM_ROWS)]
    mixes = [_dot(att_ref[sl, :], wo_ref[0:GROUP_W, :]) + _dot(rnn_ref[sl, :], wo_ref[GROUP_W:2 * GROUP_W, :])
             for sl in slabs]
    for sl, mix in zip(slabs, mixes):
        x1 = _layer_norm(DEEPNORM_ALPHA * x_ref[sl, :] + m[2:3, :] * mix, g1_ref[...], b1_ref[...])
        x1_ref[sl, :] = x1
        xm2_ref[sl, :] = (x1 * (1.0 + m[4:5, :]) + m[3:4, :]).astype(BF16)

    sub = lax.broadcasted_iota(jnp.int32, (8, FF_TILE), 0)

    def shifted(h, shift, edge_row):
        r = pltpu.roll(h, shift, axis=0)
        parts = []
        for s in range(rows // seq_len):
            edge = s * seq_len + (edge_row // 8) * 8
            fixed = jnp.where(sub == edge_row % 8, 0.0, r[edge:edge + 8])
            parts += [r[s * seq_len:edge], fixed, r[edge + 8:(s + 1) * seq_len]]
        return jnp.concatenate([p for p in parts if p.shape[0]], axis=0)

    def conv(h, cols):
        prev = shifted(h, 1, 0)
        nxt = shifted(h, rows - 1, seq_len - 1)
        return prev * cw_ref[0:1, cols] + h * cw_ref[1:2, cols] + nxt * cw_ref[2:3, cols] + cb_ref[:, cols]

    for j in range(D_FF // FF_TILE):
        cols_a = slice(j * FF_TILE, (j + 1) * FF_TILE)
        cols_u = slice(D_FF + j * FF_TILE, D_FF + (j + 1) * FF_TILE)
        a = conv(_dot(xm2_ref[...], wu_ref[:, cols_a]), cols_a)
        u = conv(_dot(xm2_ref[...], wu_ref[:, cols_u]), cols_u)
        hid_ref[:, cols_a] = (_silu(a) * u).astype(BF16)
    ffns = [_dot(hid_ref[sl, :], wd_ref[...]) for sl in slabs]
    for sl, ffn in zip(slabs, ffns):
        y = DEEPNORM_ALPHA * x1_ref[sl, :] + m[5:6, :] * ffn
        o_ref[sl, :] = _layer_norm(y, g2_ref[...], b2_ref[...])


def _tail_call(att, rnn, x2d, mod3, w_out, ln1_g, ln1_b, w_up, conv_w, conv_b, w_down, ln2_g, ln2_b,
               *, latent, seq_len):
    t = x2d.shape[0]
    rb = ROWS_FFN
    assert rb % seq_len == 0

    def mod_idx(i):
        return ((1 + i * rb // seq_len) if latent else 0, 0, 0)

    row_spec = lambda w: pl.BlockSpec((rb, w), lambda i: (i, 0))
    resident = lambda a: pl.BlockSpec(a.shape, lambda i: (0, 0), pipeline_mode=pl.Buffered(1))
    return pl.pallas_call(
        functools.partial(_tail_kernel, seq_len=seq_len),
        grid=(t // rb,),
        in_specs=[row_spec(GROUP_W), row_spec(GROUP_W), row_spec(D_MODEL),
                  pl.BlockSpec((1, N_MOD, D_MODEL), mod_idx),
                  resident(w_out), resident(ln1_g), resident(ln1_b),
                  resident(w_up), resident(conv_w), resident(conv_b), resident(w_down),
                  resident(ln2_g), resident(ln2_b)],
        out_specs=row_spec(D_MODEL),
        out_shape=jax.ShapeDtypeStruct((t, D_MODEL), F32),
        scratch_shapes=[pltpu.VMEM((rb, D_MODEL), F32), pltpu.VMEM((rb, D_MODEL), BF16),
                        pltpu.VMEM((rb, D_FF), BF16)],
        compiler_params=_params(("arbitrary",)),
        name="tail_lat" if latent else "tail_ctx",
    )(att, rnn, x2d, mod3, w_out, ln1_g, ln1_b, w_up, conv_w, conv_b, w_down, ln2_g, ln2_b)


def _rope_tables(seq_len):
    quarter = QK_DIM // 4
    freqs = 1.0 / (ROPE_BASE ** (np.arange(quarter, dtype=np.float64) / quarter))
    t = np.arange(seq_len)
    ang_r = (t // GRID_W)[:, None] * freqs
    ang_c = (t % GRID_W)[:, None] * freqs
    zeros = np.zeros_like(ang_r)

    def tile(parts):
        return jnp.asarray(np.tile(np.concatenate(parts, axis=-1), (1, GROUP_W // QK_DIM)).astype(np.float32))

    cos = tile([np.cos(ang_r), np.cos(ang_r), np.cos(ang_c), np.cos(ang_c)])
    sin_a = tile([-np.sin(ang_r), zeros, -np.sin(ang_c), zeros])
    sin_b = tile([zeros, np.sin(ang_r), zeros, np.sin(ang_c)])
    return cos, sin_a, sin_b


def kernel(x_prompt, x_sample, cache_k, cache_v, state_rnn, c, c_ctx, w_ada, b_ada, w_in, lambda_q1, lambda_k1, lambda_q2, lambda_k2, lb_fwd_logits, lb_bwd_logits, att_norm_g, rnn_norm_g, w_out, ln1_g, ln1_b, w_up, conv_w, conv_b, w_down, ln2_g, ln2_b):
    assert w_ada.shape[0] == DEPTH
    bp, lp, d = x_prompt.shape
    bs, ls, _ = x_sample.shape
    past = cache_k.shape[2]

    cond = jnp.zeros((16, d), F32).at[0].set(c_ctx).at[1:1 + bs].set(c)
    mod3 = _mod_call(cond, w_ada[0], b_ada).reshape(16, N_MOD, d)

    w_in_b = w_in[0].astype(BF16)
    lams = (lambda_q1, lambda_k1, lambda_q2, lambda_k2)
    masks = jnp.asarray(_hgrn_masks())
    xp2d = x_prompt.reshape(bp * lp, d)
    xs2d = x_sample.reshape(bs * ls, d)

    ctx_in = _inproj_call(xp2d, mod3, w_in_b, lb_fwd_logits, lb_bwd_logits, None, latent=False, seq_len=lp)
    lat_in = _inproj_call(xs2d, mod3, w_in_b, lb_fwd_logits, lb_bwd_logits, _rope_tables(ls), latent=True,
                          seq_len=ls, casts=(w_out[0], w_up[0], w_down[0]))
    k_raw, v_raw = ctx_in[8:]
    w_out_b, w_up_b, w_down_b = lat_in[8:]

    def mix_and_tail(x2d, proj, b, l, *, latent, cache, s0):
        q, k, v, rq, gf, gb, ri, rg = (o.reshape(b, l, GROUP_W) for o in proj[:8])
        att = _attn_call(q, k, v, cache, lams, att_norm_g)
        hg = _hgrn_call(rq, ri, gf, gb, rg, rnn_norm_g, masks, s0)
        y = _tail_call(att.reshape(b * l, GROUP_W), hg[0].reshape(b * l, GROUP_W), x2d, mod3, w_out_b, ln1_g,
                       ln1_b, w_up_b, conv_w[0], conv_b, w_down_b, ln2_g, ln2_b, latent=latent, seq_len=l)
        return y.reshape(b, l, d), hg[1:]

    y_p, (s_new,) = mix_and_tail(xp2d, ctx_in, bp, lp, latent=False, cache=None, s0=None)
    cache = (jnp.transpose(cache_k, (0, 1, 3, 4, 5, 2)).reshape(bs, GROUP_W, past),
             cache_v.reshape(bs, past * N_HEADS, HEAD_DIM))
    y_s, _ = mix_and_tail(xs2d, lat_in, bs, ls, latent=True, cache=cache,
                          s0=state_rnn.reshape(bs, 2, N_HEADS, HEAD_DIM, HEAD_DIM))

    new_cache_k = k_raw.reshape(bp, DEPTH, lp, N_HEADS, 2, QK_DIM)
    new_cache_v = v_raw.reshape(bp, DEPTH, lp, N_HEADS, HEAD_DIM)
    new_state = s_new.reshape(bp, DEPTH, 2, N_HEADS, HEAD_DIM, HEAD_DIM)
    return (y_p, y_s, new_cache_k, new_cache_v, new_state)
```

```python
import functools
import math

import jax
import jax.numpy as jnp
import numpy as np
from jax import lax
from jax.experimental import pallas as pl
from jax.experimental.pallas import tpu as pltpu

D_MODEL = 1024
GRID_W = 64
N_HEADS = 4
HEAD_DIM = 128
QK_DIM = 64
GROUP_W = 512
N_GROUPS = 8
D_FF = 2816
N_MOD = 6
ROPE_BASE = 10000.0
DEPTH = 1
DEEPNORM_ALPHA = (2.0 * DEPTH) ** 0.25
NORM_EPS = 1e-5
LAM_INIT = 0.8 - 0.6 * math.exp(-0.3 * 0)
LOG2_E = math.log2(math.e)

V7X_VMEM_BYTES = 64 * 1024 * 1024
VMEM_LIMIT = V7X_VMEM_BYTES * 15 // 16

MOD_TILE = 384
ROWS_PROJ = 512
ROWS_FFN = 1024
FF_TILE = 256
NORM_ROWS = 256
Q_ROWS = 256
HGRN_ROWS = 256
HGRN_LEVELS = int(math.log2(HGRN_ROWS))

F32 = jnp.float32
BF16 = jnp.bfloat16


def _params(semantics):
    return pltpu.CompilerParams(dimension_semantics=semantics, vmem_limit_bytes=VMEM_LIMIT)


def _dot(a, b):
    return jnp.dot(a, b, preferred_element_type=F32)


def _dot_nt(a, b):
    return lax.dot_general(a, b, (((1,), (1,)), ((), ())), preferred_element_type=F32)


def _dot_tn(a, b):
    return lax.dot_general(a, b, (((0,), (0,)), ((), ())), preferred_element_type=F32)


def _silu(x):
    return x * jax.nn.sigmoid(x)


def _layer_norm(y, g, b):
    mu = jnp.mean(y, axis=-1, keepdims=True)
    d = y - mu
    var = jnp.mean(d * d, axis=-1, keepdims=True)
    return d * lax.rsqrt(var + NORM_EPS) * g + b


def _mod_kernel(c_ref, w_ref, b_ref, win_ref, o_ref, win_out_ref):
    s = _silu(c_ref[...]).astype(BF16)
    o_ref[...] = _dot(s, w_ref[...].astype(BF16)) + b_ref[...]
    win_out_ref[...] = win_ref[...].astype(BF16)


def _mod_call(cond, w_ada, b_ada, w_in):
    n, d = cond.shape
    cols = w_ada.shape[1]
    tile = MOD_TILE
    n_steps = cols // tile
    slab = pl.BlockSpec((w_in.shape[0] // n_steps, w_in.shape[1]), lambda j: (j, 0))
    return pl.pallas_call(
        _mod_kernel,
        grid=(n_steps,),
        in_specs=[pl.BlockSpec((n, d), lambda j: (0, 0)),
                  pl.BlockSpec((d, tile), lambda j: (0, j)),
                  pl.BlockSpec((1, tile), lambda j: (0, j)),
                  slab],
        out_specs=[pl.BlockSpec((n, tile), lambda j: (0, j)), slab],
        out_shape=[jax.ShapeDtypeStruct((n, cols), F32), jax.ShapeDtypeStruct(w_in.shape, BF16)],
        compiler_params=_params(("arbitrary",)),
        name="mod",
    )(cond, w_ada, b_ada, w_in)


def _rope(x, cos_ref, sin_a_ref, sin_b_ref):
    parts = []
    for c in range(GROUP_W // 128):
        sl = slice(c * 128, (c + 1) * 128)
        xc = x[:, sl]
        parts.append(xc * cos_ref[:, sl]
                     + pltpu.roll(xc, 128 - 16, axis=1) * sin_a_ref[:, sl]
                     + pltpu.roll(xc, 16, axis=1) * sin_b_ref[:, sl])
    return jnp.concatenate(parts, axis=1)


def _inproj_kernel(*refs, latent, n_casts):
    if latent:
        (x_ref, mod_ref, w_ref, cos_ref, sin_a_ref, sin_b_ref) = refs[:6]
        cast_in = refs[6:6 + n_casts]
        (q_ref, k_ref, v_ref, rq_ref, xf_ref, xb_ref, ri_ref, rg_ref) = refs[6 + n_casts:14 + n_casts]
        cast_out = refs[14 + n_casts:]
        for src, dst in zip(cast_in, cast_out):
            dst[...] = src[...].astype(BF16)
    else:
        (x_ref, mod_ref, w_ref,
         q_ref, k_ref, v_ref, rq_ref, xf_ref, xb_ref, ri_ref, rg_ref, kraw_ref, vraw_ref) = refs
    m = mod_ref[0]
    xm = (x_ref[...] * (1.0 + m[1:2, :]) + m[0:1, :]).astype(BF16)

    def proj(g):
        return _dot(xm, w_ref[:, g * GROUP_W:(g + 1) * GROUP_W])

    aq = proj(0)
    ak = proj(1)
    av = proj(2)
    if latent:
        aq = _rope(aq, cos_ref, sin_a_ref, sin_b_ref)
        ak = _rope(ak, cos_ref, sin_a_ref, sin_b_ref)
    else:
        rows = ak.shape[0]
        for j in range(2 * N_HEADS):
            kraw_ref[pl.ds(j, rows, stride=2 * N_HEADS), :] = ak[:, j * QK_DIM:(j + 1) * QK_DIM]
        for h in range(N_HEADS):
            vraw_ref[pl.ds(h, rows, stride=N_HEADS), :] = av[:, h * HEAD_DIM:(h + 1) * HEAD_DIM]
    q_ref[...] = (aq * (QK_DIM ** -0.5 * LOG2_E)).astype(BF16)
    k_ref[...] = ak.astype(BF16)
    v_ref[...] = av.astype(BF16)
    rq_ref[...] = proj(3)
    xf_ref[...] = proj(4)
    xb_ref[...] = proj(5)
    ri_ref[...] = proj(6).astype(BF16)
    rg_ref[...] = _silu(proj(7))


def _inproj_call(x2d, mod3, w_in, rope_tabs, *, latent, seq_len, casts=()):
    t = x2d.shape[0]
    rb = ROWS_PROJ
    blocks_per_seq = seq_len // rb if latent else 1

    def mod_idx(i):
        return ((1 + i // blocks_per_seq) if latent else 0, 0, 0)

    row_spec = lambda w: pl.BlockSpec((rb, w), lambda i: (i, 0))
    in_specs = [row_spec(D_MODEL),
                pl.BlockSpec((1, N_MOD, D_MODEL), mod_idx),
                pl.BlockSpec(w_in.shape, lambda i: (0, 0))]
    args = [x2d, mod3, w_in]
    n_steps = t // rb
    slab = lambda a: pl.BlockSpec((a.shape[0] // n_steps, a.shape[1]), lambda i: (i, 0))
    if latent:
        in_specs += [pl.BlockSpec((rb, GROUP_W), lambda i: (i % blocks_per_seq, 0))] * 3
        args += list(rope_tabs)
        in_specs += [slab(a) for a in casts]
        args += list(casts)
    dts = [BF16, BF16, BF16, F32, F32, F32, BF16, F32]
    out_specs = [row_spec(GROUP_W) for _ in dts]
    out_shape = [jax.ShapeDtypeStruct((t, GROUP_W), dt) for dt in dts]
    if latent:
        out_specs += [slab(a) for a in casts]
        out_shape += [jax.ShapeDtypeStruct(a.shape, BF16) for a in casts]
    else:
        out_specs += [pl.BlockSpec((rb * 2 * N_HEADS, QK_DIM), lambda i: (i, 0)),
                      pl.BlockSpec((rb * N_HEADS, HEAD_DIM), lambda i: (i, 0))]
        out_shape += [jax.ShapeDtypeStruct((t * 2 * N_HEADS, QK_DIM), F32),
                      jax.ShapeDtypeStruct((t * N_HEADS, HEAD_DIM), F32)]
    return pl.pallas_call(
        functools.partial(_inproj_kernel, latent=latent, n_casts=len(casts)),
        grid=(n_steps,),
        in_specs=in_specs,
        out_specs=out_specs,
        out_shape=out_shape,
        compiler_params=_params(("arbitrary",)),
        name="inproj_lat" if latent else "inproj_ctx",
    )(*args)


def _attn_kernel(*refs, has_cache):
    if has_cache:
        (q_ref, k_ref, v_ref, ck_ref, cv_ref, lq1, lk1, lq2, lk2, g_ref, o_ref) = refs
    else:
        (q_ref, k_ref, v_ref, lq1, lk1, lq2, lk2, g_ref, o_ref) = refs
    lam = (jnp.exp(jnp.sum(lq1[...] * lk1[...], axis=-1, keepdims=True))
           - jnp.exp(jnp.sum(lq2[...] * lk2[...], axis=-1, keepdims=True)) + LAM_INIT)
    qb = q_ref.shape[1]
    lane = lax.broadcasted_iota(jnp.int32, (qb, HEAD_DIM), 1)
    first_map = lane < QK_DIM
    zero = jnp.zeros((), BF16)

    def scores(h):
        sl = slice(h * HEAD_DIM, (h + 1) * HEAD_DIM)
        qh = q_ref[0, :, sl]
        qq = jnp.concatenate([jnp.where(first_map, qh, zero), jnp.where(first_map, zero, qh)], axis=0)
        s_n = _dot_nt(k_ref[0, :, sl], qq)
        s_c = None
        if has_cache:
            s_c = _dot_nt(ck_ref[0, sl, :].astype(BF16).T, qq)
        return s_n, s_c

    ahead = 3
    pending = [scores(h) for h in range(ahead)]
    for h in range(N_HEADS):
        sl = slice(h * HEAD_DIM, (h + 1) * HEAD_DIM)
        s_n, s_c = pending.pop(0)
        if h + ahead < N_HEADS:
            pending.append(scores(h + ahead))
        mx = jnp.max(s_n, axis=0, keepdims=True)
        if has_cache:
            mx = jnp.maximum(mx, jnp.max(s_c, axis=0, keepdims=True))
        e_n = jnp.exp2(s_n - mx)
        den = jnp.sum(e_n, axis=0, keepdims=True)
        ev = _dot_tn(v_ref[0, :, sl], e_n.astype(BF16))
        if has_cache:
            e_c = jnp.exp2(s_c - mx)
            den = den + jnp.sum(e_c, axis=0, keepdims=True)
            vc = cv_ref[0, pl.ds(h, e_c.shape[0], stride=N_HEADS), :]
            ev = ev + _dot_tn(vc.astype(BF16), e_c.astype(BF16))
        inv = 1.0 / den
        o = ev[:, :qb] * inv[:, :qb] - ev[:, qb:] * (inv[:, qb:] * lam)
        o = o * lax.rsqrt(jnp.mean(o * o, axis=0, keepdims=True) + NORM_EPS)
        o_ref[0, :, sl] = (o.T * g_ref[:, sl] * (1.0 - LAM_INIT)).astype(BF16)


def _attn_call(q, k, v, cache, lams, att_g):
    b, l, _ = q.shape
    qb = Q_ROWS
    full = lambda a: pl.BlockSpec((1,) + a.shape[1:], lambda i, j: (i,) + (0,) * (a.ndim - 1))
    in_specs = [pl.BlockSpec((1, qb, GROUP_W), lambda i, j: (i, j, 0)), full(k), full(v)]
    args = [q, k, v]
    if cache is not None:
        in_specs += [full(cache[0]), full(cache[1])]
        args += list(cache)
    in_specs += [pl.BlockSpec((1, QK_DIM), lambda i, j: (0, 0))] * 4
    in_specs += [pl.BlockSpec((1, GROUP_W), lambda i, j: (0, 0))]
    args += list(lams) + [att_g]
    return pl.pallas_call(
        functools.partial(_attn_kernel, has_cache=cache is not None),
        grid=(b, l // qb),
        in_specs=in_specs,
        out_specs=pl.BlockSpec((1, qb, GROUP_W), lambda i, j: (i, j, 0)),
        out_shape=jax.ShapeDtypeStruct((b, l, GROUP_W), BF16),
        compiler_params=_params(("arbitrary", "arbitrary")),
        name="attn_lat" if cache is not None else "attn_ctx",
    )(*args)


HGRN_DIAG = 128
HGRN_MASKED_LEVELS = 5
HGRN_VREG_LEVELS = 3


def _hgrn_masks():
    r = np.arange(HGRN_DIAG)[:, None]
    c = np.arange(HGRN_DIAG)[None, :]
    levels = []
    for lv in range(HGRN_MASKED_LEVELS):
        same = (r >> (lv + 1)) == (c >> (lv + 1))
        levels.append(same & (((r >> lv) & 1) == 1) & (((c >> lv) & 1) == 0))
    fwd = [r == c] + levels
    bwd = [m.T for m in levels]
    return np.stack(fwd + bwd).astype(np.float32)


def _hgrn_block(q, k, g, v, st, mask_ref, sub, *, reverse):
    n = HGRN_ROWS
    n_diag = n // HGRN_DIAG

    def diag_scores(qe, ke, m):
        return [_dot_nt(qe[i * HGRN_DIAG:(i + 1) * HGRN_DIAG], ke[i * HGRN_DIAG:(i + 1) * HGRN_DIAG]) * m
                for i in range(n_diag)]

    def level_mask(lv):
        return mask_ref[(1 + HGRN_MASKED_LEVELS + lv) if reverse else (1 + lv)]

    g3 = g.reshape(n // 8, 8, HEAD_DIM)
    zero3 = jnp.zeros_like(g3)
    pre3, suf3 = (zero3, g3) if reverse else (g3, zero3)
    tot3 = g3
    qb, kb = q.astype(BF16), k.astype(BF16)
    acc = diag_scores(qb, kb, mask_ref[0])
    for lv in range(HGRN_VREG_LEVELS):
        b = 1 << lv
        upper = ((sub >> lv) & 1) == 1
        e = jnp.exp2(jnp.where(upper, pre3, suf3)).reshape(n, HEAD_DIM)
        eb = e.astype(BF16)
        part = diag_scores(qb * eb, kb * eb, level_mask(lv))
        acc = [a + p for a, p in zip(acc, part)]
        sib = jnp.where(upper, pltpu.roll(tot3, b, axis=1), pltpu.roll(tot3, 8 - b, axis=1))
        pre3 = pre3 + jnp.where(upper, sib, 0.0)
        suf3 = suf3 + jnp.where(upper, 0.0, sib)
        tot3 = tot3 + sib

    pieces = lambda x3: [x3[i] for i in range(n // 8)]
    pre8, suf8, tot8 = pieces(pre3), pieces(suf3), pieces(tot3)
    rows = lambda xs, lo, hi: jnp.concatenate(xs[lo // 8:hi // 8], axis=0) if hi - lo > 8 else xs[lo // 8]
    big_scores = {}
    for lv in range(HGRN_VREG_LEVELS, HGRN_LEVELS):
        b = 1 << lv
        pb = b // 8
        if lv < HGRN_MASKED_LEVELS:
            u = jnp.concatenate([(pre8 if (i // pb) % 2 else suf8)[i] for i in range(n // 8)], axis=0)
            e = jnp.exp2(u)
            eb = e.astype(BF16)
            part = diag_scores(qb * eb, kb * eb, level_mask(lv))
            acc = [a + p for a, p in zip(acc, part)]
        else:
            for j in range(n // (2 * b)):
                lo, mid, hi = 2 * b * j, 2 * b * j + b, 2 * b * (j + 1)
                e_lo = jnp.exp2(rows(suf8, lo, mid))
                e_up = jnp.exp2(rows(pre8, mid, hi))
                x_lo = (qb if reverse else kb)[lo:mid] * e_lo.astype(BF16)
                x_up = (kb if reverse else qb)[mid:hi] * e_up.astype(BF16)
                big_scores[lv, j] = _dot_nt(x_lo, x_up) if reverse else _dot_nt(x_up, x_lo)
        for j in range(n // (2 * b)):
            lo_p, mid_p, hi_p = 2 * pb * j, 2 * pb * j + pb, 2 * pb * (j + 1)
            t_lo, t_up = tot8[lo_p], tot8[mid_p]
            t_new = t_lo + t_up
            for i in range(lo_p, mid_p):
                suf8[i] = suf8[i] + t_up
                tot8[i] = t_new
            for i in range(mid_p, hi_p):
                pre8[i] = pre8[i] + t_lo
                tot8[i] = t_new
    pre = jnp.concatenate(pre8, axis=0)
    suf = jnp.concatenate(suf8, axis=0)
    q_dec, k_dec = (suf, pre) if reverse else (pre, suf)
    q_in = qb * jnp.exp2(q_dec).astype(BF16)
    k_out = kb * jnp.exp2(k_dec).astype(BF16)
    decay = jnp.exp2(tot8[0][0:1, :])

    o_inter = _dot_nt(q_in, st.astype(BF16))
    ds = _dot_tn(v, k_out)
    o_diag = [_dot(acc[i].astype(BF16), v[i * HGRN_DIAG:(i + 1) * HGRN_DIAG]) for i in range(n_diag)]
    o_big = {}
    for (lv, j), a in big_scores.items():
        lo, mid, hi = (2 * j) << lv, (2 * j + 1) << lv, (2 * j + 2) << lv
        o_big[lv, j] = _dot(a.astype(BF16), v[mid:hi] if reverse else v[lo:mid])
    st_new = st * decay + ds
    o = jnp.concatenate(o_diag, axis=0) + o_inter
    for lv in range(HGRN_MASKED_LEVELS, HGRN_LEVELS):
        contrib = []
        for j in range(n >> (lv + 1)):
            zeros = jnp.zeros((1 << lv, HEAD_DIM), F32)
            contrib += [o_big[lv, j], zeros] if reverse else [zeros, o_big[lv, j]]
        o = o + jnp.concatenate(contrib, axis=0)
    return o, st_new


def _hgrn_kernel(*refs, n_blocks, heads, has_state):
    if has_state:
        (q_ref, v_ref, xf_ref, xb_ref, gate_ref, lbf_ref, lbb_ref, g_ref, mask_ref, s0_ref,
         o_ref, acc_ref) = refs
    else:
        (q_ref, v_ref, xf_ref, xb_ref, gate_ref, lbf_ref, lbb_ref, g_ref, mask_ref,
         o_ref, sout_ref, acc_ref) = refs
    n = HGRN_ROWS
    sub = lax.broadcasted_iota(jnp.int32, (n // 8, 8, HEAD_DIM), 1)

    def lower_bound(ref, lanes):
        l0, l1 = ref[0:1, lanes], ref[1:2, lanes]
        mx = jnp.maximum(l0, l1)
        e0, e1 = jnp.exp(l0 - mx), jnp.exp(l1 - mx)
        return e0 / (e0 + e1)

    x_refs = (xf_ref, xb_ref)
    for h in range(heads):
        lanes = slice(h * HEAD_DIM, (h + 1) * HEAD_DIM)
        lbs = [lower_bound(lbf_ref, lanes), lower_bound(lbb_ref, lanes)]
        sts = [s0_ref[0, d, h].T if has_state else jnp.zeros((HEAD_DIM, HEAD_DIM), F32) for d in range(2)]
        for j in range(n_blocks):
            for d in range(2):
                blk = (n_blocks - 1 - j) if d else j
                rows = slice(blk * n, (blk + 1) * n)
                f = lbs[d] + (1.0 - lbs[d]) * jax.nn.sigmoid(x_refs[d][0, rows, lanes])
                o, sts[d] = _hgrn_block(q_ref[0, rows, lanes], 1.0 - f, jnp.log2(f), v_ref[0, rows, lanes],
                                        sts[d], mask_ref, sub, reverse=bool(d))
                acc_ref[d, rows, lanes] = o
        if not has_state:
            for d in range(2):
                sout_ref[0, d, h] = sts[d].T
    for h in range(heads):
        lanes = slice(h * HEAD_DIM, (h + 1) * HEAD_DIM)
        o = acc_ref[0, :, lanes] + acc_ref[1, :, lanes]
        o = o * lax.rsqrt(jnp.mean(o * o, axis=-1, keepdims=True) + NORM_EPS)
        o_ref[0, :, lanes] = (o * g_ref[:, lanes] * gate_ref[0, :, lanes]).astype(BF16)


def _hgrn_call(rq, ri, xf, xb, gate, lbf, lbb, rnn_g, masks, s0):
    b, l, _ = rq.shape
    n_blocks = l // HGRN_ROWS
    heads = max(1, min(N_HEADS, 4 // n_blocks))
    w = heads * HEAD_DIM
    head = lambda: pl.BlockSpec((1, l, w), lambda i, h: (i, 0, h))
    per_head = lambda rows: pl.BlockSpec((rows, w), lambda i, h: (0, h))
    state_spec = pl.BlockSpec((1, 2, heads, HEAD_DIM, HEAD_DIM), lambda i, h: (i, 0, h, 0, 0))
    in_specs = [head(), head(), head(), head(), head(), per_head(2), per_head(2), per_head(1),
                pl.BlockSpec(masks.shape, lambda i, h: (0, 0, 0))]
    args = [rq, ri, xf, xb, gate, lbf, lbb, rnn_g, masks]
    out_specs = [head()]
    out_shape = [jax.ShapeDtypeStruct((b, l, GROUP_W), BF16)]
    if s0 is not None:
        in_specs.append(state_spec)
        args.append(s0)
    else:
        out_specs.append(state_spec)
        out_shape.append(jax.ShapeDtypeStruct((b, 2, N_HEADS, HEAD_DIM, HEAD_DIM), F32))
    return pl.pallas_call(
        functools.partial(_hgrn_kernel, n_blocks=n_blocks, heads=heads, has_state=s0 is not None),
        grid=(b, N_HEADS // heads),
        in_specs=in_specs,
        out_specs=out_specs,
        out_shape=out_shape,
        scratch_shapes=[pltpu.VMEM((2, l, w), F32)],
        compiler_params=_params(("arbitrary", "arbitrary")),
        name="hgrn_lat" if s0 is not None else "hgrn_ctx",
    )(*args)


def _tail_kernel(att_ref, rnn_ref, x_ref, mod_ref, wo_ref, g1_ref, b1_ref, wu_ref, cw_ref, cb_ref, wd_ref,
                 g2_ref, b2_ref, o_ref, x1_ref, xm2_ref, hid_ref, *, seq_len):
    rows = x_ref.shape[0]
    m = mod_ref[0]
    slabs = [slice(r, r + NORM_ROWS) for r in range(0, rows, NORM_ROWS)]
    mixes = [_dot(att_ref[sl, :], wo_ref[0:GROUP_W, :]) + _dot(rnn_ref[sl, :], wo_ref[GROUP_W:2 * GROUP_W, :])
             for sl in slabs]
    for sl, mix in zip(slabs, mixes):
        x1 = _layer_norm(DEEPNORM_ALPHA * x_ref[sl, :] + m[2:3, :] * mix, g1_ref[...], b1_ref[...])
        x1_ref[sl, :] = x1
        xm2_ref[sl, :] = (x1 * (1.0 + m[4:5, :]) + m[3:4, :]).astype(BF16)

    sub = lax.broadcasted_iota(jnp.int32, (8, FF_TILE), 0)

    def shifted(h, shift, edge_row):
        r = pltpu.roll(h, shift, axis=0)
        parts = []
        for s in range(rows // seq_len):
            edge = s * seq_len + (edge_row // 8) * 8
            fixed = jnp.where(sub == edge_row % 8, 0.0, r[edge:edge + 8])
            parts += [r[s * seq_len:edge], fixed, r[edge + 8:(s + 1) * seq_len]]
        return jnp.concatenate([p for p in parts if p.shape[0]], axis=0)

    def conv(h, cols):
        prev = shifted(h, 1, 0)
        nxt = shifted(h, rows - 1, seq_len - 1)
        return prev * cw_ref[0:1, cols] + h * cw_ref[1:2, cols] + nxt * cw_ref[2:3, cols] + cb_ref[:, cols]

    for j in range(D_FF // FF_TILE):
        cols_a = slice(j * FF_TILE, (j + 1) * FF_TILE)
        cols_u = slice(D_FF + j * FF_TILE, D_FF + (j + 1) * FF_TILE)
        a = conv(_dot(xm2_ref[...], wu_ref[:, cols_a]), cols_a)
        u = conv(_dot(xm2_ref[...], wu_ref[:, cols_u]), cols_u)
        hid_ref[:, cols_a] = (_silu(a) * u).astype(BF16)
    ffns = [_dot(hid_ref[sl, :], wd_ref[...]) for sl in slabs]
    for sl, ffn in zip(slabs, ffns):
        y = DEEPNORM_ALPHA * x1_ref[sl, :] + m[5:6, :] * ffn
        o_ref[sl, :] = _layer_norm(y, g2_ref[...], b2_ref[...])


def _tail_call(att, rnn, x2d, mod3, w_out, ln1_g, ln1_b, w_up, conv_w, conv_b, w_down, ln2_g, ln2_b,
               *, latent, seq_len):
    t = x2d.shape[0]
    rb = ROWS_FFN
    assert rb % seq_len == 0

    def mod_idx(i):
        return ((1 + i * rb // seq_len) if latent else 0, 0, 0)

    row_spec = lambda w: pl.BlockSpec((rb, w), lambda i: (i, 0))
    resident = lambda a: pl.BlockSpec(a.shape, lambda i: (0, 0), pipeline_mode=pl.Buffered(1))
    return pl.pallas_call(
        functools.partial(_tail_kernel, seq_len=seq_len),
        grid=(t // rb,),
        in_specs=[row_spec(GROUP_W), row_spec(GROUP_W), row_spec(D_MODEL),
                  pl.BlockSpec((1, N_MOD, D_MODEL), mod_idx),
                  resident(w_out), resident(ln1_g), resident(ln1_b),
                  resident(w_up), resident(conv_w), resident(conv_b), resident(w_down),
                  resident(ln2_g), resident(ln2_b)],
        out_specs=row_spec(D_MODEL),
        out_shape=jax.ShapeDtypeStruct((t, D_MODEL), F32),
        scratch_shapes=[pltpu.VMEM((rb, D_MODEL), F32), pltpu.VMEM((rb, D_MODEL), BF16),
                        pltpu.VMEM((rb, D_FF), BF16)],
        compiler_params=_params(("arbitrary",)),
        name="tail_lat" if latent else "tail_ctx",
    )(att, rnn, x2d, mod3, w_out, ln1_g, ln1_b, w_up, conv_w, conv_b, w_down, ln2_g, ln2_b)


def _rope_tables(seq_len):
    quarter = QK_DIM // 4
    freqs = 1.0 / (ROPE_BASE ** (np.arange(quarter, dtype=np.float64) / quarter))
    t = np.arange(seq_len)
    ang_r = (t // GRID_W)[:, None] * freqs
    ang_c = (t % GRID_W)[:, None] * freqs
    zeros = np.zeros_like(ang_r)

    def tile(parts):
        return jnp.asarray(np.tile(np.concatenate(parts, axis=-1), (1, GROUP_W // QK_DIM)).astype(np.float32))

    cos = tile([np.cos(ang_r), np.cos(ang_r), np.cos(ang_c), np.cos(ang_c)])
    sin_a = tile([-np.sin(ang_r), zeros, -np.sin(ang_c), zeros])
    sin_b = tile([zeros, np.sin(ang_r), zeros, np.sin(ang_c)])
    return cos, sin_a, sin_b


def kernel(x_prompt, x_sample, cache_k, cache_v, state_rnn, c, c_ctx, w_ada, b_ada, w_in, lambda_q1, lambda_k1, lambda_q2, lambda_k2, lb_fwd_logits, lb_bwd_logits, att_norm_g, rnn_norm_g, w_out, ln1_g, ln1_b, w_up, conv_w, conv_b, w_down, ln2_g, ln2_b):
    assert w_ada.shape[0] == DEPTH
    bp, lp, d = x_prompt.shape
    bs, ls, _ = x_sample.shape
    past = cache_k.shape[2]

    cond = jnp.zeros((16, d), F32).at[0].set(c_ctx).at[1:1 + bs].set(c)
    mod, w_in_b = _mod_call(cond, w_ada[0], b_ada, w_in[0])
    mod3 = mod.reshape(16, N_MOD, d)

    lams = (lambda_q1, lambda_k1, lambda_q2, lambda_k2)
    masks = jnp.asarray(_hgrn_masks())
    xp2d = x_prompt.reshape(bp * lp, d)
    xs2d = x_sample.reshape(bs * ls, d)

    ctx_in = _inproj_call(xp2d, mod3, w_in_b, None, latent=False, seq_len=lp)
    lat_in = _inproj_call(xs2d, mod3, w_in_b, _rope_tables(ls), latent=True, seq_len=ls,
                          casts=(w_out[0], w_up[0], w_down[0]))
    k_raw, v_raw = ctx_in[8:]
    w_out_b, w_up_b, w_down_b = lat_in[8:]

    def mix_and_tail(x2d, proj, b, l, *, latent, cache, s0):
        q, k, v, rq, xf, xb, ri, rg = (o.reshape(b, l, GROUP_W) for o in proj[:8])
        att = _attn_call(q, k, v, cache, lams, att_norm_g)
        hg = _hgrn_call(rq, ri, xf, xb, rg, lb_fwd_logits, lb_bwd_logits, rnn_norm_g, masks, s0)
        y = _tail_call(att.reshape(b * l, GROUP_W), hg[0].reshape(b * l, GROUP_W), x2d, mod3, w_out_b, ln1_g,
                       ln1_b, w_up_b, conv_w[0], conv_b, w_down_b, ln2_g, ln2_b, latent=latent, seq_len=l)
        return y.reshape(b, l, d), hg[1:]

    y_p, (s_new,) = mix_and_tail(xp2d, ctx_in, bp, lp, latent=False, cache=None, s0=None)
    cache = (jnp.transpose(cache_k, (0, 1, 3, 4, 5, 2)).reshape(bs, GROUP_W, past),
             cache_v.reshape(bs, past * N_HEADS, HEAD_DIM))
    y_s, _ = mix_and_tail(xs2d, lat_in, bs, ls, latent=True, cache=cache,
                          s0=state_rnn.reshape(bs, 2, N_HEADS, HEAD_DIM, HEAD_DIM))

    new_cache_k = k_raw.reshape(bp, DEPTH, lp, N_HEADS, 2, QK_DIM)
    new_cache_v = v_raw.reshape(bp, DEPTH, lp, N_HEADS, HEAD_DIM)
    new_state = s_new.reshape(bp, DEPTH, 2, N_HEADS, HEAD_DIM, HEAD_DIM)
    return (y_p, y_s, new_cache_k, new_cache_v, new_state)
```

```python
import functools
import math

import jax
import jax.numpy as jnp
import numpy as np
from jax import lax
from jax.experimental import pallas as pl
from jax.experimental.pallas import tpu as pltpu

D_MODEL = 1024
GRID_W = 64
N_HEADS = 4
HEAD_DIM = 128
QK_DIM = 64
GROUP_W = 512
N_GROUPS = 8
D_FF = 2816
N_MOD = 6
ROPE_BASE = 10000.0
DEPTH = 1
DEEPNORM_ALPHA = (2.0 * DEPTH) ** 0.25
NORM_EPS = 1e-5
LAM_INIT = 0.8 - 0.6 * math.exp(-0.3 * 0)
LOG2_E = math.log2(math.e)

V7X_VMEM_BYTES = 64 * 1024 * 1024
VMEM_LIMIT = V7X_VMEM_BYTES * 15 // 16

MOD_TILE = 384
ROWS_PROJ = 512
ROWS_FFN = 1024
FF_TILE = 256
NORM_ROWS = 256
Q_ROWS = 512
HGRN_ROWS = 256
HGRN_LEVELS = int(math.log2(HGRN_ROWS))

F32 = jnp.float32
BF16 = jnp.bfloat16


def _params(semantics):
    return pltpu.CompilerParams(dimension_semantics=semantics, vmem_limit_bytes=VMEM_LIMIT)


def _dot(a, b):
    return jnp.dot(a, b, preferred_element_type=F32)


def _dot_nt(a, b):
    return lax.dot_general(a, b, (((1,), (1,)), ((), ())), preferred_element_type=F32)


def _dot_tn(a, b):
    return lax.dot_general(a, b, (((0,), (0,)), ((), ())), preferred_element_type=F32)


def _silu(x):
    return x * jax.nn.sigmoid(x)


def _layer_norm(y, g, b):
    mu = jnp.mean(y, axis=-1, keepdims=True)
    d = y - mu
    var = jnp.mean(d * d, axis=-1, keepdims=True)
    return d * lax.rsqrt(var + NORM_EPS) * g + b


def _mod_kernel(c_ref, w_ref, b_ref, win_ref, o_ref, win_out_ref):
    s = _silu(c_ref[...]).astype(BF16)
    o_ref[...] = _dot(s, w_ref[...].astype(BF16)) + b_ref[...]
    win_out_ref[...] = win_ref[...].astype(BF16)


def _mod_call(cond, w_ada, b_ada, w_in):
    n, d = cond.shape
    cols = w_ada.shape[1]
    tile = MOD_TILE
    n_steps = cols // tile
    slab = pl.BlockSpec((w_in.shape[0] // n_steps, w_in.shape[1]), lambda j: (j, 0))
    return pl.pallas_call(
        _mod_kernel,
        grid=(n_steps,),
        in_specs=[pl.BlockSpec((n, d), lambda j: (0, 0)),
                  pl.BlockSpec((d, tile), lambda j: (0, j)),
                  pl.BlockSpec((1, tile), lambda j: (0, j)),
                  slab],
        out_specs=[pl.BlockSpec((n, tile), lambda j: (0, j)), slab],
        out_shape=[jax.ShapeDtypeStruct((n, cols), F32), jax.ShapeDtypeStruct(w_in.shape, BF16)],
        compiler_params=_params(("arbitrary",)),
        name="mod",
    )(cond, w_ada, b_ada, w_in)


def _rope(x, cos_ref, sin_a_ref, sin_b_ref):
    parts = []
    for c in range(GROUP_W // 128):
        sl = slice(c * 128, (c + 1) * 128)
        xc = x[:, sl]
        parts.append(xc * cos_ref[:, sl]
                     + pltpu.roll(xc, 128 - 16, axis=1) * sin_a_ref[:, sl]
                     + pltpu.roll(xc, 16, axis=1) * sin_b_ref[:, sl])
    return jnp.concatenate(parts, axis=1)


def _inproj_kernel(*refs, latent, n_casts):
    if latent:
        (x_ref, mod_ref, w_ref, cos_ref, sin_a_ref, sin_b_ref) = refs[:6]
        cast_in = refs[6:6 + n_casts]
        (q_ref, k_ref, v_ref, rq_ref, xf_ref, xb_ref, ri_ref, rg_ref) = refs[6 + n_casts:14 + n_casts]
        cast_out = refs[14 + n_casts:]
        for src, dst in zip(cast_in, cast_out):
            dst[...] = src[...].astype(BF16)
    else:
        (x_ref, mod_ref, w_ref,
         q_ref, k_ref, v_ref, rq_ref, xf_ref, xb_ref, ri_ref, rg_ref, kraw_ref, vraw_ref) = refs
    m = mod_ref[0]
    xm = (x_ref[...] * (1.0 + m[1:2, :]) + m[0:1, :]).astype(BF16)

    def proj(g):
        return _dot(xm, w_ref[:, g * GROUP_W:(g + 1) * GROUP_W])

    aq = proj(0)
    ak = proj(1)
    av = proj(2)
    if latent:
        aq = _rope(aq, cos_ref, sin_a_ref, sin_b_ref)
        ak = _rope(ak, cos_ref, sin_a_ref, sin_b_ref)
    else:
        rows = ak.shape[0]
        for j in range(2 * N_HEADS):
            kraw_ref[pl.ds(j, rows, stride=2 * N_HEADS), :] = ak[:, j * QK_DIM:(j + 1) * QK_DIM]
        for h in range(N_HEADS):
            vraw_ref[pl.ds(h, rows, stride=N_HEADS), :] = av[:, h * HEAD_DIM:(h + 1) * HEAD_DIM]
    q_ref[...] = (aq * (QK_DIM ** -0.5 * LOG2_E)).astype(BF16)
    k_ref[...] = ak.astype(BF16)
    v_ref[...] = av.astype(BF16)
    rq_ref[...] = proj(3)
    xf_ref[...] = proj(4)
    xb_ref[...] = proj(5)
    ri_ref[...] = proj(6).astype(BF16)
    rg_ref[...] = _silu(proj(7))


def _inproj_call(x2d, mod3, w_in, rope_tabs, *, latent, seq_len, casts=()):
    t = x2d.shape[0]
    rb = ROWS_PROJ
    blocks_per_seq = seq_len // rb if latent else 1

    def mod_idx(i):
        return ((1 + i // blocks_per_seq) if latent else 0, 0, 0)

    row_spec = lambda w: pl.BlockSpec((rb, w), lambda i: (i, 0))
    in_specs = [row_spec(D_MODEL),
                pl.BlockSpec((1, N_MOD, D_MODEL), mod_idx),
                pl.BlockSpec(w_in.shape, lambda i: (0, 0))]
    args = [x2d, mod3, w_in]
    n_steps = t // rb
    slab = lambda a: pl.BlockSpec((a.shape[0] // n_steps, a.shape[1]), lambda i: (i, 0))
    if latent:
        in_specs += [pl.BlockSpec((rb, GROUP_W), lambda i: (i % blocks_per_seq, 0))] * 3
        args += list(rope_tabs)
        in_specs += [slab(a) for a in casts]
        args += list(casts)
    dts = [BF16, BF16, BF16, F32, F32, F32, BF16, F32]
    out_specs = [row_spec(GROUP_W) for _ in dts]
    out_shape = [jax.ShapeDtypeStruct((t, GROUP_W), dt) for dt in dts]
    if latent:
        out_specs += [slab(a) for a in casts]
        out_shape += [jax.ShapeDtypeStruct(a.shape, BF16) for a in casts]
    else:
        out_specs += [pl.BlockSpec((rb * 2 * N_HEADS, QK_DIM), lambda i: (i, 0)),
                      pl.BlockSpec((rb * N_HEADS, HEAD_DIM), lambda i: (i, 0))]
        out_shape += [jax.ShapeDtypeStruct((t * 2 * N_HEADS, QK_DIM), F32),
                      jax.ShapeDtypeStruct((t * N_HEADS, HEAD_DIM), F32)]
    return pl.pallas_call(
        functools.partial(_inproj_kernel, latent=latent, n_casts=len(casts)),
        grid=(n_steps,),
        in_specs=in_specs,
        out_specs=out_specs,
        out_shape=out_shape,
        compiler_params=_params(("arbitrary",)),
        name="inproj_lat" if latent else "inproj_ctx",
    )(*args)


def _attn_kernel(*refs, has_cache):
    if has_cache:
        (q_ref, k_ref, v_ref, ck_ref, cv_ref, lq1, lk1, lq2, lk2, g_ref, o_ref) = refs
    else:
        (q_ref, k_ref, v_ref, lq1, lk1, lq2, lk2, g_ref, o_ref) = refs
    lam = (jnp.exp(jnp.sum(lq1[...] * lk1[...], axis=-1, keepdims=True))
           - jnp.exp(jnp.sum(lq2[...] * lk2[...], axis=-1, keepdims=True)) + LAM_INIT)
    qb = q_ref.shape[1]
    lane = lax.broadcasted_iota(jnp.int32, (qb, HEAD_DIM), 1)
    first_map = lane < QK_DIM
    zero = jnp.zeros((), BF16)

    def scores(h):
        sl = slice(h * HEAD_DIM, (h + 1) * HEAD_DIM)
        qh = q_ref[0, :, sl]
        qq = jnp.concatenate([jnp.where(first_map, qh, zero), jnp.where(first_map, zero, qh)], axis=0)
        s_n = _dot_nt(k_ref[0, :, sl], qq)
        s_c = None
        if has_cache:
            s_c = _dot_nt(ck_ref[0, sl, :].astype(BF16).T, qq)
        return s_n, s_c

    ahead = 3
    pending = [scores(h) for h in range(ahead)]
    for h in range(N_HEADS):
        sl = slice(h * HEAD_DIM, (h + 1) * HEAD_DIM)
        s_n, s_c = pending.pop(0)
        if h + ahead < N_HEADS:
            pending.append(scores(h + ahead))
        mx = jnp.max(s_n, axis=0, keepdims=True)
        if has_cache:
            mx = jnp.maximum(mx, jnp.max(s_c, axis=0, keepdims=True))
        e_n = jnp.exp2(s_n - mx)
        den = jnp.sum(e_n, axis=0, keepdims=True)
        ev = _dot_tn(v_ref[0, :, sl], e_n.astype(BF16))
        if has_cache:
            e_c = jnp.exp2(s_c - mx)
            den = den + jnp.sum(e_c, axis=0, keepdims=True)
            vc = cv_ref[0, pl.ds(h, e_c.shape[0], stride=N_HEADS), :]
            ev = ev + _dot_tn(vc.astype(BF16), e_c.astype(BF16))
        inv = 1.0 / den
        o = ev[:, :qb] * inv[:, :qb] - ev[:, qb:] * (inv[:, qb:] * lam)
        o = o * lax.rsqrt(jnp.mean(o * o, axis=0, keepdims=True) + NORM_EPS)
        o_ref[0, :, sl] = (o.T * g_ref[:, sl] * (1.0 - LAM_INIT)).astype(BF16)


def _attn_call(q, k, v, cache, lams, att_g):
    b, l, _ = q.shape
    qb = min(Q_ROWS, l)
    full = lambda a: pl.BlockSpec((1,) + a.shape[1:], lambda i, j: (i,) + (0,) * (a.ndim - 1))
    in_specs = [pl.BlockSpec((1, qb, GROUP_W), lambda i, j: (i, j, 0)), full(k), full(v)]
    args = [q, k, v]
    if cache is not None:
        in_specs += [full(cache[0]), full(cache[1])]
        args += list(cache)
    in_specs += [pl.BlockSpec((1, QK_DIM), lambda i, j: (0, 0))] * 4
    in_specs += [pl.BlockSpec((1, GROUP_W), lambda i, j: (0, 0))]
    args += list(lams) + [att_g]
    return pl.pallas_call(
        functools.partial(_attn_kernel, has_cache=cache is not None),
        grid=(b, l // qb),
        in_specs=in_specs,
        out_specs=pl.BlockSpec((1, qb, GROUP_W), lambda i, j: (i, j, 0)),
        out_shape=jax.ShapeDtypeStruct((b, l, GROUP_W), BF16),
        compiler_params=_params(("arbitrary", "arbitrary")),
        name="attn_lat" if cache is not None else "attn_ctx",
    )(*args)


HGRN_DIAG = 128
HGRN_MASKED_LEVELS = 5
HGRN_VREG_LEVELS = 3


def _hgrn_masks():
    r = np.arange(HGRN_DIAG)[:, None]
    c = np.arange(HGRN_DIAG)[None, :]
    levels = []
    for lv in range(HGRN_MASKED_LEVELS):
        same = (r >> (lv + 1)) == (c >> (lv + 1))
        levels.append(same & (((r >> lv) & 1) == 1) & (((c >> lv) & 1) == 0))
    fwd = [r == c] + levels
    bwd = [m.T for m in levels]
    return np.stack(fwd + bwd).astype(np.float32)


def _hgrn_block(q, k, g, v, st, mask_ref, sub, *, reverse):
    n = HGRN_ROWS
    n_diag = n // HGRN_DIAG

    def diag_scores(qe, ke, m):
        return [_dot_nt(qe[i * HGRN_DIAG:(i + 1) * HGRN_DIAG], ke[i * HGRN_DIAG:(i + 1) * HGRN_DIAG]) * m
                for i in range(n_diag)]

    def level_mask(lv):
        return mask_ref[(1 + HGRN_MASKED_LEVELS + lv) if reverse else (1 + lv)]

    g3 = g.reshape(n // 8, 8, HEAD_DIM)
    zero3 = jnp.zeros_like(g3)
    pre3, suf3 = (zero3, g3) if reverse else (g3, zero3)
    tot3 = g3
    qb, kb = q.astype(BF16), k.astype(BF16)
    acc = diag_scores(qb, kb, mask_ref[0])
    for lv in range(HGRN_VREG_LEVELS):
        b = 1 << lv
        upper = ((sub >> lv) & 1) == 1
        e = jnp.exp2(jnp.where(upper, pre3, suf3)).reshape(n, HEAD_DIM)
        eb = e.astype(BF16)
        part = diag_scores(qb * eb, kb * eb, level_mask(lv))
        acc = [a + p for a, p in zip(acc, part)]
        sib = jnp.where(upper, pltpu.roll(tot3, b, axis=1), pltpu.roll(tot3, 8 - b, axis=1))
        pre3 = pre3 + jnp.where(upper, sib, 0.0)
        suf3 = suf3 + jnp.where(upper, 0.0, sib)
        tot3 = tot3 + sib

    pieces = lambda x3: [x3[i] for i in range(n // 8)]
    pre8, suf8, tot8 = pieces(pre3), pieces(suf3), pieces(tot3)
    rows = lambda xs, lo, hi: jnp.concatenate(xs[lo // 8:hi // 8], axis=0) if hi - lo > 8 else xs[lo // 8]
    big_scores = {}
    for lv in range(HGRN_VREG_LEVELS, HGRN_LEVELS):
        b = 1 << lv
        pb = b // 8
        if lv < HGRN_MASKED_LEVELS:
            u = jnp.concatenate([(pre8 if (i // pb) % 2 else suf8)[i] for i in range(n // 8)], axis=0)
            e = jnp.exp2(u)
            eb = e.astype(BF16)
            part = diag_scores(qb * eb, kb * eb, level_mask(lv))
            acc = [a + p for a, p in zip(acc, part)]
        else:
            for j in range(n // (2 * b)):
                lo, mid, hi = 2 * b * j, 2 * b * j + b, 2 * b * (j + 1)
                e_lo = jnp.exp2(rows(suf8, lo, mid))
                e_up = jnp.exp2(rows(pre8, mid, hi))
                x_lo = (qb if reverse else kb)[lo:mid] * e_lo.astype(BF16)
                x_up = (kb if reverse else qb)[mid:hi] * e_up.astype(BF16)
                big_scores[lv, j] = _dot_nt(x_lo, x_up) if reverse else _dot_nt(x_up, x_lo)
        for j in range(n // (2 * b)):
            lo_p, mid_p, hi_p = 2 * pb * j, 2 * pb * j + pb, 2 * pb * (j + 1)
            t_lo, t_up = tot8[lo_p], tot8[mid_p]
            t_new = t_lo + t_up
            for i in range(lo_p, mid_p):
                suf8[i] = suf8[i] + t_up
                tot8[i] = t_new
            for i in range(mid_p, hi_p):
                pre8[i] = pre8[i] + t_lo
                tot8[i] = t_new
    pre = jnp.concatenate(pre8, axis=0)
    suf = jnp.concatenate(suf8, axis=0)
    q_dec, k_dec = (suf, pre) if reverse else (pre, suf)
    q_in = qb * jnp.exp2(q_dec).astype(BF16)
    k_out = kb * jnp.exp2(k_dec).astype(BF16)
    decay = jnp.exp2(tot8[0][0:1, :])

    o_inter = _dot_nt(q_in, st.astype(BF16))
    ds = _dot_tn(v, k_out)
    o_diag = [_dot(acc[i].astype(BF16), v[i * HGRN_DIAG:(i + 1) * HGRN_DIAG]) for i in range(n_diag)]
    o_big = {}
    for (lv, j), a in big_scores.items():
        lo, mid, hi = (2 * j) << lv, (2 * j + 1) << lv, (2 * j + 2) << lv
        o_big[lv, j] = _dot(a.astype(BF16), v[mid:hi] if reverse else v[lo:mid])
    st_new = st * decay + ds
    o = jnp.concatenate(o_diag, axis=0) + o_inter
    for lv in range(HGRN_MASKED_LEVELS, HGRN_LEVELS):
        contrib = []
        for j in range(n >> (lv + 1)):
            zeros = jnp.zeros((1 << lv, HEAD_DIM), F32)
            contrib += [o_big[lv, j], zeros] if reverse else [zeros, o_big[lv, j]]
        o = o + jnp.concatenate(contrib, axis=0)
    return o, st_new


def _hgrn_kernel(*refs, n_blocks, heads, has_state):
    if has_state:
        (q_ref, v_ref, xf_ref, xb_ref, gate_ref, lbf_ref, lbb_ref, g_ref, mask_ref, s0_ref,
         o_ref, acc_ref) = refs
    else:
        (q_ref, v_ref, xf_ref, xb_ref, gate_ref, lbf_ref, lbb_ref, g_ref, mask_ref,
         o_ref, sout_ref, acc_ref) = refs
    n = HGRN_ROWS
    sub = lax.broadcasted_iota(jnp.int32, (n // 8, 8, HEAD_DIM), 1)

    def lower_bound(ref, lanes):
        l0, l1 = ref[0:1, lanes], ref[1:2, lanes]
        mx = jnp.maximum(l0, l1)
        e0, e1 = jnp.exp(l0 - mx), jnp.exp(l1 - mx)
        return e0 / (e0 + e1)

    x_refs = (xf_ref, xb_ref)
    for h in range(heads):
        lanes = slice(h * HEAD_DIM, (h + 1) * HEAD_DIM)
        lbs = [lower_bound(lbf_ref, lanes), lower_bound(lbb_ref, lanes)]
        sts = [s0_ref[0, d, h].T if has_state else jnp.zeros((HEAD_DIM, HEAD_DIM), F32) for d in range(2)]
        for j in range(n_blocks):
            for d in range(2):
                blk = (n_blocks - 1 - j) if d else j
                rows = slice(blk * n, (blk + 1) * n)
                f = lbs[d] + (1.0 - lbs[d]) * jax.nn.sigmoid(x_refs[d][0, rows, lanes])
                o, sts[d] = _hgrn_block(q_ref[0, rows, lanes], 1.0 - f, jnp.log2(f), v_ref[0, rows, lanes],
                                        sts[d], mask_ref, sub, reverse=bool(d))
                acc_ref[d, rows, lanes] = o
        if not has_state:
            for d in range(2):
                sout_ref[0, d, h] = sts[d].T
    for h in range(heads):
        lanes = slice(h * HEAD_DIM, (h + 1) * HEAD_DIM)
        o = acc_ref[0, :, lanes] + acc_ref[1, :, lanes]
        o = o * lax.rsqrt(jnp.mean(o * o, axis=-1, keepdims=True) + NORM_EPS)
        o_ref[0, :, lanes] = (o * g_ref[:, lanes] * gate_ref[0, :, lanes]).astype(BF16)


def _hgrn_call(rq, ri, xf, xb, gate, lbf, lbb, rnn_g, masks, s0):
    b, l, _ = rq.shape
    n_blocks = l // HGRN_ROWS
    heads = N_HEADS
    w = heads * HEAD_DIM
    head = lambda: pl.BlockSpec((1, l, w), lambda i, h: (i, 0, h))
    per_head = lambda rows: pl.BlockSpec((rows, w), lambda i, h: (0, h))
    state_spec = pl.BlockSpec((1, 2, heads, HEAD_DIM, HEAD_DIM), lambda i, h: (i, 0, h, 0, 0))
    in_specs = [head(), head(), head(), head(), head(), per_head(2), per_head(2), per_head(1),
                pl.BlockSpec(masks.shape, lambda i, h: (0, 0, 0))]
    args = [rq, ri, xf, xb, gate, lbf, lbb, rnn_g, masks]
    out_specs = [head()]
    out_shape = [jax.ShapeDtypeStruct((b, l, GROUP_W), BF16)]
    if s0 is not None:
        in_specs.append(state_spec)
        args.append(s0)
    else:
        out_specs.append(state_spec)
        out_shape.append(jax.ShapeDtypeStruct((b, 2, N_HEADS, HEAD_DIM, HEAD_DIM), F32))
    return pl.pallas_call(
        functools.partial(_hgrn_kernel, n_blocks=n_blocks, heads=heads, has_state=s0 is not None),
        grid=(b, N_HEADS // heads),
        in_specs=in_specs,
        out_specs=out_specs,
        out_shape=out_shape,
        scratch_shapes=[pltpu.VMEM((2, l, w), F32)],
        compiler_params=_params(("arbitrary", "arbitrary")),
        name="hgrn_lat" if s0 is not None else "hgrn_ctx",
    )(*args)


def _tail_kernel(att_ref, rnn_ref, x_ref, mod_ref, wo_ref, g1_ref, b1_ref, wu_ref, cw_ref, cb_ref, wd_ref,
                 g2_ref, b2_ref, o_ref, x1_ref, xm2_ref, hid_ref, *, seq_len):
    rows = x_ref.shape[0]
    m = mod_ref[0]
    slabs = [slice(r, r + NORM_ROWS) for r in range(0, rows, NORM_ROWS)]
    mixes = [_dot(att_ref[sl, :], wo_ref[0:GROUP_W, :]) + _dot(rnn_ref[sl, :], wo_ref[GROUP_W:2 * GROUP_W, :])
             for sl in slabs]
    for sl, mix in zip(slabs, mixes):
        x1 = _layer_norm(DEEPNORM_ALPHA * x_ref[sl, :] + m[2:3, :] * mix, g1_ref[...], b1_ref[...])
        x1_ref[sl, :] = x1
        xm2_ref[sl, :] = (x1 * (1.0 + m[4:5, :]) + m[3:4, :]).astype(BF16)

    sub = lax.broadcasted_iota(jnp.int32, (8, FF_TILE), 0)

    def shifted(h, shift, edge_row):
        r = pltpu.roll(h, shift, axis=0)
        parts = []
        for s in range(rows // seq_len):
            edge = s * seq_len + (edge_row // 8) * 8
            fixed = jnp.where(sub == edge_row % 8, 0.0, r[edge:edge + 8])
            parts += [r[s * seq_len:edge], fixed, r[edge + 8:(s + 1) * seq_len]]
        return jnp.concatenate([p for p in parts if p.shape[0]], axis=0)

    def conv(h, cols):
        prev = shifted(h, 1, 0)
        nxt = shifted(h, rows - 1, seq_len - 1)
        return prev * cw_ref[0:1, cols] + h * cw_ref[1:2, cols] + nxt * cw_ref[2:3, cols] + cb_ref[:, cols]

    for j in range(D_FF // FF_TILE):
        cols_a = slice(j * FF_TILE, (j + 1) * FF_TILE)
        cols_u = slice(D_FF + j * FF_TILE, D_FF + (j + 1) * FF_TILE)
        a = conv(_dot(xm2_ref[...], wu_ref[:, cols_a]), cols_a)
        u = conv(_dot(xm2_ref[...], wu_ref[:, cols_u]), cols_u)
        hid_ref[:, cols_a] = (_silu(a) * u).astype(BF16)
    ffns = [_dot(hid_ref[sl, :], wd_ref[...]) for sl in slabs]
    for sl, ffn in zip(slabs, ffns):
        y = DEEPNORM_ALPHA * x1_ref[sl, :] + m[5:6, :] * ffn
        o_ref[sl, :] = _layer_norm(y, g2_ref[...], b2_ref[...])


def _tail_call(att, rnn, x2d, mod3, w_out, ln1_g, ln1_b, w_up, conv_w, conv_b, w_down, ln2_g, ln2_b,
               *, latent, seq_len):
    t = x2d.shape[0]
    rb = ROWS_FFN
    assert rb % seq_len == 0

    def mod_idx(i):
        return ((1 + i * rb // seq_len) if latent else 0, 0, 0)

    row_spec = lambda w: pl.BlockSpec((rb, w), lambda i: (i, 0))
    resident = lambda a: pl.BlockSpec(a.shape, lambda i: (0, 0), pipeline_mode=pl.Buffered(1))
    return pl.pallas_call(
        functools.partial(_tail_kernel, seq_len=seq_len),
        grid=(t // rb,),
        in_specs=[row_spec(GROUP_W), row_spec(GROUP_W), row_spec(D_MODEL),
                  pl.BlockSpec((1, N_MOD, D_MODEL), mod_idx),
                  resident(w_out), resident(ln1_g), resident(ln1_b),
                  resident(w_up), resident(conv_w), resident(conv_b), resident(w_down),
                  resident(ln2_g), resident(ln2_b)],
        out_specs=row_spec(D_MODEL),
        out_shape=jax.ShapeDtypeStruct((t, D_MODEL), F32),
        scratch_shapes=[pltpu.VMEM((rb, D_MODEL), F32), pltpu.VMEM((rb, D_MODEL), BF16),
                        pltpu.VMEM((rb, D_FF), BF16)],
        compiler_params=_params(("arbitrary",)),
        name="tail_lat" if latent else "tail_ctx",
    )(att, rnn, x2d, mod3, w_out, ln1_g, ln1_b, w_up, conv_w, conv_b, w_down, ln2_g, ln2_b)


def _rope_tables(seq_len):
    quarter = QK_DIM // 4
    freqs = 1.0 / (ROPE_BASE ** (np.arange(quarter, dtype=np.float64) / quarter))
    t = np.arange(seq_len)
    ang_r = (t // GRID_W)[:, None] * freqs
    ang_c = (t % GRID_W)[:, None] * freqs
    zeros = np.zeros_like(ang_r)

    def tile(parts):
        return jnp.asarray(np.tile(np.concatenate(parts, axis=-1), (1, GROUP_W // QK_DIM)).astype(np.float32))

    cos = tile([np.cos(ang_r), np.cos(ang_r), np.cos(ang_c), np.cos(ang_c)])
    sin_a = tile([-np.sin(ang_r), zeros, -np.sin(ang_c), zeros])
    sin_b = tile([zeros, np.sin(ang_r), zeros, np.sin(ang_c)])
    return cos, sin_a, sin_b


def kernel(x_prompt, x_sample, cache_k, cache_v, state_rnn, c, c_ctx, w_ada, b_ada, w_in, lambda_q1, lambda_k1, lambda_q2, lambda_k2, lb_fwd_logits, lb_bwd_logits, att_norm_g, rnn_norm_g, w_out, ln1_g, ln1_b, w_up, conv_w, conv_b, w_down, ln2_g, ln2_b):
    assert w_ada.shape[0] == DEPTH
    bp, lp, d = x_prompt.shape
    bs, ls, _ = x_sample.shape
    past = cache_k.shape[2]

    cond = jnp.zeros((16, d), F32).at[0].set(c_ctx).at[1:1 + bs].set(c)
    mod, w_in_b = _mod_call(cond, w_ada[0], b_ada, w_in[0])
    mod3 = mod.reshape(16, N_MOD, d)

    lams = (lambda_q1, lambda_k1, lambda_q2, lambda_k2)
    masks = jnp.asarray(_hgrn_masks())
    xp2d = x_prompt.reshape(bp * lp, d)
    xs2d = x_sample.reshape(bs * ls, d)

    ctx_in = _inproj_call(xp2d, mod3, w_in_b, None, latent=False, seq_len=lp)
    lat_in = _inproj_call(xs2d, mod3, w_in_b, _rope_tables(ls), latent=True, seq_len=ls,
                          casts=(w_out[0], w_up[0], w_down[0]))
    k_raw, v_raw = ctx_in[8:]
    w_out_b, w_up_b, w_down_b = lat_in[8:]

    def mix_and_tail(x2d, proj, b, l, *, latent, cache, s0):
        q, k, v, rq, xf, xb, ri, rg = (o.reshape(b, l, GROUP_W) for o in proj[:8])
        att = _attn_call(q, k, v, cache, lams, att_norm_g)
        hg = _hgrn_call(rq, ri, xf, xb, rg, lb_fwd_logits, lb_bwd_logits, rnn_norm_g, masks, s0)
        y = _tail_call(att.reshape(b * l, GROUP_W), hg[0].reshape(b * l, GROUP_W), x2d, mod3, w_out_b, ln1_g,
                       ln1_b, w_up_b, conv_w[0], conv_b, w_down_b, ln2_g, ln2_b, latent=latent, seq_len=l)
        return y.reshape(b, l, d), hg[1:]

    y_p, (s_new,) = mix_and_tail(xp2d, ctx_in, bp, lp, latent=False, cache=None, s0=None)
    cache = (jnp.transpose(cache_k, (0, 1, 3, 4, 5, 2)).reshape(bs, GROUP_W, past),
             cache_v.reshape(bs, past * N_HEADS, HEAD_DIM))
    y_s, _ = mix_and_tail(xs2d, lat_in, bs, ls, latent=True, cache=cache,
                          s0=state_rnn.reshape(bs, 2, N_HEADS, HEAD_DIM, HEAD_DIM))

    new_cache_k = k_raw.reshape(bp, DEPTH, lp, N_HEADS, 2, QK_DIM)
    new_cache_v = v_raw.reshape(bp, DEPTH, lp, N_HEADS, HEAD_DIM)
    new_state = s_new.reshape(bp, DEPTH, 2, N_HEADS, HEAD_DIM, HEAD_DIM)
    return (y_p, y_s, new_cache_k, new_cache_v, new_state)
```

```python
import functools
import math

import jax
import jax.numpy as jnp
import numpy as np
from jax import lax
from jax.experimental import pallas as pl
from jax.experimental.pallas import tpu as pltpu

D_MODEL = 1024
GRID_W = 64
N_HEADS = 4
HEAD_DIM = 128
QK_DIM = 64
GROUP_W = 512
N_GROUPS = 8
D_FF = 2816
N_MOD = 6
ROPE_BASE = 10000.0
DEPTH = 1
DEEPNORM_ALPHA = (2.0 * DEPTH) ** 0.25
NORM_EPS = 1e-5
LAM_INIT = 0.8 - 0.6 * math.exp(-0.3 * 0)
LOG2_E = math.log2(math.e)

V7X_VMEM_BYTES = 64 * 1024 * 1024
VMEM_LIMIT = V7X_VMEM_BYTES * 15 // 16

MOD_TILE = 384
ROWS_PROJ = 512
ROWS_FFN = 1024
FF_TILE = 256
NORM_ROWS = 256
Q_ROWS = 512
HGRN_ROWS = 256
HGRN_LEVELS = int(math.log2(HGRN_ROWS))

F32 = jnp.float32
BF16 = jnp.bfloat16


def _params(semantics):
    return pltpu.CompilerParams(dimension_semantics=semantics, vmem_limit_bytes=VMEM_LIMIT)


def _dot(a, b):
    return jnp.dot(a, b, preferred_element_type=F32)


def _dot_nt(a, b):
    return lax.dot_general(a, b, (((1,), (1,)), ((), ())), preferred_element_type=F32)


def _dot_tn(a, b):
    return lax.dot_general(a, b, (((0,), (0,)), ((), ())), preferred_element_type=F32)


def _silu(x):
    return x * jax.nn.sigmoid(x)


def _layer_norm(y, g, b):
    mu = jnp.mean(y, axis=-1, keepdims=True)
    d = y - mu
    var = jnp.mean(d * d, axis=-1, keepdims=True)
    return d * lax.rsqrt(var + NORM_EPS) * g + b


def _mod_kernel(c_ref, w_ref, b_ref, win_ref, o_ref, win_out_ref):
    s = _silu(c_ref[...]).astype(BF16)
    o_ref[...] = _dot(s, w_ref[...].astype(BF16)) + b_ref[...]
    win_out_ref[...] = win_ref[...].astype(BF16)


def _mod_call(cond, w_ada, b_ada, w_in):
    n, d = cond.shape
    cols = w_ada.shape[1]
    tile = MOD_TILE
    n_steps = cols // tile
    slab = pl.BlockSpec((w_in.shape[0] // n_steps, w_in.shape[1]), lambda j: (j, 0))
    return pl.pallas_call(
        _mod_kernel,
        grid=(n_steps,),
        in_specs=[pl.BlockSpec((n, d), lambda j: (0, 0)),
                  pl.BlockSpec((d, tile), lambda j: (0, j)),
                  pl.BlockSpec((1, tile), lambda j: (0, j)),
                  slab],
        out_specs=[pl.BlockSpec((n, tile), lambda j: (0, j)), slab],
        out_shape=[jax.ShapeDtypeStruct((n, cols), F32), jax.ShapeDtypeStruct(w_in.shape, BF16)],
        compiler_params=_params(("arbitrary",)),
        name="mod",
    )(cond, w_ada, b_ada, w_in)


def _rope(x, cos_ref, sin_a_ref, sin_b_ref):
    parts = []
    for c in range(GROUP_W // 128):
        sl = slice(c * 128, (c + 1) * 128)
        xc = x[:, sl]
        parts.append(xc * cos_ref[:, sl]
                     + pltpu.roll(xc, 128 - 16, axis=1) * sin_a_ref[:, sl]
                     + pltpu.roll(xc, 16, axis=1) * sin_b_ref[:, sl])
    return jnp.concatenate(parts, axis=1)


def _inproj_kernel(*refs, latent):
    if latent:
        (x_ref, mod_ref, w_ref, cos_ref, sin_a_ref, sin_b_ref,
         q_ref, k_ref, v_ref, rq_ref, xf_ref, xb_ref, ri_ref, rg_ref) = refs
    else:
        (x_ref, mod_ref, w_ref,
         q_ref, k_ref, v_ref, rq_ref, xf_ref, xb_ref, ri_ref, rg_ref, kraw_ref, vraw_ref) = refs
    m = mod_ref[0]
    xm = (x_ref[...] * (1.0 + m[1:2, :]) + m[0:1, :]).astype(BF16)

    def proj(g):
        return _dot(xm, w_ref[:, g * GROUP_W:(g + 1) * GROUP_W])

    aq = proj(0)
    ak = proj(1)
    av = proj(2)
    if latent:
        aq = _rope(aq, cos_ref, sin_a_ref, sin_b_ref)
        ak = _rope(ak, cos_ref, sin_a_ref, sin_b_ref)
    else:
        rows = ak.shape[0]
        for j in range(2 * N_HEADS):
            kraw_ref[pl.ds(j, rows, stride=2 * N_HEADS), :] = ak[:, j * QK_DIM:(j + 1) * QK_DIM]
        for h in range(N_HEADS):
            vraw_ref[pl.ds(h, rows, stride=N_HEADS), :] = av[:, h * HEAD_DIM:(h + 1) * HEAD_DIM]
    q_ref[...] = (aq * (QK_DIM ** -0.5 * LOG2_E)).astype(BF16)
    k_ref[...] = ak.astype(BF16)
    v_ref[...] = av.astype(BF16)
    rq_ref[...] = proj(3)
    xf_ref[...] = proj(4)
    xb_ref[...] = proj(5)
    ri_ref[...] = proj(6).astype(BF16)
    rg_ref[...] = _silu(proj(7))


def _inproj_call(x2d, mod3, w_in, rope_tabs, *, latent, seq_len):
    t = x2d.shape[0]
    rb = ROWS_PROJ
    blocks_per_seq = seq_len // rb if latent else 1

    def mod_idx(i):
        return ((1 + i // blocks_per_seq) if latent else 0, 0, 0)

    row_spec = lambda w: pl.BlockSpec((rb, w), lambda i: (i, 0))
    in_specs = [row_spec(D_MODEL),
                pl.BlockSpec((1, N_MOD, D_MODEL), mod_idx),
                pl.BlockSpec(w_in.shape, lambda i: (0, 0))]
    args = [x2d, mod3, w_in]
    if latent:
        in_specs += [pl.BlockSpec((rb, GROUP_W), lambda i: (i % blocks_per_seq, 0))] * 3
        args += list(rope_tabs)
    dts = [BF16, BF16, BF16, F32, F32, F32, BF16, F32]
    out_specs = [row_spec(GROUP_W) for _ in dts]
    out_shape = [jax.ShapeDtypeStruct((t, GROUP_W), dt) for dt in dts]
    if not latent:
        out_specs += [pl.BlockSpec((rb * 2 * N_HEADS, QK_DIM), lambda i: (i, 0)),
                      pl.BlockSpec((rb * N_HEADS, HEAD_DIM), lambda i: (i, 0))]
        out_shape += [jax.ShapeDtypeStruct((t * 2 * N_HEADS, QK_DIM), F32),
                      jax.ShapeDtypeStruct((t * N_HEADS, HEAD_DIM), F32)]
    return pl.pallas_call(
        functools.partial(_inproj_kernel, latent=latent),
        grid=(t // rb,),
        in_specs=in_specs,
        out_specs=out_specs,
        out_shape=out_shape,
        compiler_params=_params(("arbitrary",)),
        name="inproj_lat" if latent else "inproj_ctx",
    )(*args)


def _attn_kernel(*refs, has_cache, n_casts):
    if has_cache:
        (q_ref, k_ref, v_ref, ck_ref, cv_ref, lq1, lk1, lq2, lk2, g_ref) = refs[:10]
        o_ref = refs[10 + n_casts]
        for src, dst in zip(refs[10:10 + n_casts], refs[11 + n_casts:]):
            dst[...] = src[...].astype(BF16)
    else:
        (q_ref, k_ref, v_ref, lq1, lk1, lq2, lk2, g_ref, o_ref) = refs
    lam = (jnp.exp(jnp.sum(lq1[...] * lk1[...], axis=-1, keepdims=True))
           - jnp.exp(jnp.sum(lq2[...] * lk2[...], axis=-1, keepdims=True)) + LAM_INIT)
    qb = q_ref.shape[1]
    lane = lax.broadcasted_iota(jnp.int32, (qb, HEAD_DIM), 1)
    first_map = lane < QK_DIM
    zero = jnp.zeros((), BF16)

    def scores(h):
        sl = slice(h * HEAD_DIM, (h + 1) * HEAD_DIM)
        qh = q_ref[0, :, sl]
        qq = jnp.concatenate([jnp.where(first_map, qh, zero), jnp.where(first_map, zero, qh)], axis=0)
        s_n = _dot_nt(k_ref[0, :, sl], qq)
        s_c = None
        if has_cache:
            s_c = _dot_nt(ck_ref[0, sl, :].astype(BF16).T, qq)
        return s_n, s_c

    ahead = 3
    pending = [scores(h) for h in range(ahead)]
    for h in range(N_HEADS):
        sl = slice(h * HEAD_DIM, (h + 1) * HEAD_DIM)
        s_n, s_c = pending.pop(0)
        if h + ahead < N_HEADS:
            pending.append(scores(h + ahead))
        mx = jnp.max(s_n, axis=0, keepdims=True)
        if has_cache:
            mx = jnp.maximum(mx, jnp.max(s_c, axis=0, keepdims=True))
        e_n = jnp.exp2(s_n - mx)
        den = jnp.sum(e_n, axis=0, keepdims=True)
        ev = _dot_tn(v_ref[0, :, sl], e_n.astype(BF16))
        if has_cache:
            e_c = jnp.exp2(s_c - mx)
            den = den + jnp.sum(e_c, axis=0, keepdims=True)
            vc = cv_ref[0, pl.ds(h, e_c.shape[0], stride=N_HEADS), :]
            ev = ev + _dot_tn(vc.astype(BF16), e_c.astype(BF16))
        inv = 1.0 / den
        o = ev[:, :qb] * inv[:, :qb] - ev[:, qb:] * (inv[:, qb:] * lam)
        o = o * lax.rsqrt(jnp.mean(o * o, axis=0, keepdims=True) + NORM_EPS)
        o_ref[0, :, sl] = (o.T * g_ref[:, sl] * (1.0 - LAM_INIT)).astype(BF16)


def _attn_call(q, k, v, cache, lams, att_g, casts=()):
    b, l, _ = q.shape
    qb = min(Q_ROWS, l)
    nq = l // qb
    slab = lambda a: pl.BlockSpec((a.shape[0] // (b * nq), a.shape[1]), lambda i, j: (i * nq + j, 0))
    full = lambda a: pl.BlockSpec((1,) + a.shape[1:], lambda i, j: (i,) + (0,) * (a.ndim - 1))
    in_specs = [pl.BlockSpec((1, qb, GROUP_W), lambda i, j: (i, j, 0)), full(k), full(v)]
    args = [q, k, v]
    if cache is not None:
        in_specs += [full(cache[0]), full(cache[1])]
        args += list(cache)
    in_specs += [pl.BlockSpec((1, QK_DIM), lambda i, j: (0, 0))] * 4
    in_specs += [pl.BlockSpec((1, GROUP_W), lambda i, j: (0, 0))]
    args += list(lams) + [att_g] + list(casts)
    in_specs += [slab(a) for a in casts]
    return pl.pallas_call(
        functools.partial(_attn_kernel, has_cache=cache is not None, n_casts=len(casts)),
        grid=(b, nq),
        in_specs=in_specs,
        out_specs=[pl.BlockSpec((1, qb, GROUP_W), lambda i, j: (i, j, 0))] + [slab(a) for a in casts],
        out_shape=[jax.ShapeDtypeStruct((b, l, GROUP_W), BF16)] + [jax.ShapeDtypeStruct(a.shape, BF16) for a in casts],
        compiler_params=_params(("arbitrary", "arbitrary")),
        name="attn_lat" if cache is not None else "attn_ctx",
    )(*args)


HGRN_DIAG = 128
HGRN_MASKED_LEVELS = 5
HGRN_VREG_LEVELS = 3


def _hgrn_masks():
    r = np.arange(HGRN_DIAG)[:, None]
    c = np.arange(HGRN_DIAG)[None, :]
    levels = []
    for lv in range(HGRN_MASKED_LEVELS):
        same = (r >> (lv + 1)) == (c >> (lv + 1))
        levels.append(same & (((r >> lv) & 1) == 1) & (((c >> lv) & 1) == 0))
    fwd = [r == c] + levels
    bwd = [m.T for m in levels]
    return np.stack(fwd + bwd).astype(np.float32)


def _hgrn_block(q, k, g, v, st, mask_ref, sub, *, reverse):
    n = HGRN_ROWS
    n_diag = n // HGRN_DIAG

    def diag_scores(qe, ke, m):
        return [_dot_nt(qe[i * HGRN_DIAG:(i + 1) * HGRN_DIAG], ke[i * HGRN_DIAG:(i + 1) * HGRN_DIAG]) * m
                for i in range(n_diag)]

    def level_mask(lv):
        return mask_ref[(1 + HGRN_MASKED_LEVELS + lv) if reverse else (1 + lv)]

    g3 = g.reshape(n // 8, 8, HEAD_DIM)
    zero3 = jnp.zeros_like(g3)
    pre3, suf3 = (zero3, g3) if reverse else (g3, zero3)
    tot3 = g3
    qb, kb = q.astype(BF16), k.astype(BF16)
    acc = diag_scores(qb, kb, mask_ref[0])
    for lv in range(HGRN_VREG_LEVELS):
        b = 1 << lv
        upper = ((sub >> lv) & 1) == 1
        e = jnp.exp2(jnp.where(upper, pre3, suf3)).reshape(n, HEAD_DIM)
        eb = e.astype(BF16)
        part = diag_scores(qb * eb, kb * eb, level_mask(lv))
        acc = [a + p for a, p in zip(acc, part)]
        sib = jnp.where(upper, pltpu.roll(tot3, b, axis=1), pltpu.roll(tot3, 8 - b, axis=1))
        pre3 = pre3 + jnp.where(upper, sib, 0.0)
        suf3 = suf3 + jnp.where(upper, 0.0, sib)
        tot3 = tot3 + sib

    pieces = lambda x3: [x3[i] for i in range(n // 8)]
    pre8, suf8, tot8 = pieces(pre3), pieces(suf3), pieces(tot3)
    rows = lambda xs, lo, hi: jnp.concatenate(xs[lo // 8:hi // 8], axis=0) if hi - lo > 8 else xs[lo // 8]
    big_scores = {}
    for lv in range(HGRN_VREG_LEVELS, HGRN_LEVELS):
        b = 1 << lv
        pb = b // 8
        if lv < HGRN_MASKED_LEVELS:
            u = jnp.concatenate([(pre8 if (i // pb) % 2 else suf8)[i] for i in range(n // 8)], axis=0)
            e = jnp.exp2(u)
            eb = e.astype(BF16)
            part = diag_scores(qb * eb, kb * eb, level_mask(lv))
            acc = [a + p for a, p in zip(acc, part)]
        else:
            for j in range(n // (2 * b)):
                lo, mid, hi = 2 * b * j, 2 * b * j + b, 2 * b * (j + 1)
                e_lo = jnp.exp2(rows(suf8, lo, mid))
                e_up = jnp.exp2(rows(pre8, mid, hi))
                x_lo = (qb if reverse else kb)[lo:mid] * e_lo.astype(BF16)
                x_up = (kb if reverse else qb)[mid:hi] * e_up.astype(BF16)
                big_scores[lv, j] = _dot_nt(x_lo, x_up) if reverse else _dot_nt(x_up, x_lo)
        for j in range(n // (2 * b)):
            lo_p, mid_p, hi_p = 2 * pb * j, 2 * pb * j + pb, 2 * pb * (j + 1)
            t_lo, t_up = tot8[lo_p], tot8[mid_p]
            t_new = t_lo + t_up
            for i in range(lo_p, mid_p):
                suf8[i] = suf8[i] + t_up
                tot8[i] = t_new
            for i in range(mid_p, hi_p):
                pre8[i] = pre8[i] + t_lo
                tot8[i] = t_new
    pre = jnp.concatenate(pre8, axis=0)
    suf = jnp.concatenate(suf8, axis=0)
    q_dec, k_dec = (suf, pre) if reverse else (pre, suf)
    q_in = qb * jnp.exp2(q_dec).astype(BF16)
    k_out = kb * jnp.exp2(k_dec).astype(BF16)
    decay = jnp.exp2(tot8[0][0:1, :])

    o_inter = _dot_nt(q_in, st.astype(BF16))
    ds = _dot_tn(v, k_out)
    o_diag = [_dot(acc[i].astype(BF16), v[i * HGRN_DIAG:(i + 1) * HGRN_DIAG]) for i in range(n_diag)]
    o_big = {}
    for (lv, j), a in big_scores.items():
        lo, mid, hi = (2 * j) << lv, (2 * j + 1) << lv, (2 * j + 2) << lv
        o_big[lv, j] = _dot(a.astype(BF16), v[mid:hi] if reverse else v[lo:mid])
    st_new = st * decay + ds
    o = jnp.concatenate(o_diag, axis=0) + o_inter
    for lv in range(HGRN_MASKED_LEVELS, HGRN_LEVELS):
        contrib = []
        for j in range(n >> (lv + 1)):
            zeros = jnp.zeros((1 << lv, HEAD_DIM), F32)
            contrib += [o_big[lv, j], zeros] if reverse else [zeros, o_big[lv, j]]
        o = o + jnp.concatenate(contrib, axis=0)
    return o, st_new


def _hgrn_kernel(*refs, n_blocks, heads, has_state):
    if has_state:
        (q_ref, v_ref, xf_ref, xb_ref, gate_ref, lbf_ref, lbb_ref, g_ref, mask_ref, s0_ref,
         o_ref, acc_ref) = refs
    else:
        (q_ref, v_ref, xf_ref, xb_ref, gate_ref, lbf_ref, lbb_ref, g_ref, mask_ref,
         o_ref, sout_ref, acc_ref) = refs
    n = HGRN_ROWS
    sub = lax.broadcasted_iota(jnp.int32, (n // 8, 8, HEAD_DIM), 1)

    def lower_bound(ref, lanes):
        l0, l1 = ref[0:1, lanes], ref[1:2, lanes]
        mx = jnp.maximum(l0, l1)
        e0, e1 = jnp.exp(l0 - mx), jnp.exp(l1 - mx)
        return e0 / (e0 + e1)

    x_refs = (xf_ref, xb_ref)
    for h in range(heads):
        lanes = slice(h * HEAD_DIM, (h + 1) * HEAD_DIM)
        lbs = [lower_bound(lbf_ref, lanes), lower_bound(lbb_ref, lanes)]
        sts = [s0_ref[0, d, h].T if has_state else jnp.zeros((HEAD_DIM, HEAD_DIM), F32) for d in range(2)]
        for j in range(n_blocks):
            for d in range(2):
                blk = (n_blocks - 1 - j) if d else j
                rows = slice(blk * n, (blk + 1) * n)
                f = lbs[d] + (1.0 - lbs[d]) * jax.nn.sigmoid(x_refs[d][0, rows, lanes])
                o, sts[d] = _hgrn_block(q_ref[0, rows, lanes], 1.0 - f, jnp.log2(f), v_ref[0, rows, lanes],
                                        sts[d], mask_ref, sub, reverse=bool(d))
                acc_ref[d, rows, lanes] = o
        if not has_state:
            for d in range(2):
                sout_ref[0, d, h] = sts[d].T
    for h in range(heads):
        lanes = slice(h * HEAD_DIM, (h + 1) * HEAD_DIM)
        o = acc_ref[0, :, lanes] + acc_ref[1, :, lanes]
        o = o * lax.rsqrt(jnp.mean(o * o, axis=-1, keepdims=True) + NORM_EPS)
        o_ref[0, :, lanes] = (o * g_ref[:, lanes] * gate_ref[0, :, lanes]).astype(BF16)


def _hgrn_call(rq, ri, xf, xb, gate, lbf, lbb, rnn_g, masks, s0):
    b, l, _ = rq.shape
    n_blocks = l // HGRN_ROWS
    heads = N_HEADS
    w = heads * HEAD_DIM
    head = lambda: pl.BlockSpec((1, l, w), lambda i, h: (i, 0, h))
    per_head = lambda rows: pl.BlockSpec((rows, w), lambda i, h: (0, h))
    state_spec = pl.BlockSpec((1, 2, heads, HEAD_DIM, HEAD_DIM), lambda i, h: (i, 0, h, 0, 0))
    in_specs = [head(), head(), head(), head(), head(), per_head(2), per_head(2), per_head(1),
                pl.BlockSpec(masks.shape, lambda i, h: (0, 0, 0))]
    args = [rq, ri, xf, xb, gate, lbf, lbb, rnn_g, masks]
    out_specs = [head()]
    out_shape = [jax.ShapeDtypeStruct((b, l, GROUP_W), BF16)]
    if s0 is not None:
        in_specs.append(state_spec)
        args.append(s0)
    else:
        out_specs.append(state_spec)
        out_shape.append(jax.ShapeDtypeStruct((b, 2, N_HEADS, HEAD_DIM, HEAD_DIM), F32))
    return pl.pallas_call(
        functools.partial(_hgrn_kernel, n_blocks=n_blocks, heads=heads, has_state=s0 is not None),
        grid=(b, N_HEADS // heads),
        in_specs=in_specs,
        out_specs=out_specs,
        out_shape=out_shape,
        scratch_shapes=[pltpu.VMEM((2, l, w), F32)],
        compiler_params=_params(("arbitrary", "arbitrary")),
        name="hgrn_lat" if s0 is not None else "hgrn_ctx",
    )(*args)


def _tail_kernel(att_ref, rnn_ref, x_ref, mod_ref, wo_ref, g1_ref, b1_ref, wu_ref, cw_ref, cb_ref, wd_ref,
                 g2_ref, b2_ref, o_ref, x1_ref, xm2_ref, hid_ref, *, seq_len):
    rows = x_ref.shape[0]
    m = mod_ref[0]
    slabs = [slice(r, r + NORM_ROWS) for r in range(0, rows, NORM_ROWS)]
    mixes = [_dot(att_ref[sl, :], wo_ref[0:GROUP_W, :]) + _dot(rnn_ref[sl, :], wo_ref[GROUP_W:2 * GROUP_W, :])
             for sl in slabs]
    for sl, mix in zip(slabs, mixes):
        x1 = _layer_norm(DEEPNORM_ALPHA * x_ref[sl, :] + m[2:3, :] * mix, g1_ref[...], b1_ref[...])
        x1_ref[sl, :] = x1
        xm2_ref[sl, :] = (x1 * (1.0 + m[4:5, :]) + m[3:4, :]).astype(BF16)

    sub = lax.broadcasted_iota(jnp.int32, (8, FF_TILE), 0)

    def shifted(h, shift, edge_row):
        r = pltpu.roll(h, shift, axis=0)
        parts = []
        for s in range(rows // seq_len):
            edge = s * seq_len + (edge_row // 8) * 8
            fixed = jnp.where(sub == edge_row % 8, 0.0, r[edge:edge + 8])
            parts += [r[s * seq_len:edge], fixed, r[edge + 8:(s + 1) * seq_len]]
        return jnp.concatenate([p for p in parts if p.shape[0]], axis=0)

    def conv(h, cols):
        prev = shifted(h, 1, 0)
        nxt = shifted(h, rows - 1, seq_len - 1)
        return prev * cw_ref[0:1, cols] + h * cw_ref[1:2, cols] + nxt * cw_ref[2:3, cols] + cb_ref[:, cols]

    for j in range(D_FF // FF_TILE):
        cols_a = slice(j * FF_TILE, (j + 1) * FF_TILE)
        cols_u = slice(D_FF + j * FF_TILE, D_FF + (j + 1) * FF_TILE)
        a = conv(_dot(xm2_ref[...], wu_ref[:, cols_a]), cols_a)
        u = conv(_dot(xm2_ref[...], wu_ref[:, cols_u]), cols_u)
        hid_ref[:, cols_a] = (_silu(a) * u).astype(BF16)
    ffns = [_dot(hid_ref[sl, :], wd_ref[...]) for sl in slabs]
    for sl, ffn in zip(slabs, ffns):
        y = DEEPNORM_ALPHA * x1_ref[sl, :] + m[5:6, :] * ffn
        o_ref[sl, :] = _layer_norm(y, g2_ref[...], b2_ref[...])


def _tail_call(att, rnn, x2d, mod3, w_out, ln1_g, ln1_b, w_up, conv_w, conv_b, w_down, ln2_g, ln2_b,
               *, latent, seq_len):
    t = x2d.shape[0]
    rb = ROWS_FFN
    assert rb % seq_len == 0

    def mod_idx(i):
        return ((1 + i * rb // seq_len) if latent else 0, 0, 0)

    row_spec = lambda w: pl.BlockSpec((rb, w), lambda i: (i, 0))
    resident = lambda a: pl.BlockSpec(a.shape, lambda i: (0, 0), pipeline_mode=pl.Buffered(1))
    return pl.pallas_call(
        functools.partial(_tail_kernel, seq_len=seq_len),
        grid=(t // rb,),
        in_specs=[row_spec(GROUP_W), row_spec(GROUP_W), row_spec(D_MODEL),
                  pl.BlockSpec((1, N_MOD, D_MODEL), mod_idx),
                  resident(w_out), resident(ln1_g), resident(ln1_b),
                  resident(w_up), resident(conv_w), resident(conv_b), resident(w_down),
                  resident(ln2_g), resident(ln2_b)],
        out_specs=row_spec(D_MODEL),
        out_shape=jax.ShapeDtypeStruct((t, D_MODEL), F32),
        scratch_shapes=[pltpu.VMEM((rb, D_MODEL), F32), pltpu.VMEM((rb, D_MODEL), BF16),
                        pltpu.VMEM((rb, D_FF), BF16)],
        compiler_params=_params(("arbitrary",)),
        name="tail_lat" if latent else "tail_ctx",
    )(att, rnn, x2d, mod3, w_out, ln1_g, ln1_b, w_up, conv_w, conv_b, w_down, ln2_g, ln2_b)


def _rope_tables(seq_len):
    quarter = QK_DIM // 4
    freqs = 1.0 / (ROPE_BASE ** (np.arange(quarter, dtype=np.float64) / quarter))
    t = np.arange(seq_len)
    ang_r = (t // GRID_W)[:, None] * freqs
    ang_c = (t % GRID_W)[:, None] * freqs
    zeros = np.zeros_like(ang_r)

    def tile(parts):
        return jnp.asarray(np.tile(np.concatenate(parts, axis=-1), (1, GROUP_W // QK_DIM)).astype(np.float32))

    cos = tile([np.cos(ang_r), np.cos(ang_r), np.cos(ang_c), np.cos(ang_c)])
    sin_a = tile([-np.sin(ang_r), zeros, -np.sin(ang_c), zeros])
    sin_b = tile([zeros, np.sin(ang_r), zeros, np.sin(ang_c)])
    return cos, sin_a, sin_b


def kernel(x_prompt, x_sample, cache_k, cache_v, state_rnn, c, c_ctx, w_ada, b_ada, w_in, lambda_q1, lambda_k1, lambda_q2, lambda_k2, lb_fwd_logits, lb_bwd_logits, att_norm_g, rnn_norm_g, w_out, ln1_g, ln1_b, w_up, conv_w, conv_b, w_down, ln2_g, ln2_b):
    assert w_ada.shape[0] == DEPTH
    bp, lp, d = x_prompt.shape
    bs, ls, _ = x_sample.shape
    past = cache_k.shape[2]

    cond = jnp.zeros((16, d), F32).at[0].set(c_ctx).at[1:1 + bs].set(c)
    mod, w_in_b = _mod_call(cond, w_ada[0], b_ada, w_in[0])
    mod3 = mod.reshape(16, N_MOD, d)

    lams = (lambda_q1, lambda_k1, lambda_q2, lambda_k2)
    masks = jnp.asarray(_hgrn_masks())
    xp2d = x_prompt.reshape(bp * lp, d)
    xs2d = x_sample.reshape(bs * ls, d)

    ctx_in = _inproj_call(xp2d, mod3, w_in_b, None, latent=False, seq_len=lp)
    lat_in = _inproj_call(xs2d, mod3, w_in_b, _rope_tables(ls), latent=True, seq_len=ls)
    k_raw, v_raw = ctx_in[8:]
    cache = (jnp.transpose(cache_k, (0, 1, 3, 4, 5, 2)).reshape(bs, GROUP_W, past),
             cache_v.reshape(bs, past * N_HEADS, HEAD_DIM))

    def heads(proj, b, l):
        return tuple(o.reshape(b, l, GROUP_W) for o in proj[:8])

    q, k, v, rq, xf, xb, ri, rg = heads(lat_in, bs, ls)
    att_s, w_out_b, w_up_b, w_down_b = _attn_call(q, k, v, cache, lams, att_norm_g,
                                                  casts=(w_out[0], w_up[0], w_down[0]))
    rnn_s, = _hgrn_call(rq, ri, xf, xb, rg, lb_fwd_logits, lb_bwd_logits, rnn_norm_g, masks,
                        state_rnn.reshape(bs, 2, N_HEADS, HEAD_DIM, HEAD_DIM))
    q, k, v, rq, xf, xb, ri, rg = heads(ctx_in, bp, lp)
    att_p, = _attn_call(q, k, v, None, lams, att_norm_g)
    rnn_p, s_new = _hgrn_call(rq, ri, xf, xb, rg, lb_fwd_logits, lb_bwd_logits, rnn_norm_g, masks, None)

    def tail(att, rnn, x2d, b, l, latent):
        y = _tail_call(att.reshape(b * l, GROUP_W), rnn.reshape(b * l, GROUP_W), x2d, mod3, w_out_b, ln1_g,
                       ln1_b, w_up_b, conv_w[0], conv_b, w_down_b, ln2_g, ln2_b, latent=latent, seq_len=l)
        return y.reshape(b, l, d)

    y_p = tail(att_p, rnn_p, xp2d, bp, lp, False)
    y_s = tail(att_s, rnn_s, xs2d, bs, ls, True)

    new_cache_k = k_raw.reshape(bp, DEPTH, lp, N_HEADS, 2, QK_DIM)
    new_cache_v = v_raw.reshape(bp, DEPTH, lp, N_HEADS, HEAD_DIM)
    new_state = s_new.reshape(bp, DEPTH, 2, N_HEADS, HEAD_DIM, HEAD_DIM)
    return (y_p, y_s, new_cache_k, new_cache_v, new_state)
```

```python
import functools
import math

import jax
import jax.numpy as jnp
import numpy as np
from jax import lax
from jax.experimental import pallas as pl
from jax.experimental.pallas import tpu as pltpu

D_MODEL = 1024
GRID_W = 64
N_HEADS = 4
HEAD_DIM = 128
QK_DIM = 64
GROUP_W = 512
N_GROUPS = 8
D_FF = 2816
N_MOD = 6
ROPE_BASE = 10000.0
DEPTH = 1
DEEPNORM_ALPHA = (2.0 * DEPTH) ** 0.25
NORM_EPS = 1e-5
LAM_INIT = 0.8 - 0.6 * math.exp(-0.3 * 0)
LOG2_E = math.log2(math.e)

V7X_VMEM_BYTES = 64 * 1024 * 1024
VMEM_LIMIT = V7X_VMEM_BYTES * 15 // 16

MOD_TILE = 384
ROWS_PROJ = 512
ROWS_FFN = 1024
FF_TILE = 256
NORM_ROWS = 256
Q_ROWS = 512
HGRN_ROWS = 256
HGRN_LEVELS = int(math.log2(HGRN_ROWS))

F32 = jnp.float32
BF16 = jnp.bfloat16


def _params(semantics):
    return pltpu.CompilerParams(dimension_semantics=semantics, vmem_limit_bytes=VMEM_LIMIT)


def _dot(a, b):
    return jnp.dot(a, b, preferred_element_type=F32)


def _dot_nt(a, b):
    return lax.dot_general(a, b, (((1,), (1,)), ((), ())), preferred_element_type=F32)


def _dot_tn(a, b):
    return lax.dot_general(a, b, (((0,), (0,)), ((), ())), preferred_element_type=F32)


def _silu(x):
    return x * jax.nn.sigmoid(x)


def _layer_norm(y, g, b):
    mu = jnp.mean(y, axis=-1, keepdims=True)
    d = y - mu
    var = jnp.mean(d * d, axis=-1, keepdims=True)
    return d * lax.rsqrt(var + NORM_EPS) * g + b


def _mod_kernel(c_ref, w_ref, b_ref, win_ref, o_ref, win_out_ref):
    s = _silu(c_ref[...]).astype(BF16)
    o_ref[...] = _dot(s, w_ref[...].astype(BF16)) + b_ref[...]
    win_out_ref[...] = win_ref[...].astype(BF16)


def _mod_call(cond, w_ada, b_ada, w_in):
    n, d = cond.shape
    cols = w_ada.shape[1]
    tile = MOD_TILE
    n_steps = cols // tile
    slab = pl.BlockSpec((w_in.shape[0] // n_steps, w_in.shape[1]), lambda j: (j, 0))
    return pl.pallas_call(
        _mod_kernel,
        grid=(n_steps,),
        in_specs=[pl.BlockSpec((n, d), lambda j: (0, 0)),
                  pl.BlockSpec((d, tile), lambda j: (0, j)),
                  pl.BlockSpec((1, tile), lambda j: (0, j)),
                  slab],
        out_specs=[pl.BlockSpec((n, tile), lambda j: (0, j)), slab],
        out_shape=[jax.ShapeDtypeStruct((n, cols), F32), jax.ShapeDtypeStruct(w_in.shape, BF16)],
        compiler_params=_params(("arbitrary",)),
        name="mod",
    )(cond, w_ada, b_ada, w_in)


def _rope(x, cos_ref, sin_a_ref, sin_b_ref):
    parts = []
    for c in range(GROUP_W // 128):
        sl = slice(c * 128, (c + 1) * 128)
        xc = x[:, sl]
        parts.append(xc * cos_ref[:, sl]
                     + pltpu.roll(xc, 128 - 16, axis=1) * sin_a_ref[:, sl]
                     + pltpu.roll(xc, 16, axis=1) * sin_b_ref[:, sl])
    return jnp.concatenate(parts, axis=1)


def _inproj_kernel(*refs, latent):
    if latent:
        (x_ref, mod_ref, w_ref, cos_ref, sin_a_ref, sin_b_ref,
         q_ref, k_ref, v_ref, rq_ref, xf_ref, xb_ref, ri_ref, rg_ref) = refs
    else:
        (x_ref, mod_ref, w_ref,
         q_ref, k_ref, v_ref, rq_ref, xf_ref, xb_ref, ri_ref, rg_ref, kraw_ref, vraw_ref) = refs
    m = mod_ref[0]
    xm = (x_ref[...] * (1.0 + m[1:2, :]) + m[0:1, :]).astype(BF16)

    def proj(g):
        return _dot(xm, w_ref[:, g * GROUP_W:(g + 1) * GROUP_W])

    aq = proj(0)
    ak = proj(1)
    av = proj(2)
    if latent:
        aq = _rope(aq, cos_ref, sin_a_ref, sin_b_ref)
        ak = _rope(ak, cos_ref, sin_a_ref, sin_b_ref)
    else:
        rows = ak.shape[0]
        for j in range(2 * N_HEADS):
            kraw_ref[pl.ds(j, rows, stride=2 * N_HEADS), :] = ak[:, j * QK_DIM:(j + 1) * QK_DIM]
        for h in range(N_HEADS):
            vraw_ref[pl.ds(h, rows, stride=N_HEADS), :] = av[:, h * HEAD_DIM:(h + 1) * HEAD_DIM]
    q_ref[...] = (aq * (QK_DIM ** -0.5 * LOG2_E)).astype(BF16)
    k_ref[...] = ak.astype(BF16)
    v_ref[...] = av.astype(BF16)
    rq_ref[...] = proj(3)
    xf_ref[...] = proj(4)
    xb_ref[...] = proj(5)
    ri_ref[...] = proj(6).astype(BF16)
    rg_ref[...] = _silu(proj(7))


def _inproj_call(x2d, mod3, w_in, rope_tabs, *, latent, seq_len):
    t = x2d.shape[0]
    rb = ROWS_PROJ
    blocks_per_seq = seq_len // rb if latent else 1

    def mod_idx(i):
        return ((1 + i // blocks_per_seq) if latent else 0, 0, 0)

    row_spec = lambda w: pl.BlockSpec((rb, w), lambda i: (i, 0))
    in_specs = [row_spec(D_MODEL),
                pl.BlockSpec((1, N_MOD, D_MODEL), mod_idx),
                pl.BlockSpec(w_in.shape, lambda i: (0, 0))]
    args = [x2d, mod3, w_in]
    if latent:
        in_specs += [pl.BlockSpec((rb, GROUP_W), lambda i: (i % blocks_per_seq, 0))] * 3
        args += list(rope_tabs)
    dts = [BF16, BF16, BF16, F32, F32, F32, BF16, F32]
    out_specs = [row_spec(GROUP_W) for _ in dts]
    out_shape = [jax.ShapeDtypeStruct((t, GROUP_W), dt) for dt in dts]
    if not latent:
        out_specs += [pl.BlockSpec((rb * 2 * N_HEADS, QK_DIM), lambda i: (i, 0)),
                      pl.BlockSpec((rb * N_HEADS, HEAD_DIM), lambda i: (i, 0))]
        out_shape += [jax.ShapeDtypeStruct((t * 2 * N_HEADS, QK_DIM), F32),
                      jax.ShapeDtypeStruct((t * N_HEADS, HEAD_DIM), F32)]
    return pl.pallas_call(
        functools.partial(_inproj_kernel, latent=latent),
        grid=(t // rb,),
        in_specs=in_specs,
        out_specs=out_specs,
        out_shape=out_shape,
        compiler_params=_params(("arbitrary",)),
        name="inproj_lat" if latent else "inproj_ctx",
    )(*args)


def _attn_kernel(*refs, has_cache, n_casts):
    if has_cache:
        (q_ref, k_ref, v_ref, ck_ref, cv_ref, lq1, lk1, lq2, lk2, g_ref) = refs[:10]
        o_ref = refs[10 + n_casts]
        for src, dst in zip(refs[10:10 + n_casts], refs[11 + n_casts:]):
            dst[...] = src[...].astype(BF16)
    else:
        (q_ref, k_ref, v_ref, lq1, lk1, lq2, lk2, g_ref, o_ref) = refs
    lam = (jnp.exp(jnp.sum(lq1[...] * lk1[...], axis=-1, keepdims=True))
           - jnp.exp(jnp.sum(lq2[...] * lk2[...], axis=-1, keepdims=True)) + LAM_INIT)
    qb = q_ref.shape[1]
    lane = lax.broadcasted_iota(jnp.int32, (qb, HEAD_DIM), 1)
    first_map = lane < QK_DIM
    zero = jnp.zeros((), BF16)

    def scores(h):
        sl = slice(h * HEAD_DIM, (h + 1) * HEAD_DIM)
        qh = q_ref[0, :, sl]
        qq = jnp.concatenate([jnp.where(first_map, qh, zero), jnp.where(first_map, zero, qh)], axis=0)
        s_n = _dot_nt(k_ref[0, :, sl], qq)
        s_c = None
        if has_cache:
            s_c = _dot_nt(ck_ref[0, sl, :].astype(BF16).T, qq)
        return s_n, s_c

    ahead = 3
    pending = [scores(h) for h in range(ahead)]
    for h in range(N_HEADS):
        sl = slice(h * HEAD_DIM, (h + 1) * HEAD_DIM)
        s_n, s_c = pending.pop(0)
        if h + ahead < N_HEADS:
            pending.append(scores(h + ahead))
        mx = jnp.max(s_n, axis=0, keepdims=True)
        if has_cache:
            mx = jnp.maximum(mx, jnp.max(s_c, axis=0, keepdims=True))
        e_n = jnp.exp2(s_n - mx)
        den = jnp.sum(e_n, axis=0, keepdims=True)
        ev = _dot_tn(v_ref[0, :, sl], e_n.astype(BF16))
        if has_cache:
            e_c = jnp.exp2(s_c - mx)
            den = den + jnp.sum(e_c, axis=0, keepdims=True)
            vc = cv_ref[0, pl.ds(h, e_c.shape[0], stride=N_HEADS), :]
            ev = ev + _dot_tn(vc.astype(BF16), e_c.astype(BF16))
        inv = 1.0 / den
        o = ev[:, :qb] * inv[:, :qb] - ev[:, qb:] * (inv[:, qb:] * lam)
        o = o * lax.rsqrt(jnp.mean(o * o, axis=0, keepdims=True) + NORM_EPS)
        o_ref[0, :, sl] = (o.T * g_ref[:, sl] * (1.0 - LAM_INIT)).astype(BF16)


def _attn_call(q, k, v, cache, lams, att_g, casts=()):
    b, l, _ = q.shape
    qb = min(Q_ROWS, l)
    nq = l // qb
    slab = lambda a: pl.BlockSpec((a.shape[0] // (b * nq), a.shape[1]), lambda i, j: (i * nq + j, 0))
    full = lambda a: pl.BlockSpec((1,) + a.shape[1:], lambda i, j: (i,) + (0,) * (a.ndim - 1))
    in_specs = [pl.BlockSpec((1, qb, GROUP_W), lambda i, j: (i, j, 0)), full(k), full(v)]
    args = [q, k, v]
    if cache is not None:
        in_specs += [full(cache[0]), full(cache[1])]
        args += list(cache)
    in_specs += [pl.BlockSpec((1, QK_DIM), lambda i, j: (0, 0))] * 4
    in_specs += [pl.BlockSpec((1, GROUP_W), lambda i, j: (0, 0))]
    args += list(lams) + [att_g] + list(casts)
    in_specs += [slab(a) for a in casts]
    return pl.pallas_call(
        functools.partial(_attn_kernel, has_cache=cache is not None, n_casts=len(casts)),
        grid=(b, nq),
        in_specs=in_specs,
        out_specs=[pl.BlockSpec((1, qb, GROUP_W), lambda i, j: (i, j, 0))] + [slab(a) for a in casts],
        out_shape=[jax.ShapeDtypeStruct((b, l, GROUP_W), BF16)] + [jax.ShapeDtypeStruct(a.shape, BF16) for a in casts],
        compiler_params=_params(("arbitrary", "arbitrary")),
        name="attn_lat" if cache is not None else "attn_ctx",
    )(*args)


HGRN_DIAG = 128
HGRN_MASKED_LEVELS = 5
HGRN_VREG_LEVELS = 3


def _hgrn_masks():
    r = np.arange(HGRN_DIAG)[:, None]
    c = np.arange(HGRN_DIAG)[None, :]
    levels = []
    for lv in range(HGRN_MASKED_LEVELS):
        same = (r >> (lv + 1)) == (c >> (lv + 1))
        levels.append(same & (((r >> lv) & 1) == 1) & (((c >> lv) & 1) == 0))
    fwd = [r == c] + levels
    bwd = [m.T for m in levels]
    return np.stack(fwd + bwd).astype(np.float32)


def _hgrn_block(q, k, g, v, st, mask_ref, sub, *, reverse):
    n = HGRN_ROWS
    n_diag = n // HGRN_DIAG

    def diag_scores(qe, ke, m):
        return [_dot_nt(qe[i * HGRN_DIAG:(i + 1) * HGRN_DIAG], ke[i * HGRN_DIAG:(i + 1) * HGRN_DIAG]).astype(BF16) * m
                for i in range(n_diag)]

    def level_mask(lv):
        return mask_ref[(1 + HGRN_MASKED_LEVELS + lv) if reverse else (1 + lv)]

    g3 = g.reshape(n // 8, 8, HEAD_DIM)
    zero3 = jnp.zeros_like(g3)
    pre3, suf3 = (zero3, g3) if reverse else (g3, zero3)
    tot3 = g3
    qb, kb = q.astype(BF16), k.astype(BF16)
    acc = diag_scores(qb, kb, mask_ref[0])
    for lv in range(HGRN_VREG_LEVELS):
        b = 1 << lv
        upper = ((sub >> lv) & 1) == 1
        e = jnp.exp2(jnp.where(upper, pre3, suf3)).reshape(n, HEAD_DIM)
        eb = e.astype(BF16)
        part = diag_scores(qb * eb, kb * eb, level_mask(lv))
        acc = [a + p for a, p in zip(acc, part)]
        sib = jnp.where(upper, pltpu.roll(tot3, b, axis=1), pltpu.roll(tot3, 8 - b, axis=1))
        pre3 = pre3 + jnp.where(upper, sib, 0.0)
        suf3 = suf3 + jnp.where(upper, 0.0, sib)
        tot3 = tot3 + sib

    pieces = lambda x3: [x3[i] for i in range(n // 8)]
    pre8, suf8, tot8 = pieces(pre3), pieces(suf3), pieces(tot3)
    rows = lambda xs, lo, hi: jnp.concatenate(xs[lo // 8:hi // 8], axis=0) if hi - lo > 8 else xs[lo // 8]
    big_scores = {}
    for lv in range(HGRN_VREG_LEVELS, HGRN_LEVELS):
        b = 1 << lv
        pb = b // 8
        if lv < HGRN_MASKED_LEVELS:
            u = jnp.concatenate([(pre8 if (i // pb) % 2 else suf8)[i] for i in range(n // 8)], axis=0)
            e = jnp.exp2(u)
            eb = e.astype(BF16)
            part = diag_scores(qb * eb, kb * eb, level_mask(lv))
            acc = [a + p for a, p in zip(acc, part)]
        else:
            for j in range(n // (2 * b)):
                lo, mid, hi = 2 * b * j, 2 * b * j + b, 2 * b * (j + 1)
                e_lo = jnp.exp2(rows(suf8, lo, mid))
                e_up = jnp.exp2(rows(pre8, mid, hi))
                x_lo = (qb if reverse else kb)[lo:mid] * e_lo.astype(BF16)
                x_up = (kb if reverse else qb)[mid:hi] * e_up.astype(BF16)
                big_scores[lv, j] = _dot_nt(x_lo, x_up) if reverse else _dot_nt(x_up, x_lo)
        for j in range(n // (2 * b)):
            lo_p, mid_p, hi_p = 2 * pb * j, 2 * pb * j + pb, 2 * pb * (j + 1)
            t_lo, t_up = tot8[lo_p], tot8[mid_p]
            t_new = t_lo + t_up
            for i in range(lo_p, mid_p):
                suf8[i] = suf8[i] + t_up
                tot8[i] = t_new
            for i in range(mid_p, hi_p):
                pre8[i] = pre8[i] + t_lo
                tot8[i] = t_new
    pre = jnp.concatenate(pre8, axis=0)
    suf = jnp.concatenate(suf8, axis=0)
    q_dec, k_dec = (suf, pre) if reverse else (pre, suf)
    q_in = qb * jnp.exp2(q_dec).astype(BF16)
    k_out = kb * jnp.exp2(k_dec).astype(BF16)
    decay = jnp.exp2(tot8[0][0:1, :])

    o_inter = _dot_nt(q_in, st.astype(BF16))
    ds = _dot_tn(v, k_out)
    o_diag = [_dot(acc[i].astype(BF16), v[i * HGRN_DIAG:(i + 1) * HGRN_DIAG]) for i in range(n_diag)]
    o_big = {}
    for (lv, j), a in big_scores.items():
        lo, mid, hi = (2 * j) << lv, (2 * j + 1) << lv, (2 * j + 2) << lv
        o_big[lv, j] = _dot(a.astype(BF16), v[mid:hi] if reverse else v[lo:mid])
    st_new = st * decay + ds
    o = jnp.concatenate(o_diag, axis=0) + o_inter
    for lv in range(HGRN_MASKED_LEVELS, HGRN_LEVELS):
        contrib = []
        for j in range(n >> (lv + 1)):
            zeros = jnp.zeros((1 << lv, HEAD_DIM), F32)
            contrib += [o_big[lv, j], zeros] if reverse else [zeros, o_big[lv, j]]
        o = o + jnp.concatenate(contrib, axis=0)
    return o, st_new


def _hgrn_kernel(*refs, n_blocks, heads, has_state):
    if has_state:
        (q_ref, v_ref, xf_ref, xb_ref, gate_ref, lbf_ref, lbb_ref, g_ref, mask_ref, s0_ref,
         o_ref, acc_ref) = refs
    else:
        (q_ref, v_ref, xf_ref, xb_ref, gate_ref, lbf_ref, lbb_ref, g_ref, mask_ref,
         o_ref, sout_ref, acc_ref) = refs
    n = HGRN_ROWS
    sub = lax.broadcasted_iota(jnp.int32, (n // 8, 8, HEAD_DIM), 1)

    def lower_bound(ref, lanes):
        l0, l1 = ref[0:1, lanes], ref[1:2, lanes]
        mx = jnp.maximum(l0, l1)
        e0, e1 = jnp.exp(l0 - mx), jnp.exp(l1 - mx)
        return e0 / (e0 + e1)

    x_refs = (xf_ref, xb_ref)
    for h in range(heads):
        lanes = slice(h * HEAD_DIM, (h + 1) * HEAD_DIM)
        lbs = [lower_bound(lbf_ref, lanes), lower_bound(lbb_ref, lanes)]
        sts = [s0_ref[0, d, h].T if has_state else jnp.zeros((HEAD_DIM, HEAD_DIM), F32) for d in range(2)]
        for j in range(n_blocks):
            for d in range(2):
                blk = (n_blocks - 1 - j) if d else j
                rows = slice(blk * n, (blk + 1) * n)
                f = lbs[d] + (1.0 - lbs[d]) * jax.nn.sigmoid(x_refs[d][0, rows, lanes])
                o, sts[d] = _hgrn_block(q_ref[0, rows, lanes], 1.0 - f, jnp.log2(f), v_ref[0, rows, lanes],
                                        sts[d], mask_ref, sub, reverse=bool(d))
                acc_ref[d, rows, lanes] = o
        if not has_state:
            for d in range(2):
                sout_ref[0, d, h] = sts[d].T
    for h in range(heads):
        lanes = slice(h * HEAD_DIM, (h + 1) * HEAD_DIM)
        o = acc_ref[0, :, lanes] + acc_ref[1, :, lanes]
        o = o * lax.rsqrt(jnp.mean(o * o, axis=-1, keepdims=True) + NORM_EPS)
        o_ref[0, :, lanes] = (o * g_ref[:, lanes] * gate_ref[0, :, lanes]).astype(BF16)


def _hgrn_call(rq, ri, xf, xb, gate, lbf, lbb, rnn_g, masks, s0):
    b, l, _ = rq.shape
    n_blocks = l // HGRN_ROWS
    heads = N_HEADS
    w = heads * HEAD_DIM
    head = lambda: pl.BlockSpec((1, l, w), lambda i, h: (i, 0, h))
    per_head = lambda rows: pl.BlockSpec((rows, w), lambda i, h: (0, h))
    state_spec = pl.BlockSpec((1, 2, heads, HEAD_DIM, HEAD_DIM), lambda i, h: (i, 0, h, 0, 0))
    in_specs = [head(), head(), head(), head(), head(), per_head(2), per_head(2), per_head(1),
                pl.BlockSpec(masks.shape, lambda i, h: (0, 0, 0))]
    args = [rq, ri, xf, xb, gate, lbf, lbb, rnn_g, masks]
    out_specs = [head()]
    out_shape = [jax.ShapeDtypeStruct((b, l, GROUP_W), BF16)]
    if s0 is not None:
        in_specs.append(state_spec)
        args.append(s0)
    else:
        out_specs.append(state_spec)
        out_shape.append(jax.ShapeDtypeStruct((b, 2, N_HEADS, HEAD_DIM, HEAD_DIM), F32))
    return pl.pallas_call(
        functools.partial(_hgrn_kernel, n_blocks=n_blocks, heads=heads, has_state=s0 is not None),
        grid=(b, N_HEADS // heads),
        in_specs=in_specs,
        out_specs=out_specs,
        out_shape=out_shape,
        scratch_shapes=[pltpu.VMEM((2, l, w), F32)],
        compiler_params=_params(("arbitrary", "arbitrary")),
        name="hgrn_lat" if s0 is not None else "hgrn_ctx",
    )(*args)


def _tail_kernel(att_ref, rnn_ref, x_ref, mod_ref, wo_ref, g1_ref, b1_ref, wu_ref, cw_ref, cb_ref, wd_ref,
                 g2_ref, b2_ref, o_ref, x1_ref, xm2_ref, hid_ref, *, seq_len):
    rows = x_ref.shape[0]
    m = mod_ref[0]
    slabs = [slice(r, r + NORM_ROWS) for r in range(0, rows, NORM_ROWS)]
    mixes = [_dot(att_ref[sl, :], wo_ref[0:GROUP_W, :]) + _dot(rnn_ref[sl, :], wo_ref[GROUP_W:2 * GROUP_W, :])
             for sl in slabs]
    for sl, mix in zip(slabs, mixes):
        x1 = _layer_norm(DEEPNORM_ALPHA * x_ref[sl, :] + m[2:3, :] * mix, g1_ref[...], b1_ref[...])
        x1_ref[sl, :] = x1
        xm2_ref[sl, :] = (x1 * (1.0 + m[4:5, :]) + m[3:4, :]).astype(BF16)

    sub = lax.broadcasted_iota(jnp.int32, (8, FF_TILE), 0)

    def shifted(h, shift, edge_row):
        r = pltpu.roll(h, shift, axis=0)
        parts = []
        for s in range(rows // seq_len):
            edge = s * seq_len + (edge_row // 8) * 8
            fixed = jnp.where(sub == edge_row % 8, 0.0, r[edge:edge + 8])
            parts += [r[s * seq_len:edge], fixed, r[edge + 8:(s + 1) * seq_len]]
        return jnp.concatenate([p for p in parts if p.shape[0]], axis=0)

    def conv(h, cols):
        prev = shifted(h, 1, 0)
        nxt = shifted(h, rows - 1, seq_len - 1)
        return prev * cw_ref[0:1, cols] + h * cw_ref[1:2, cols] + nxt * cw_ref[2:3, cols] + cb_ref[:, cols]

    for j in range(D_FF // FF_TILE):
        cols_a = slice(j * FF_TILE, (j + 1) * FF_TILE)
        cols_u = slice(D_FF + j * FF_TILE, D_FF + (j + 1) * FF_TILE)
        a = conv(_dot(xm2_ref[...], wu_ref[:, cols_a]), cols_a)
        u = conv(_dot(xm2_ref[...], wu_ref[:, cols_u]), cols_u)
        hid_ref[:, cols_a] = (_silu(a) * u).astype(BF16)
    ffns = [_dot(hid_ref[sl, :], wd_ref[...]) for sl in slabs]
    for sl, ffn in zip(slabs, ffns):
        y = DEEPNORM_ALPHA * x1_ref[sl, :] + m[5:6, :] * ffn
        o_ref[sl, :] = _layer_norm(y, g2_ref[...], b2_ref[...])


def _tail_call(att, rnn, x2d, mod3, w_out, ln1_g, ln1_b, w_up, conv_w, conv_b, w_down, ln2_g, ln2_b,
               *, latent, seq_len):
    t = x2d.shape[0]
    rb = ROWS_FFN
    assert rb % seq_len == 0

    def mod_idx(i):
        return ((1 + i * rb // seq_len) if latent else 0, 0, 0)

    row_spec = lambda w: pl.BlockSpec((rb, w), lambda i: (i, 0))
    resident = lambda a: pl.BlockSpec(a.shape, lambda i: (0, 0), pipeline_mode=pl.Buffered(1))
    return pl.pallas_call(
        functools.partial(_tail_kernel, seq_len=seq_len),
        grid=(t // rb,),
        in_specs=[row_spec(GROUP_W), row_spec(GROUP_W), row_spec(D_MODEL),
                  pl.BlockSpec((1, N_MOD, D_MODEL), mod_idx),
                  resident(w_out), resident(ln1_g), resident(ln1_b),
                  resident(w_up), resident(conv_w), resident(conv_b), resident(w_down),
                  resident(ln2_g), resident(ln2_b)],
        out_specs=row_spec(D_MODEL),
        out_shape=jax.ShapeDtypeStruct((t, D_MODEL), F32),
        scratch_shapes=[pltpu.VMEM((rb, D_MODEL), F32), pltpu.VMEM((rb, D_MODEL), BF16),
                        pltpu.VMEM((rb, D_FF), BF16)],
        compiler_params=_params(("arbitrary",)),
        name="tail_lat" if latent else "tail_ctx",
    )(att, rnn, x2d, mod3, w_out, ln1_g, ln1_b, w_up, conv_w, conv_b, w_down, ln2_g, ln2_b)


def _rope_tables(seq_len):
    quarter = QK_DIM // 4
    freqs = 1.0 / (ROPE_BASE ** (np.arange(quarter, dtype=np.float64) / quarter))
    t = np.arange(seq_len)
    ang_r = (t // GRID_W)[:, None] * freqs
    ang_c = (t % GRID_W)[:, None] * freqs
    zeros = np.zeros_like(ang_r)

    def tile(parts):
        return jnp.asarray(np.tile(np.concatenate(parts, axis=-1), (1, GROUP_W // QK_DIM)).astype(np.float32))

    cos = tile([np.cos(ang_r), np.cos(ang_r), np.cos(ang_c), np.cos(ang_c)])
    sin_a = tile([-np.sin(ang_r), zeros, -np.sin(ang_c), zeros])
    sin_b = tile([zeros, np.sin(ang_r), zeros, np.sin(ang_c)])
    return cos, sin_a, sin_b


def kernel(x_prompt, x_sample, cache_k, cache_v, state_rnn, c, c_ctx, w_ada, b_ada, w_in, lambda_q1, lambda_k1, lambda_q2, lambda_k2, lb_fwd_logits, lb_bwd_logits, att_norm_g, rnn_norm_g, w_out, ln1_g, ln1_b, w_up, conv_w, conv_b, w_down, ln2_g, ln2_b):
    assert w_ada.shape[0] == DEPTH
    bp, lp, d = x_prompt.shape
    bs, ls, _ = x_sample.shape
    past = cache_k.shape[2]

    cond = jnp.zeros((16, d), F32).at[0].set(c_ctx).at[1:1 + bs].set(c)
    mod, w_in_b = _mod_call(cond, w_ada[0], b_ada, w_in[0])
    mod3 = mod.reshape(16, N_MOD, d)

    lams = (lambda_q1, lambda_k1, lambda_q2, lambda_k2)
    masks = jnp.asarray(_hgrn_masks(), dtype=BF16)
    xp2d = x_prompt.reshape(bp * lp, d)
    xs2d = x_sample.reshape(bs * ls, d)

    ctx_in = _inproj_call(xp2d, mod3, w_in_b, None, latent=False, seq_len=lp)
    lat_in = _inproj_call(xs2d, mod3, w_in_b, _rope_tables(ls), latent=True, seq_len=ls)
    k_raw, v_raw = ctx_in[8:]
    cache = (jnp.transpose(cache_k, (0, 1, 3, 4, 5, 2)).reshape(bs, GROUP_W, past),
             cache_v.reshape(bs, past * N_HEADS, HEAD_DIM))

    def heads(proj, b, l):
        return tuple(o.reshape(b, l, GROUP_W) for o in proj[:8])

    q, k, v, rq, xf, xb, ri, rg = heads(lat_in, bs, ls)
    att_s, w_out_b, w_up_b, w_down_b = _attn_call(q, k, v, cache, lams, att_norm_g,
                                                  casts=(w_out[0], w_up[0], w_down[0]))
    rnn_s, = _hgrn_call(rq, ri, xf, xb, rg, lb_fwd_logits, lb_bwd_logits, rnn_norm_g, masks,
                        state_rnn.reshape(bs, 2, N_HEADS, HEAD_DIM, HEAD_DIM))
    q, k, v, rq, xf, xb, ri, rg = heads(ctx_in, bp, lp)
    att_p, = _attn_call(q, k, v, None, lams, att_norm_g)
    rnn_p, s_new = _hgrn_call(rq, ri, xf, xb, rg, lb_fwd_logits, lb_bwd_logits, rnn_norm_g, masks, None)

    def tail(att, rnn, x2d, b, l, latent):
        y = _tail_call(att.reshape(b * l, GROUP_W), rnn.reshape(b * l, GROUP_W), x2d, mod3, w_out_b, ln1_g,
                       ln1_b, w_up_b, conv_w[0], conv_b, w_down_b, ln2_g, ln2_b, latent=latent, seq_len=l)
        return y.reshape(b, l, d)

    y_p = tail(att_p, rnn_p, xp2d, bp, lp, False)
    y_s = tail(att_s, rnn_s, xs2d, bs, ls, True)

    new_cache_k = k_raw.reshape(bp, DEPTH, lp, N_HEADS, 2, QK_DIM)
    new_cache_v = v_raw.reshape(bp, DEPTH, lp, N_HEADS, HEAD_DIM)
    new_state = s_new.reshape(bp, DEPTH, 2, N_HEADS, HEAD_DIM, HEAD_DIM)
    return (y_p, y_s, new_cache_k, new_cache_v, new_state)
```

```python
import functools
import math

import jax
import jax.numpy as jnp
import numpy as np
from jax import lax
from jax.experimental import pallas as pl
from jax.experimental.pallas import tpu as pltpu

D_MODEL = 1024
GRID_W = 64
N_HEADS = 4
HEAD_DIM = 128
QK_DIM = 64
GROUP_W = 512
N_GROUPS = 8
D_FF = 2816
N_MOD = 6
ROPE_BASE = 10000.0
DEPTH = 1
DEEPNORM_ALPHA = (2.0 * DEPTH) ** 0.25
NORM_EPS = 1e-5
LAM_INIT = 0.8 - 0.6 * math.exp(-0.3 * 0)
LOG2_E = math.log2(math.e)

V7X_VMEM_BYTES = 64 * 1024 * 1024
VMEM_LIMIT = V7X_VMEM_BYTES * 15 // 16
COND_ROWS = 16

MOD_TILE = 384
ROWS_PROJ = 512
ROWS_FFN = 1024
FF_TILE = 256
NORM_ROWS = 256
Q_ROWS = 512
HGRN_ROWS = 256
HGRN_LEVELS = int(math.log2(HGRN_ROWS))

F32 = jnp.float32
BF16 = jnp.bfloat16


def _params(semantics):
    return pltpu.CompilerParams(dimension_semantics=semantics, vmem_limit_bytes=VMEM_LIMIT)


def _dot(a, b):
    return jnp.dot(a, b, preferred_element_type=F32)


def _dot_nt(a, b):
    return lax.dot_general(a, b, (((1,), (1,)), ((), ())), preferred_element_type=F32)


def _dot_tn(a, b):
    return lax.dot_general(a, b, (((0,), (0,)), ((), ())), preferred_element_type=F32)


def _silu(x):
    return x * jax.nn.sigmoid(x)


def _layer_norm(y, g, b):
    mu = jnp.mean(y, axis=-1, keepdims=True)
    d = y - mu
    var = jnp.mean(d * d, axis=-1, keepdims=True)
    return d * lax.rsqrt(var + NORM_EPS) * g + b


def _mod_kernel(c_ref, w_ref, b_ref, win_ref, o_ref, win_out_ref):
    s = _silu(c_ref[...]).astype(BF16)
    o_ref[...] = _dot(s, w_ref[...].astype(BF16)) + b_ref[...]
    win_out_ref[...] = win_ref[...].astype(BF16)


def _mod_call(cond, w_ada, b_ada, w_in):
    n, d = cond.shape
    cols = w_ada.shape[1]
    tile = MOD_TILE
    n_steps = cols // tile
    slab = pl.BlockSpec((w_in.shape[0] // n_steps, w_in.shape[1]), lambda j: (j, 0))
    return pl.pallas_call(
        _mod_kernel,
        grid=(n_steps,),
        in_specs=[pl.BlockSpec((n, d), lambda j: (0, 0)),
                  pl.BlockSpec((d, tile), lambda j: (0, j)),
                  pl.BlockSpec((1, tile), lambda j: (0, j)),
                  slab],
        out_specs=[pl.BlockSpec((n, tile), lambda j: (0, j)), slab],
        out_shape=[jax.ShapeDtypeStruct((n, cols), F32), jax.ShapeDtypeStruct(w_in.shape, BF16)],
        compiler_params=_params(("arbitrary",)),
        name="mod",
    )(cond, w_ada, b_ada, w_in)


def _rope(x, cos_ref, sin_a_ref, sin_b_ref):
    parts = []
    for c in range(GROUP_W // 128):
        sl = slice(c * 128, (c + 1) * 128)
        xc = x[:, sl]
        parts.append(xc * cos_ref[:, sl]
                     + pltpu.roll(xc, 128 - 16, axis=1) * sin_a_ref[:, sl]
                     + pltpu.roll(xc, 16, axis=1) * sin_b_ref[:, sl])
    return jnp.concatenate(parts, axis=1)


def _inproj_kernel(*refs, latent):
    if latent:
        (x_ref, mod_ref, w_ref, cos_ref, sin_a_ref, sin_b_ref,
         q_ref, k_ref, v_ref, rq_ref, xf_ref, xb_ref, ri_ref, rg_ref) = refs
    else:
        (x_ref, mod_ref, w_ref,
         q_ref, k_ref, v_ref, rq_ref, xf_ref, xb_ref, ri_ref, rg_ref, kraw_ref, vraw_ref) = refs
    m = mod_ref[0]
    xm = (x_ref[...] * (1.0 + m[1:2, :]) + m[0:1, :]).astype(BF16)

    def proj(g):
        return _dot(xm, w_ref[:, g * GROUP_W:(g + 1) * GROUP_W])

    aq = proj(0)
    ak = proj(1)
    av = proj(2)
    if latent:
        aq = _rope(aq, cos_ref, sin_a_ref, sin_b_ref)
        ak = _rope(ak, cos_ref, sin_a_ref, sin_b_ref)
    else:
        rows = ak.shape[0]
        for j in range(2 * N_HEADS):
            kraw_ref[pl.ds(j, rows, stride=2 * N_HEADS), :] = ak[:, j * QK_DIM:(j + 1) * QK_DIM]
        for h in range(N_HEADS):
            vraw_ref[pl.ds(h, rows, stride=N_HEADS), :] = av[:, h * HEAD_DIM:(h + 1) * HEAD_DIM]
    q_ref[...] = (aq * (QK_DIM ** -0.5 * LOG2_E)).astype(BF16)
    k_ref[...] = ak.astype(BF16)
    v_ref[...] = av.astype(BF16)
    rq_ref[...] = proj(3)
    xf_ref[...] = proj(4)
    xb_ref[...] = proj(5)
    ri_ref[...] = proj(6).astype(BF16)
    rg_ref[...] = _silu(proj(7))


def _inproj_call(x2d, mod3, w_in, rope_tabs, *, latent, seq_len):
    t = x2d.shape[0]
    rb = ROWS_PROJ
    blocks_per_seq = seq_len // rb if latent else 1

    def mod_idx(i):
        return ((1 + i // blocks_per_seq) if latent else 0, 0, 0)

    row_spec = lambda w: pl.BlockSpec((rb, w), lambda i: (i, 0))
    in_specs = [row_spec(D_MODEL),
                pl.BlockSpec((1, N_MOD, D_MODEL), mod_idx),
                pl.BlockSpec(w_in.shape, lambda i: (0, 0))]
    args = [x2d, mod3, w_in]
    if latent:
        in_specs += [pl.BlockSpec((rb, GROUP_W), lambda i: (i % blocks_per_seq, 0))] * 3
        args += list(rope_tabs)
    dts = [BF16, BF16, BF16, F32, F32, F32, BF16, F32]
    out_specs = [row_spec(GROUP_W) for _ in dts]
    out_shape = [jax.ShapeDtypeStruct((t, GROUP_W), dt) for dt in dts]
    if not latent:
        out_specs += [pl.BlockSpec((rb * 2 * N_HEADS, QK_DIM), lambda i: (i, 0)),
                      pl.BlockSpec((rb * N_HEADS, HEAD_DIM), lambda i: (i, 0))]
        out_shape += [jax.ShapeDtypeStruct((t * 2 * N_HEADS, QK_DIM), F32),
                      jax.ShapeDtypeStruct((t * N_HEADS, HEAD_DIM), F32)]
    return pl.pallas_call(
        functools.partial(_inproj_kernel, latent=latent),
        grid=(t // rb,),
        in_specs=in_specs,
        out_specs=out_specs,
        out_shape=out_shape,
        compiler_params=_params(("arbitrary",)),
        name="inproj_lat" if latent else "inproj_ctx",
    )(*args)


def _attn_kernel(*refs, has_cache, n_casts):
    if has_cache:
        (q_ref, k_ref, v_ref, ck_ref, cv_ref, lq1, lk1, lq2, lk2, g_ref) = refs[:10]
        o_ref = refs[10 + n_casts]
        for src, dst in zip(refs[10:10 + n_casts], refs[11 + n_casts:]):
            dst[...] = src[...].astype(BF16)
    else:
        (q_ref, k_ref, v_ref, lq1, lk1, lq2, lk2, g_ref, o_ref) = refs
    lam = (jnp.exp(jnp.sum(lq1[...] * lk1[...], axis=-1, keepdims=True))
           - jnp.exp(jnp.sum(lq2[...] * lk2[...], axis=-1, keepdims=True)) + LAM_INIT)
    qb = q_ref.shape[1]
    lane = lax.broadcasted_iota(jnp.int32, (qb, HEAD_DIM), 1)
    first_map = lane < QK_DIM
    zero = jnp.zeros((), BF16)

    def scores(h):
        sl = slice(h * HEAD_DIM, (h + 1) * HEAD_DIM)
        qh = q_ref[0, :, sl]
        qq = jnp.concatenate([jnp.where(first_map, qh, zero), jnp.where(first_map, zero, qh)], axis=0)
        s_n = _dot_nt(k_ref[0, :, sl], qq)
        s_c = None
        if has_cache:
            s_c = _dot_nt(ck_ref[0, sl, :].astype(BF16).T, qq)
        return s_n, s_c

    ahead = 3
    pending = [scores(h) for h in range(ahead)]
    for h in range(N_HEADS):
        sl = slice(h * HEAD_DIM, (h + 1) * HEAD_DIM)
        s_n, s_c = pending.pop(0)
        if h + ahead < N_HEADS:
            pending.append(scores(h + ahead))
        mx = jnp.max(s_n, axis=0, keepdims=True)
        if has_cache:
            mx = jnp.maximum(mx, jnp.max(s_c, axis=0, keepdims=True))
        e_n = jnp.exp2(s_n - mx)
        den = jnp.sum(e_n, axis=0, keepdims=True)
        ev = _dot_tn(v_ref[0, :, sl], e_n.astype(BF16))
        if has_cache:
            e_c = jnp.exp2(s_c - mx)
            den = den + jnp.sum(e_c, axis=0, keepdims=True)
            vc = cv_ref[0, pl.ds(h, e_c.shape[0], stride=N_HEADS), :]
            ev = ev + _dot_tn(vc.astype(BF16), e_c.astype(BF16))
        inv = 1.0 / den
        o = ev[:, :qb] * inv[:, :qb] - ev[:, qb:] * (inv[:, qb:] * lam)
        o = o * lax.rsqrt(jnp.mean(o * o, axis=0, keepdims=True) + NORM_EPS)
        o_ref[0, :, sl] = (o.T * g_ref[:, sl] * (1.0 - LAM_INIT)).astype(BF16)


def _attn_call(q, k, v, cache, lams, att_g, casts=()):
    b, l, _ = q.shape
    qb = min(Q_ROWS, l)
    nq = l // qb
    slab = lambda a: pl.BlockSpec((a.shape[0] // (b * nq), a.shape[1]), lambda i, j: (i * nq + j, 0))
    full = lambda a: pl.BlockSpec((1,) + a.shape[1:], lambda i, j: (i,) + (0,) * (a.ndim - 1))
    in_specs = [pl.BlockSpec((1, qb, GROUP_W), lambda i, j: (i, j, 0)), full(k), full(v)]
    args = [q, k, v]
    if cache is not None:
        in_specs += [full(cache[0]), full(cache[1])]
        args += list(cache)
    in_specs += [pl.BlockSpec((1, QK_DIM), lambda i, j: (0, 0))] * 4
    in_specs += [pl.BlockSpec((1, GROUP_W), lambda i, j: (0, 0))]
    args += list(lams) + [att_g] + list(casts)
    in_specs += [slab(a) for a in casts]
    return pl.pallas_call(
        functools.partial(_attn_kernel, has_cache=cache is not None, n_casts=len(casts)),
        grid=(b, nq),
        in_specs=in_specs,
        out_specs=[pl.BlockSpec((1, qb, GROUP_W), lambda i, j: (i, j, 0))] + [slab(a) for a in casts],
        out_shape=[jax.ShapeDtypeStruct((b, l, GROUP_W), BF16)] + [jax.ShapeDtypeStruct(a.shape, BF16) for a in casts],
        compiler_params=_params(("arbitrary", "arbitrary")),
        name="attn_lat" if cache is not None else "attn_ctx",
    )(*args)


HGRN_DIAG = 128
HGRN_MASKED_LEVELS = 5
HGRN_VREG_LEVELS = 3


def _hgrn_masks():
    r = np.arange(HGRN_DIAG)[:, None]
    c = np.arange(HGRN_DIAG)[None, :]
    levels = []
    for lv in range(HGRN_MASKED_LEVELS):
        same = (r >> (lv + 1)) == (c >> (lv + 1))
        levels.append(same & (((r >> lv) & 1) == 1) & (((c >> lv) & 1) == 0))
    fwd = [r == c] + levels
    bwd = [m.T for m in levels]
    return np.stack(fwd + bwd).astype(np.float32)


def _hgrn_block(q, k, g, v, st, mask_ref, sub, *, reverse):
    n = HGRN_ROWS
    n_diag = n // HGRN_DIAG

    def diag_scores(qe, ke, m):
        return [_dot_nt(qe[i * HGRN_DIAG:(i + 1) * HGRN_DIAG], ke[i * HGRN_DIAG:(i + 1) * HGRN_DIAG]).astype(BF16) * m
                for i in range(n_diag)]

    def level_mask(lv):
        return mask_ref[(1 + HGRN_MASKED_LEVELS + lv) if reverse else (1 + lv)]

    g3 = g.reshape(n // 8, 8, HEAD_DIM)
    zero3 = jnp.zeros_like(g3)
    pre3, suf3 = (zero3, g3) if reverse else (g3, zero3)
    tot3 = g3
    qb, kb = q.astype(BF16), k.astype(BF16)
    acc = diag_scores(qb, kb, mask_ref[0])
    for lv in range(HGRN_VREG_LEVELS):
        b = 1 << lv
        upper = ((sub >> lv) & 1) == 1
        e = jnp.exp2(jnp.where(upper, pre3, suf3)).reshape(n, HEAD_DIM)
        eb = e.astype(BF16)
        part = diag_scores(qb * eb, kb * eb, level_mask(lv))
        acc = [a + p for a, p in zip(acc, part)]
        sib = jnp.where(upper, pltpu.roll(tot3, b, axis=1), pltpu.roll(tot3, 8 - b, axis=1))
        pre3 = pre3 + jnp.where(upper, sib, 0.0)
        suf3 = suf3 + jnp.where(upper, 0.0, sib)
        tot3 = tot3 + sib

    pieces = lambda x3: [x3[i] for i in range(n // 8)]
    pre8, suf8, tot8 = pieces(pre3), pieces(suf3), pieces(tot3)
    rows = lambda xs, lo, hi: jnp.concatenate(xs[lo // 8:hi // 8], axis=0) if hi - lo > 8 else xs[lo // 8]
    big_scores = {}
    for lv in range(HGRN_VREG_LEVELS, HGRN_LEVELS):
        b = 1 << lv
        pb = b // 8
        if lv < HGRN_MASKED_LEVELS:
            u = jnp.concatenate([(pre8 if (i // pb) % 2 else suf8)[i] for i in range(n // 8)], axis=0)
            e = jnp.exp2(u)
            eb = e.astype(BF16)
            part = diag_scores(qb * eb, kb * eb, level_mask(lv))
            acc = [a + p for a, p in zip(acc, part)]
        else:
            for j in range(n // (2 * b)):
                lo, mid, hi = 2 * b * j, 2 * b * j + b, 2 * b * (j + 1)
                e_lo = jnp.exp2(rows(suf8, lo, mid))
                e_up = jnp.exp2(rows(pre8, mid, hi))
                x_lo = (qb if reverse else kb)[lo:mid] * e_lo.astype(BF16)
                x_up = (kb if reverse else qb)[mid:hi] * e_up.astype(BF16)
                big_scores[lv, j] = _dot_nt(x_lo, x_up) if reverse else _dot_nt(x_up, x_lo)
        for j in range(n // (2 * b)):
            lo_p, mid_p, hi_p = 2 * pb * j, 2 * pb * j + pb, 2 * pb * (j + 1)
            t_lo, t_up = tot8[lo_p], tot8[mid_p]
            t_new = t_lo + t_up
            for i in range(lo_p, mid_p):
                suf8[i] = suf8[i] + t_up
                tot8[i] = t_new
            for i in range(mid_p, hi_p):
                pre8[i] = pre8[i] + t_lo
                tot8[i] = t_new
    pre = jnp.concatenate(pre8, axis=0)
    suf = jnp.concatenate(suf8, axis=0)
    q_dec, k_dec = (suf, pre) if reverse else (pre, suf)
    q_in = qb * jnp.exp2(q_dec).astype(BF16)
    k_out = kb * jnp.exp2(k_dec).astype(BF16)
    decay = jnp.exp2(tot8[0][0:1, :])

    o_inter = _dot_nt(q_in, st.astype(BF16))
    ds = _dot_tn(v, k_out)
    o_diag = [_dot(acc[i].astype(BF16), v[i * HGRN_DIAG:(i + 1) * HGRN_DIAG]) for i in range(n_diag)]
    o_big = {}
    for (lv, j), a in big_scores.items():
        lo, mid, hi = (2 * j) << lv, (2 * j + 1) << lv, (2 * j + 2) << lv
        o_big[lv, j] = _dot(a.astype(BF16), v[mid:hi] if reverse else v[lo:mid])
    st_new = st * decay + ds
    o = jnp.concatenate(o_diag, axis=0) + o_inter
    for lv in range(HGRN_MASKED_LEVELS, HGRN_LEVELS):
        contrib = []
        for j in range(n >> (lv + 1)):
            zeros = jnp.zeros((1 << lv, HEAD_DIM), F32)
            contrib += [o_big[lv, j], zeros] if reverse else [zeros, o_big[lv, j]]
        o = o + jnp.concatenate(contrib, axis=0)
    return o, st_new


def _hgrn_kernel(*refs, n_blocks, heads, has_state):
    if has_state:
        (q_ref, v_ref, xf_ref, xb_ref, gate_ref, lbf_ref, lbb_ref, g_ref, mask_ref, s0_ref,
         o_ref, acc_ref) = refs
    else:
        (q_ref, v_ref, xf_ref, xb_ref, gate_ref, lbf_ref, lbb_ref, g_ref, mask_ref,
         o_ref, sout_ref, acc_ref) = refs
    n = HGRN_ROWS
    sub = lax.broadcasted_iota(jnp.int32, (n // 8, 8, HEAD_DIM), 1)

    def lower_bound(ref, lanes):
        l0, l1 = ref[0:1, lanes], ref[1:2, lanes]
        mx = jnp.maximum(l0, l1)
        e0, e1 = jnp.exp(l0 - mx), jnp.exp(l1 - mx)
        return e0 / (e0 + e1)

    x_refs = (xf_ref, xb_ref)
    for h in range(heads):
        lanes = slice(h * HEAD_DIM, (h + 1) * HEAD_DIM)
        lbs = [lower_bound(lbf_ref, lanes), lower_bound(lbb_ref, lanes)]
        sts = [s0_ref[0, d, h].T if has_state else jnp.zeros((HEAD_DIM, HEAD_DIM), F32) for d in range(2)]
        for j in range(n_blocks):
            for d in range(2):
                blk = (n_blocks - 1 - j) if d else j
                rows = slice(blk * n, (blk + 1) * n)
                f = lbs[d] + (1.0 - lbs[d]) * jax.nn.sigmoid(x_refs[d][0, rows, lanes])
                o, sts[d] = _hgrn_block(q_ref[0, rows, lanes], 1.0 - f, jnp.log2(f), v_ref[0, rows, lanes],
                                        sts[d], mask_ref, sub, reverse=bool(d))
                acc_ref[d, rows, lanes] = o
        if not has_state:
            for d in range(2):
                sout_ref[0, d, h] = sts[d].T
    for h in range(heads):
        lanes = slice(h * HEAD_DIM, (h + 1) * HEAD_DIM)
        o = acc_ref[0, :, lanes] + acc_ref[1, :, lanes]
        o = o * lax.rsqrt(jnp.mean(o * o, axis=-1, keepdims=True) + NORM_EPS)
        o_ref[0, :, lanes] = (o * g_ref[:, lanes] * gate_ref[0, :, lanes]).astype(BF16)


def _hgrn_call(rq, ri, xf, xb, gate, lbf, lbb, rnn_g, masks, s0):
    b, l, _ = rq.shape
    n_blocks = l // HGRN_ROWS
    heads = N_HEADS
    w = heads * HEAD_DIM
    head = lambda: pl.BlockSpec((1, l, w), lambda i, h: (i, 0, h))
    per_head = lambda rows: pl.BlockSpec((rows, w), lambda i, h: (0, h))
    state_spec = pl.BlockSpec((1, 2, heads, HEAD_DIM, HEAD_DIM), lambda i, h: (i, 0, h, 0, 0))
    in_specs = [head(), head(), head(), head(), head(), per_head(2), per_head(2), per_head(1),
                pl.BlockSpec(masks.shape, lambda i, h: (0, 0, 0))]
    args = [rq, ri, xf, xb, gate, lbf, lbb, rnn_g, masks]
    out_specs = [head()]
    out_shape = [jax.ShapeDtypeStruct((b, l, GROUP_W), BF16)]
    if s0 is not None:
        in_specs.append(state_spec)
        args.append(s0)
    else:
        out_specs.append(state_spec)
        out_shape.append(jax.ShapeDtypeStruct((b, 2, N_HEADS, HEAD_DIM, HEAD_DIM), F32))
    return pl.pallas_call(
        functools.partial(_hgrn_kernel, n_blocks=n_blocks, heads=heads, has_state=s0 is not None),
        grid=(b, N_HEADS // heads),
        in_specs=in_specs,
        out_specs=out_specs,
        out_shape=out_shape,
        scratch_shapes=[pltpu.VMEM((2, l, w), F32)],
        compiler_params=_params(("arbitrary", "arbitrary")),
        name="hgrn_lat" if s0 is not None else "hgrn_ctx",
    )(*args)


def _tail_kernel(att_ref, rnn_ref, x_ref, mod_ref, wo_ref, g1_ref, b1_ref, wu_ref, cw_ref, cb_ref, wd_ref,
                 g2_ref, b2_ref, o_ref, x1_ref, xm2_ref, hid_ref, *, seq_len):
    rows = x_ref.shape[0]
    m = mod_ref[0]
    slabs = [slice(r, r + NORM_ROWS) for r in range(0, rows, NORM_ROWS)]
    mixes = [_dot(att_ref[sl, :], wo_ref[0:GROUP_W, :]) + _dot(rnn_ref[sl, :], wo_ref[GROUP_W:2 * GROUP_W, :])
             for sl in slabs]
    for sl, mix in zip(slabs, mixes):
        x1 = _layer_norm(DEEPNORM_ALPHA * x_ref[sl, :] + m[2:3, :] * mix, g1_ref[...], b1_ref[...])
        x1_ref[sl, :] = x1
        xm2_ref[sl, :] = (x1 * (1.0 + m[4:5, :]) + m[3:4, :]).astype(BF16)

    sub = lax.broadcasted_iota(jnp.int32, (8, FF_TILE), 0)

    def shifted(h, shift, edge_row):
        r = pltpu.roll(h, shift, axis=0)
        parts = []
        for s in range(rows // seq_len):
            edge = s * seq_len + (edge_row // 8) * 8
            fixed = jnp.where(sub == edge_row % 8, 0.0, r[edge:edge + 8])
            parts += [r[s * seq_len:edge], fixed, r[edge + 8:(s + 1) * seq_len]]
        return jnp.concatenate([p for p in parts if p.shape[0]], axis=0)

    def conv(h, cols):
        prev = shifted(h, 1, 0)
        nxt = shifted(h, rows - 1, seq_len - 1)
        return prev * cw_ref[0:1, cols] + h * cw_ref[1:2, cols] + nxt * cw_ref[2:3, cols] + cb_ref[:, cols]

    for j in range(D_FF // FF_TILE):
        cols_a = slice(j * FF_TILE, (j + 1) * FF_TILE)
        cols_u = slice(D_FF + j * FF_TILE, D_FF + (j + 1) * FF_TILE)
        a = conv(_dot(xm2_ref[...], wu_ref[:, cols_a]), cols_a)
        u = conv(_dot(xm2_ref[...], wu_ref[:, cols_u]), cols_u)
        hid_ref[:, cols_a] = (_silu(a) * u).astype(BF16)
    ffns = [_dot(hid_ref[sl, :], wd_ref[...]) for sl in slabs]
    for sl, ffn in zip(slabs, ffns):
        y = DEEPNORM_ALPHA * x1_ref[sl, :] + m[5:6, :] * ffn
        o_ref[sl, :] = _layer_norm(y, g2_ref[...], b2_ref[...])


def _tail_call(att, rnn, x2d, mod3, w_out, ln1_g, ln1_b, w_up, conv_w, conv_b, w_down, ln2_g, ln2_b,
               *, latent, seq_len):
    t = x2d.shape[0]
    rb = ROWS_FFN
    assert rb % seq_len == 0

    def mod_idx(i):
        return ((1 + i * rb // seq_len) if latent else 0, 0, 0)

    row_spec = lambda w: pl.BlockSpec((rb, w), lambda i: (i, 0))
    resident = lambda a: pl.BlockSpec(a.shape, lambda i: (0, 0), pipeline_mode=pl.Buffered(1))
    return pl.pallas_call(
        functools.partial(_tail_kernel, seq_len=seq_len),
        grid=(t // rb,),
        in_specs=[row_spec(GROUP_W), row_spec(GROUP_W), row_spec(D_MODEL),
                  pl.BlockSpec((1, N_MOD, D_MODEL), mod_idx),
                  resident(w_out), resident(ln1_g), resident(ln1_b),
                  resident(w_up), resident(conv_w), resident(conv_b), resident(w_down),
                  resident(ln2_g), resident(ln2_b)],
        out_specs=row_spec(D_MODEL),
        out_shape=jax.ShapeDtypeStruct((t, D_MODEL), F32),
        scratch_shapes=[pltpu.VMEM((rb, D_MODEL), F32), pltpu.VMEM((rb, D_MODEL), BF16),
                        pltpu.VMEM((rb, D_FF), BF16)],
        compiler_params=_params(("arbitrary",)),
        name="tail_lat" if latent else "tail_ctx",
    )(att, rnn, x2d, mod3, w_out, ln1_g, ln1_b, w_up, conv_w, conv_b, w_down, ln2_g, ln2_b)


def _rope_tables(seq_len):
    quarter = QK_DIM // 4
    freqs = 1.0 / (ROPE_BASE ** (np.arange(quarter, dtype=np.float64) / quarter))
    t = np.arange(seq_len)
    ang_r = (t // GRID_W)[:, None] * freqs
    ang_c = (t % GRID_W)[:, None] * freqs
    zeros = np.zeros_like(ang_r)

    def tile(parts):
        return jnp.asarray(np.tile(np.concatenate(parts, axis=-1), (1, GROUP_W // QK_DIM)).astype(np.float32))

    cos = tile([np.cos(ang_r), np.cos(ang_r), np.cos(ang_c), np.cos(ang_c)])
    sin_a = tile([-np.sin(ang_r), zeros, -np.sin(ang_c), zeros])
    sin_b = tile([zeros, np.sin(ang_r), zeros, np.sin(ang_c)])
    return cos, sin_a, sin_b


def kernel(x_prompt, x_sample, cache_k, cache_v, state_rnn, c, c_ctx, w_ada, b_ada, w_in, lambda_q1, lambda_k1, lambda_q2, lambda_k2, lb_fwd_logits, lb_bwd_logits, att_norm_g, rnn_norm_g, w_out, ln1_g, ln1_b, w_up, conv_w, conv_b, w_down, ln2_g, ln2_b):
    assert w_ada.shape[0] == DEPTH
    bp, lp, d = x_prompt.shape
    bs, ls, _ = x_sample.shape
    past = cache_k.shape[2]

    assert 1 + bs <= COND_ROWS
    cond = jnp.zeros((COND_ROWS, d), F32).at[0].set(c_ctx).at[1:1 + bs].set(c)
    mod, w_in_b = _mod_call(cond, w_ada[0], b_ada, w_in[0])
    mod3 = mod.reshape(COND_ROWS, N_MOD, d)

    lams = (lambda_q1, lambda_k1, lambda_q2, lambda_k2)
    masks = jnp.asarray(_hgrn_masks(), dtype=BF16)
    xp2d = x_prompt.reshape(bp * lp, d)
    xs2d = x_sample.reshape(bs * ls, d)

    ctx_in = _inproj_call(xp2d, mod3, w_in_b, None, latent=False, seq_len=lp)
    lat_in = _inproj_call(xs2d, mod3, w_in_b, _rope_tables(ls), latent=True, seq_len=ls)
    k_raw, v_raw = ctx_in[8:]
    cache = (jnp.transpose(cache_k, (0, 1, 3, 4, 5, 2)).reshape(bs, GROUP_W, past),
             cache_v.reshape(bs, past * N_HEADS, HEAD_DIM))

    def heads(proj, b, l):
        return tuple(o.reshape(b, l, GROUP_W) for o in proj[:8])

    q, k, v, rq, xf, xb, ri, rg = heads(lat_in, bs, ls)
    att_s, w_out_b, w_up_b, w_down_b = _attn_call(q, k, v, cache, lams, att_norm_g,
                                                  casts=(w_out[0], w_up[0], w_down[0]))
    rnn_s, = _hgrn_call(rq, ri, xf, xb, rg, lb_fwd_logits, lb_bwd_logits, rnn_norm_g, masks,
                        state_rnn.reshape(bs, 2, N_HEADS, HEAD_DIM, HEAD_DIM))
    q, k, v, rq, xf, xb, ri, rg = heads(ctx_in, bp, lp)
    att_p, = _attn_call(q, k, v, None, lams, att_norm_g)
    rnn_p, s_new = _hgrn_call(rq, ri, xf, xb, rg, lb_fwd_logits, lb_bwd_logits, rnn_norm_g, masks, None)

    def tail(att, rnn, x2d, b, l, latent):
        y = _tail_call(att.reshape(b * l, GROUP_W), rnn.reshape(b * l, GROUP_W), x2d, mod3, w_out_b, ln1_g,
                       ln1_b, w_up_b, conv_w[0], conv_b, w_down_b, ln2_g, ln2_b, latent=latent, seq_len=l)
        return y.reshape(b, l, d)

    y_p = tail(att_p, rnn_p, xp2d, bp, lp, False)
    y_s = tail(att_s, rnn_s, xs2d, bs, ls, True)

    new_cache_k = k_raw.reshape(bp, DEPTH, lp, N_HEADS, 2, QK_DIM)
    new_cache_v = v_raw.reshape(bp, DEPTH, lp, N_HEADS, HEAD_DIM)
    new_state = s_new.reshape(bp, DEPTH, 2, N_HEADS, HEAD_DIM, HEAD_DIM)
    return (y_p, y_s, new_cache_k, new_cache_v, new_state)
```

```python
import functools
import math

import jax
import jax.numpy as jnp
import numpy as np
from jax import lax
from jax.experimental import pallas as pl
from jax.experimental.pallas import tpu as pltpu

D_MODEL = 1024
GRID_W = 64
N_HEADS = 4
HEAD_DIM = 128
QK_DIM = 64
GROUP_W = 512
N_GROUPS = 8
D_FF = 2816
N_MOD = 6
ROPE_BASE = 10000.0
DEPTH = 1
DEEPNORM_ALPHA = (2.0 * DEPTH) ** 0.25
NORM_EPS = 1e-5
LAM_INIT = 0.8 - 0.6 * math.exp(-0.3 * 0)
LOG2_E = math.log2(math.e)

V7X_VMEM_BYTES = 64 * 1024 * 1024
VMEM_LIMIT = V7X_VMEM_BYTES * 15 // 16
COND_ROWS = 16

MOD_TILE = 768
ROWS_PROJ = 512
ROWS_FFN = 1024
FF_TILE = 256
NORM_ROWS = 256
Q_ROWS = 512
HGRN_ROWS = 256
HGRN_LEVELS = int(math.log2(HGRN_ROWS))

F32 = jnp.float32
BF16 = jnp.bfloat16


def _params(semantics):
    return pltpu.CompilerParams(dimension_semantics=semantics, vmem_limit_bytes=VMEM_LIMIT)


def _dot(a, b):
    return jnp.dot(a, b, preferred_element_type=F32)


def _dot_nt(a, b):
    return lax.dot_general(a, b, (((1,), (1,)), ((), ())), preferred_element_type=F32)


def _dot_tn(a, b):
    return lax.dot_general(a, b, (((0,), (0,)), ((), ())), preferred_element_type=F32)


def _silu(x):
    return x * jax.nn.sigmoid(x)


def _layer_norm(y, g, b):
    mu = jnp.mean(y, axis=-1, keepdims=True)
    d = y - mu
    var = jnp.mean(d * d, axis=-1, keepdims=True)
    return d * lax.rsqrt(var + NORM_EPS) * g + b


def _mod_kernel(c_ref, w_ref, b_ref, win_ref, o_ref, win_out_ref):
    s = _silu(c_ref[...]).astype(BF16)
    o_ref[...] = _dot(s, w_ref[...].astype(BF16)) + b_ref[...]
    win_out_ref[...] = win_ref[...].astype(BF16)


def _mod_call(cond, w_ada, b_ada, w_in):
    n, d = cond.shape
    cols = w_ada.shape[1]
    tile = MOD_TILE
    n_steps = cols // tile
    slab = pl.BlockSpec((w_in.shape[0] // n_steps, w_in.shape[1]), lambda j: (j, 0))
    return pl.pallas_call(
        _mod_kernel,
        grid=(n_steps,),
        in_specs=[pl.BlockSpec((n, d), lambda j: (0, 0)),
                  pl.BlockSpec((d, tile), lambda j: (0, j)),
                  pl.BlockSpec((1, tile), lambda j: (0, j)),
                  slab],
        out_specs=[pl.BlockSpec((n, tile), lambda j: (0, j)), slab],
        out_shape=[jax.ShapeDtypeStruct((n, cols), F32), jax.ShapeDtypeStruct(w_in.shape, BF16)],
        compiler_params=_params(("arbitrary",)),
        name="mod",
    )(cond, w_ada, b_ada, w_in)


def _rope(x, cos_ref, sin_a_ref, sin_b_ref):
    parts = []
    for c in range(GROUP_W // 128):
        sl = slice(c * 128, (c + 1) * 128)
        xc = x[:, sl]
        parts.append(xc * cos_ref[:, sl]
                     + pltpu.roll(xc, 128 - 16, axis=1) * sin_a_ref[:, sl]
                     + pltpu.roll(xc, 16, axis=1) * sin_b_ref[:, sl])
    return jnp.concatenate(parts, axis=1)


def _inproj_kernel(*refs, latent):
    if latent:
        (x_ref, mod_ref, w_ref, cos_ref, sin_a_ref, sin_b_ref,
         q_ref, k_ref, v_ref, rq_ref, xf_ref, xb_ref, ri_ref, rg_ref) = refs
    else:
        (x_ref, mod_ref, w_ref,
         q_ref, k_ref, v_ref, rq_ref, xf_ref, xb_ref, ri_ref, rg_ref, kraw_ref, vraw_ref) = refs
    m = mod_ref[0]
    xm = (x_ref[...] * (1.0 + m[1:2, :]) + m[0:1, :]).astype(BF16)

    def proj(g):
        return _dot(xm, w_ref[:, g * GROUP_W:(g + 1) * GROUP_W])

    aq = proj(0)
    ak = proj(1)
    av = proj(2)
    if latent:
        aq = _rope(aq, cos_ref, sin_a_ref, sin_b_ref)
        ak = _rope(ak, cos_ref, sin_a_ref, sin_b_ref)
    else:
        rows = ak.shape[0]
        for j in range(2 * N_HEADS):
            kraw_ref[pl.ds(j, rows, stride=2 * N_HEADS), :] = ak[:, j * QK_DIM:(j + 1) * QK_DIM]
        for h in range(N_HEADS):
            vraw_ref[pl.ds(h, rows, stride=N_HEADS), :] = av[:, h * HEAD_DIM:(h + 1) * HEAD_DIM]
    q_ref[...] = (aq * (QK_DIM ** -0.5 * LOG2_E)).astype(BF16)
    k_ref[...] = ak.astype(BF16)
    v_ref[...] = av.astype(BF16)
    rq_ref[...] = proj(3)
    xf_ref[...] = proj(4)
    xb_ref[...] = proj(5)
    ri_ref[...] = proj(6).astype(BF16)
    rg_ref[...] = _silu(proj(7))


def _inproj_call(x2d, mod3, w_in, rope_tabs, *, latent, seq_len):
    t = x2d.shape[0]
    rb = ROWS_PROJ
    blocks_per_seq = seq_len // rb if latent else 1

    def mod_idx(i):
        return ((1 + i // blocks_per_seq) if latent else 0, 0, 0)

    row_spec = lambda w: pl.BlockSpec((rb, w), lambda i: (i, 0))
    in_specs = [row_spec(D_MODEL),
                pl.BlockSpec((1, N_MOD, D_MODEL), mod_idx),
                pl.BlockSpec(w_in.shape, lambda i: (0, 0))]
    args = [x2d, mod3, w_in]
    if latent:
        in_specs += [pl.BlockSpec((rb, GROUP_W), lambda i: (i % blocks_per_seq, 0))] * 3
        args += list(rope_tabs)
    dts = [BF16, BF16, BF16, F32, F32, F32, BF16, F32]
    out_specs = [row_spec(GROUP_W) for _ in dts]
    out_shape = [jax.ShapeDtypeStruct((t, GROUP_W), dt) for dt in dts]
    if not latent:
        out_specs += [pl.BlockSpec((rb * 2 * N_HEADS, QK_DIM), lambda i: (i, 0)),
                      pl.BlockSpec((rb * N_HEADS, HEAD_DIM), lambda i: (i, 0))]
        out_shape += [jax.ShapeDtypeStruct((t * 2 * N_HEADS, QK_DIM), F32),
                      jax.ShapeDtypeStruct((t * N_HEADS, HEAD_DIM), F32)]
    return pl.pallas_call(
        functools.partial(_inproj_kernel, latent=latent),
        grid=(t // rb,),
        in_specs=in_specs,
        out_specs=out_specs,
        out_shape=out_shape,
        compiler_params=_params(("arbitrary",)),
        name="inproj_lat" if latent else "inproj_ctx",
    )(*args)


def _attn_kernel(*refs, has_cache, n_casts):
    if has_cache:
        (q_ref, k_ref, v_ref, ck_ref, cv_ref, lq1, lk1, lq2, lk2, g_ref) = refs[:10]
        o_ref = refs[10 + n_casts]
        for src, dst in zip(refs[10:10 + n_casts], refs[11 + n_casts:]):
            dst[...] = src[...].astype(BF16)
    else:
        (q_ref, k_ref, v_ref, lq1, lk1, lq2, lk2, g_ref, o_ref) = refs
    lam = (jnp.exp(jnp.sum(lq1[...] * lk1[...], axis=-1, keepdims=True))
           - jnp.exp(jnp.sum(lq2[...] * lk2[...], axis=-1, keepdims=True)) + LAM_INIT)
    qb = q_ref.shape[1]
    lane = lax.broadcasted_iota(jnp.int32, (qb, HEAD_DIM), 1)
    first_map = lane < QK_DIM
    zero = jnp.zeros((), BF16)

    def scores(h):
        sl = slice(h * HEAD_DIM, (h + 1) * HEAD_DIM)
        qh = q_ref[0, :, sl]
        qq = jnp.concatenate([jnp.where(first_map, qh, zero), jnp.where(first_map, zero, qh)], axis=0)
        s_n = _dot_nt(k_ref[0, :, sl], qq)
        s_c = None
        if has_cache:
            s_c = _dot_nt(ck_ref[0, sl, :].astype(BF16).T, qq)
        return s_n, s_c

    ahead = 2
    pending = [scores(h) for h in range(ahead)]
    for h in range(N_HEADS):
        sl = slice(h * HEAD_DIM, (h + 1) * HEAD_DIM)
        s_n, s_c = pending.pop(0)
        if h + ahead < N_HEADS:
            pending.append(scores(h + ahead))
        mx = jnp.max(s_n, axis=0, keepdims=True)
        if has_cache:
            mx = jnp.maximum(mx, jnp.max(s_c, axis=0, keepdims=True))
        e_n = jnp.exp2(s_n - mx)
        den = jnp.sum(e_n, axis=0, keepdims=True)
        ev = _dot_tn(v_ref[0, :, sl], e_n.astype(BF16))
        if has_cache:
            e_c = jnp.exp2(s_c - mx)
            den = den + jnp.sum(e_c, axis=0, keepdims=True)
            vc = cv_ref[0, pl.ds(h, e_c.shape[0], stride=N_HEADS), :]
            ev = ev + _dot_tn(vc.astype(BF16), e_c.astype(BF16))
        inv = 1.0 / den
        o = ev[:, :qb] * inv[:, :qb] - ev[:, qb:] * (inv[:, qb:] * lam)
        o = o * lax.rsqrt(jnp.mean(o * o, axis=0, keepdims=True) + NORM_EPS)
        o_ref[0, :, sl] = (o.T * g_ref[:, sl] * (1.0 - LAM_INIT)).astype(BF16)


def _attn_call(q, k, v, cache, lams, att_g, casts=()):
    b, l, _ = q.shape
    qb = min(Q_ROWS, l)
    nq = l // qb
    slab = lambda a: pl.BlockSpec((a.shape[0] // (b * nq), a.shape[1]), lambda i, j: (i * nq + j, 0))
    full = lambda a: pl.BlockSpec((1,) + a.shape[1:], lambda i, j: (i,) + (0,) * (a.ndim - 1))
    in_specs = [pl.BlockSpec((1, qb, GROUP_W), lambda i, j: (i, j, 0)), full(k), full(v)]
    args = [q, k, v]
    if cache is not None:
        in_specs += [full(cache[0]), full(cache[1])]
        args += list(cache)
    in_specs += [pl.BlockSpec((1, QK_DIM), lambda i, j: (0, 0))] * 4
    in_specs += [pl.BlockSpec((1, GROUP_W), lambda i, j: (0, 0))]
    args += list(lams) + [att_g] + list(casts)
    in_specs += [slab(a) for a in casts]
    return pl.pallas_call(
        functools.partial(_attn_kernel, has_cache=cache is not None, n_casts=len(casts)),
        grid=(b, nq),
        in_specs=in_specs,
        out_specs=[pl.BlockSpec((1, qb, GROUP_W), lambda i, j: (i, j, 0))] + [slab(a) for a in casts],
        out_shape=[jax.ShapeDtypeStruct((b, l, GROUP_W), BF16)] + [jax.ShapeDtypeStruct(a.shape, BF16) for a in casts],
        compiler_params=_params(("arbitrary", "arbitrary")),
        name="attn_lat" if cache is not None else "attn_ctx",
    )(*args)


HGRN_DIAG = 128
HGRN_MASKED_LEVELS = 5
HGRN_VREG_LEVELS = 3


def _hgrn_masks():
    r = np.arange(HGRN_DIAG)[:, None]
    c = np.arange(HGRN_DIAG)[None, :]
    levels = []
    for lv in range(HGRN_MASKED_LEVELS):
        same = (r >> (lv + 1)) == (c >> (lv + 1))
        levels.append(same & (((r >> lv) & 1) == 1) & (((c >> lv) & 1) == 0))
    fwd = [r == c] + levels
    bwd = [m.T for m in levels]
    return np.stack(fwd + bwd).astype(np.float32)


def _hgrn_block(q, k, g, v, st, mask_ref, sub, *, reverse):
    n = HGRN_ROWS
    n_diag = n // HGRN_DIAG

    def diag_scores(qe, ke, m):
        return [_dot_nt(qe[i * HGRN_DIAG:(i + 1) * HGRN_DIAG], ke[i * HGRN_DIAG:(i + 1) * HGRN_DIAG]).astype(BF16) * m
                for i in range(n_diag)]

    def level_mask(lv):
        return mask_ref[(1 + HGRN_MASKED_LEVELS + lv) if reverse else (1 + lv)]

    g3 = g.reshape(n // 8, 8, HEAD_DIM)
    zero3 = jnp.zeros_like(g3)
    pre3, suf3 = (zero3, g3) if reverse else (g3, zero3)
    tot3 = g3
    qb, kb = q.astype(BF16), k.astype(BF16)
    acc = diag_scores(qb, kb, mask_ref[0])
    for lv in range(HGRN_VREG_LEVELS):
        b = 1 << lv
        upper = ((sub >> lv) & 1) == 1
        e = jnp.exp2(jnp.where(upper, pre3, suf3)).reshape(n, HEAD_DIM)
        eb = e.astype(BF16)
        part = diag_scores(qb * eb, kb * eb, level_mask(lv))
        acc = [a + p for a, p in zip(acc, part)]
        sib = jnp.where(upper, pltpu.roll(tot3, b, axis=1), pltpu.roll(tot3, 8 - b, axis=1))
        pre3 = pre3 + jnp.where(upper, sib, 0.0)
        suf3 = suf3 + jnp.where(upper, 0.0, sib)
        tot3 = tot3 + sib

    pieces = lambda x3: [x3[i] for i in range(n // 8)]
    pre8, suf8, tot8 = pieces(pre3), pieces(suf3), pieces(tot3)
    rows = lambda xs, lo, hi: jnp.concatenate(xs[lo // 8:hi // 8], axis=0) if hi - lo > 8 else xs[lo // 8]
    big_scores = {}
    for lv in range(HGRN_VREG_LEVELS, HGRN_LEVELS):
        b = 1 << lv
        pb = b // 8
        if lv < HGRN_MASKED_LEVELS:
            u = jnp.concatenate([(pre8 if (i // pb) % 2 else suf8)[i] for i in range(n // 8)], axis=0)
            e = jnp.exp2(u)
            eb = e.astype(BF16)
            part = diag_scores(qb * eb, kb * eb, level_mask(lv))
            acc = [a + p for a, p in zip(acc, part)]
        else:
            for j in range(n // (2 * b)):
                lo, mid, hi = 2 * b * j, 2 * b * j + b, 2 * b * (j + 1)
                e_lo = jnp.exp2(rows(suf8, lo, mid))
                e_up = jnp.exp2(rows(pre8, mid, hi))
                x_lo = (qb if reverse else kb)[lo:mid] * e_lo.astype(BF16)
                x_up = (kb if reverse else qb)[mid:hi] * e_up.astype(BF16)
                big_scores[lv, j] = _dot_nt(x_lo, x_up) if reverse else _dot_nt(x_up, x_lo)
        for j in range(n // (2 * b)):
            lo_p, mid_p, hi_p = 2 * pb * j, 2 * pb * j + pb, 2 * pb * (j + 1)
            t_lo, t_up = tot8[lo_p], tot8[mid_p]
            t_new = t_lo + t_up
            for i in range(lo_p, mid_p):
                suf8[i] = suf8[i] + t_up
                tot8[i] = t_new
            for i in range(mid_p, hi_p):
                pre8[i] = pre8[i] + t_lo
                tot8[i] = t_new
    pre = jnp.concatenate(pre8, axis=0)
    suf = jnp.concatenate(suf8, axis=0)
    q_dec, k_dec = (suf, pre) if reverse else (pre, suf)
    q_in = qb * jnp.exp2(q_dec).astype(BF16)
    k_out = kb * jnp.exp2(k_dec).astype(BF16)
    decay = jnp.exp2(tot8[0][0:1, :])

    o_inter = _dot_nt(q_in, st.astype(BF16))
    ds = _dot_tn(v, k_out)
    o_diag = [_dot(acc[i].astype(BF16), v[i * HGRN_DIAG:(i + 1) * HGRN_DIAG]) for i in range(n_diag)]
    o_big = {}
    for (lv, j), a in big_scores.items():
        lo, mid, hi = (2 * j) << lv, (2 * j + 1) << lv, (2 * j + 2) << lv
        o_big[lv, j] = _dot(a.astype(BF16), v[mid:hi] if reverse else v[lo:mid])
    st_new = st * decay + ds
    o = jnp.concatenate(o_diag, axis=0) + o_inter
    for lv in range(HGRN_MASKED_LEVELS, HGRN_LEVELS):
        contrib = []
        for j in range(n >> (lv + 1)):
            zeros = jnp.zeros((1 << lv, HEAD_DIM), F32)
            contrib += [o_big[lv, j], zeros] if reverse else [zeros, o_big[lv, j]]
        o = o + jnp.concatenate(contrib, axis=0)
    return o, st_new


def _hgrn_kernel(*refs, n_blocks, heads, has_state):
    if has_state:
        (q_ref, v_ref, xf_ref, xb_ref, gate_ref, lbf_ref, lbb_ref, g_ref, mask_ref, s0_ref,
         o_ref, acc_ref) = refs
    else:
        (q_ref, v_ref, xf_ref, xb_ref, gate_ref, lbf_ref, lbb_ref, g_ref, mask_ref,
         o_ref, sout_ref, acc_ref) = refs
    n = HGRN_ROWS
    sub = lax.broadcasted_iota(jnp.int32, (n // 8, 8, HEAD_DIM), 1)

    def lower_bound(ref, lanes):
        l0, l1 = ref[0:1, lanes], ref[1:2, lanes]
        mx = jnp.maximum(l0, l1)
        e0, e1 = jnp.exp(l0 - mx), jnp.exp(l1 - mx)
        return e0 / (e0 + e1)

    x_refs = (xf_ref, xb_ref)
    for h in range(heads):
        lanes = slice(h * HEAD_DIM, (h + 1) * HEAD_DIM)
        lbs = [lower_bound(lbf_ref, lanes), lower_bound(lbb_ref, lanes)]
        sts = [s0_ref[0, d, h].T if has_state else jnp.zeros((HEAD_DIM, HEAD_DIM), F32) for d in range(2)]
        for j in range(n_blocks):
            for d in range(2):
                blk = (n_blocks - 1 - j) if d else j
                rows = slice(blk * n, (blk + 1) * n)
                f = lbs[d] + (1.0 - lbs[d]) * jax.nn.sigmoid(x_refs[d][0, rows, lanes])
                o, sts[d] = _hgrn_block(q_ref[0, rows, lanes], 1.0 - f, jnp.log2(f), v_ref[0, rows, lanes],
                                        sts[d], mask_ref, sub, reverse=bool(d))
                acc_ref[d, rows, lanes] = o
        if not has_state:
            for d in range(2):
                sout_ref[0, d, h] = sts[d].T
    for h in range(heads):
        lanes = slice(h * HEAD_DIM, (h + 1) * HEAD_DIM)
        o = acc_ref[0, :, lanes] + acc_ref[1, :, lanes]
        o = o * lax.rsqrt(jnp.mean(o * o, axis=-1, keepdims=True) + NORM_EPS)
        o_ref[0, :, lanes] = (o * g_ref[:, lanes] * gate_ref[0, :, lanes]).astype(BF16)


def _hgrn_call(rq, ri, xf, xb, gate, lbf, lbb, rnn_g, masks, s0):
    b, l, _ = rq.shape
    n_blocks = l // HGRN_ROWS
    heads = N_HEADS
    w = heads * HEAD_DIM
    head = lambda: pl.BlockSpec((1, l, w), lambda i, h: (i, 0, h))
    per_head = lambda rows: pl.BlockSpec((rows, w), lambda i, h: (0, h))
    state_spec = pl.BlockSpec((1, 2, heads, HEAD_DIM, HEAD_DIM), lambda i, h: (i, 0, h, 0, 0))
    in_specs = [head(), head(), head(), head(), head(), per_head(2), per_head(2), per_head(1),
                pl.BlockSpec(masks.shape, lambda i, h: (0, 0, 0))]
    args = [rq, ri, xf, xb, gate, lbf, lbb, rnn_g, masks]
    out_specs = [head()]
    out_shape = [jax.ShapeDtypeStruct((b, l, GROUP_W), BF16)]
    if s0 is not None:
        in_specs.append(state_spec)
        args.append(s0)
    else:
        out_specs.append(state_spec)
        out_shape.append(jax.ShapeDtypeStruct((b, 2, N_HEADS, HEAD_DIM, HEAD_DIM), F32))
    return pl.pallas_call(
        functools.partial(_hgrn_kernel, n_blocks=n_blocks, heads=heads, has_state=s0 is not None),
        grid=(b, N_HEADS // heads),
        in_specs=in_specs,
        out_specs=out_specs,
        out_shape=out_shape,
        scratch_shapes=[pltpu.VMEM((2, l, w), F32)],
        compiler_params=_params(("arbitrary", "arbitrary")),
        name="hgrn_lat" if s0 is not None else "hgrn_ctx",
    )(*args)


def _tail_kernel(att_ref, rnn_ref, x_ref, mod_ref, wo_ref, g1_ref, b1_ref, wu_ref, cw_ref, cb_ref, wd_ref,
                 g2_ref, b2_ref, o_ref, x1_ref, xm2_ref, hid_ref, *, seq_len):
    rows = x_ref.shape[0]
    m = mod_ref[0]
    slabs = [slice(r, r + NORM_ROWS) for r in range(0, rows, NORM_ROWS)]
    mixes = [_dot(att_ref[sl, :], wo_ref[0:GROUP_W, :]) + _dot(rnn_ref[sl, :], wo_ref[GROUP_W:2 * GROUP_W, :])
             for sl in slabs]
    for sl, mix in zip(slabs, mixes):
        x1 = _layer_norm(DEEPNORM_ALPHA * x_ref[sl, :] + m[2:3, :] * mix, g1_ref[...], b1_ref[...])
        x1_ref[sl, :] = x1
        xm2_ref[sl, :] = (x1 * (1.0 + m[4:5, :]) + m[3:4, :]).astype(BF16)

    sub = lax.broadcasted_iota(jnp.int32, (8, FF_TILE), 0)

    def shifted(h, shift, edge_row):
        r = pltpu.roll(h, shift, axis=0)
        parts = []
        for s in range(rows // seq_len):
            edge = s * seq_len + (edge_row // 8) * 8
            fixed = jnp.where(sub == edge_row % 8, 0.0, r[edge:edge + 8])
            parts += [r[s * seq_len:edge], fixed, r[edge + 8:(s + 1) * seq_len]]
        return jnp.concatenate([p for p in parts if p.shape[0]], axis=0)

    def conv(h, cols):
        prev = shifted(h, 1, 0)
        nxt = shifted(h, rows - 1, seq_len - 1)
        return prev * cw_ref[0:1, cols] + h * cw_ref[1:2, cols] + nxt * cw_ref[2:3, cols] + cb_ref[:, cols]

    for j in range(D_FF // FF_TILE):
        cols_a = slice(j * FF_TILE, (j + 1) * FF_TILE)
        cols_u = slice(D_FF + j * FF_TILE, D_FF + (j + 1) * FF_TILE)
        a = conv(_dot(xm2_ref[...], wu_ref[:, cols_a]), cols_a)
        u = conv(_dot(xm2_ref[...], wu_ref[:, cols_u]), cols_u)
        hid_ref[:, cols_a] = (_silu(a) * u).astype(BF16)
    ffns = [_dot(hid_ref[sl, :], wd_ref[...]) for sl in slabs]
    for sl, ffn in zip(slabs, ffns):
        y = DEEPNORM_ALPHA * x1_ref[sl, :] + m[5:6, :] * ffn
        o_ref[sl, :] = _layer_norm(y, g2_ref[...], b2_ref[...])


def _tail_call(att, rnn, x2d, mod3, w_out, ln1_g, ln1_b, w_up, conv_w, conv_b, w_down, ln2_g, ln2_b,
               *, latent, seq_len):
    t = x2d.shape[0]
    rb = ROWS_FFN
    assert rb % seq_len == 0

    def mod_idx(i):
        return ((1 + i * rb // seq_len) if latent else 0, 0, 0)

    row_spec = lambda w: pl.BlockSpec((rb, w), lambda i: (i, 0))
    resident = lambda a: pl.BlockSpec(a.shape, lambda i: (0, 0), pipeline_mode=pl.Buffered(1))
    return pl.pallas_call(
        functools.partial(_tail_kernel, seq_len=seq_len),
        grid=(t // rb,),
        in_specs=[row_spec(GROUP_W), row_spec(GROUP_W), row_spec(D_MODEL),
                  pl.BlockSpec((1, N_MOD, D_MODEL), mod_idx),
                  resident(w_out), resident(ln1_g), resident(ln1_b),
                  resident(w_up), resident(conv_w), resident(conv_b), resident(w_down),
                  resident(ln2_g), resident(ln2_b)],
        out_specs=row_spec(D_MODEL),
        out_shape=jax.ShapeDtypeStruct((t, D_MODEL), F32),
        scratch_shapes=[pltpu.VMEM((rb, D_MODEL), F32), pltpu.VMEM((rb, D_MODEL), BF16),
                        pltpu.VMEM((rb, D_FF), BF16)],
        compiler_params=_params(("arbitrary",)),
        name="tail_lat" if latent else "tail_ctx",
    )(att, rnn, x2d, mod3, w_out, ln1_g, ln1_b, w_up, conv_w, conv_b, w_down, ln2_g, ln2_b)


def _rope_tables(seq_len):
    quarter = QK_DIM // 4
    freqs = 1.0 / (ROPE_BASE ** (np.arange(quarter, dtype=np.float64) / quarter))
    t = np.arange(seq_len)
    ang_r = (t // GRID_W)[:, None] * freqs
    ang_c = (t % GRID_W)[:, None] * freqs
    zeros = np.zeros_like(ang_r)

    def tile(parts):
        return jnp.asarray(np.tile(np.concatenate(parts, axis=-1), (1, GROUP_W // QK_DIM)).astype(np.float32))

    cos = tile([np.cos(ang_r), np.cos(ang_r), np.cos(ang_c), np.cos(ang_c)])
    sin_a = tile([-np.sin(ang_r), zeros, -np.sin(ang_c), zeros])
    sin_b = tile([zeros, np.sin(ang_r), zeros, np.sin(ang_c)])
    return cos, sin_a, sin_b


def kernel(x_prompt, x_sample, cache_k, cache_v, state_rnn, c, c_ctx, w_ada, b_ada, w_in, lambda_q1, lambda_k1, lambda_q2, lambda_k2, lb_fwd_logits, lb_bwd_logits, att_norm_g, rnn_norm_g, w_out, ln1_g, ln1_b, w_up, conv_w, conv_b, w_down, ln2_g, ln2_b):
    assert w_ada.shape[0] == DEPTH
    bp, lp, d = x_prompt.shape
    bs, ls, _ = x_sample.shape
    past = cache_k.shape[2]

    assert 1 + bs <= COND_ROWS
    cond = jnp.zeros((COND_ROWS, d), F32).at[0].set(c_ctx).at[1:1 + bs].set(c)
    mod, w_in_b = _mod_call(cond, w_ada[0], b_ada, w_in[0])
    mod3 = mod.reshape(COND_ROWS, N_MOD, d)

    lams = (lambda_q1, lambda_k1, lambda_q2, lambda_k2)
    masks = jnp.asarray(_hgrn_masks(), dtype=BF16)
    xp2d = x_prompt.reshape(bp * lp, d)
    xs2d = x_sample.reshape(bs * ls, d)

    ctx_in = _inproj_call(xp2d, mod3, w_in_b, None, latent=False, seq_len=lp)
    lat_in = _inproj_call(xs2d, mod3, w_in_b, _rope_tables(ls), latent=True, seq_len=ls)
    k_raw, v_raw = ctx_in[8:]
    cache = (jnp.transpose(cache_k, (0, 1, 3, 4, 5, 2)).reshape(bs, GROUP_W, past),
             cache_v.reshape(bs, past * N_HEADS, HEAD_DIM))

    def heads(proj, b, l):
        return tuple(o.reshape(b, l, GROUP_W) for o in proj[:8])

    q, k, v, rq, xf, xb, ri, rg = heads(lat_in, bs, ls)
    att_s, w_out_b, w_up_b, w_down_b = _attn_call(q, k, v, cache, lams, att_norm_g,
                                                  casts=(w_out[0], w_up[0], w_down[0]))
    rnn_s, = _hgrn_call(rq, ri, xf, xb, rg, lb_fwd_logits, lb_bwd_logits, rnn_norm_g, masks,
                        state_rnn.reshape(bs, 2, N_HEADS, HEAD_DIM, HEAD_DIM))
    q, k, v, rq, xf, xb, ri, rg = heads(ctx_in, bp, lp)
    att_p, = _attn_call(q, k, v, None, lams, att_norm_g)
    rnn_p, s_new = _hgrn_call(rq, ri, xf, xb, rg, lb_fwd_logits, lb_bwd_logits, rnn_norm_g, masks, None)

    def tail(att, rnn, x2d, b, l, latent):
        y = _tail_call(att.reshape(b * l, GROUP_W), rnn.reshape(b * l, GROUP_W), x2d, mod3, w_out_b, ln1_g,
                       ln1_b, w_up_b, conv_w[0], conv_b, w_down_b, ln2_g, ln2_b, latent=latent, seq_len=l)
        return y.reshape(b, l, d)

    y_p = tail(att_p, rnn_p, xp2d, bp, lp, False)
    y_s = tail(att_s, rnn_s, xs2d, bs, ls, True)

    new_cache_k = k_raw.reshape(bp, DEPTH, lp, N_HEADS, 2, QK_DIM)
    new_cache_v = v_raw.reshape(bp, DEPTH, lp, N_HEADS, HEAD_DIM)
    new_state = s_new.reshape(bp, DEPTH, 2, N_HEADS, HEAD_DIM, HEAD_DIM)
    return (y_p, y_s, new_cache_k, new_cache_v, new_state)
```

```python
import functools
import math

import jax
import jax.numpy as jnp
import numpy as np
from jax import lax
from jax.experimental import pallas as pl
from jax.experimental.pallas import tpu as pltpu

D_MODEL = 1024
GRID_W = 64
N_HEADS = 4
HEAD_DIM = 128
QK_DIM = 64
GROUP_W = 512
N_GROUPS = 8
D_FF = 2816
N_MOD = 6
ROPE_BASE = 10000.0
DEPTH = 1
DEEPNORM_ALPHA = (2.0 * DEPTH) ** 0.25
NORM_EPS = 1e-5
LAM_INIT = 0.8 - 0.6 * math.exp(-0.3 * 0)
LOG2_E = math.log2(math.e)

V7X_VMEM_BYTES = 64 * 1024 * 1024
VMEM_LIMIT = V7X_VMEM_BYTES * 15 // 16
COND_ROWS = 16

MOD_TILE = 768
ROWS_PROJ = 512
ROWS_FFN = 1024
FF_TILE = 256
NORM_ROWS = 256
Q_ROWS = 512
SHORT_SEQS_PER_STEP = 2
HGRN_ROWS = 256
HGRN_LEVELS = int(math.log2(HGRN_ROWS))

F32 = jnp.float32
BF16 = jnp.bfloat16


def _params(semantics):
    return pltpu.CompilerParams(dimension_semantics=semantics, vmem_limit_bytes=VMEM_LIMIT)


def _dot(a, b):
    return jnp.dot(a, b, preferred_element_type=F32)


def _dot_nt(a, b):
    return lax.dot_general(a, b, (((1,), (1,)), ((), ())), preferred_element_type=F32)


def _dot_tn(a, b):
    return lax.dot_general(a, b, (((0,), (0,)), ((), ())), preferred_element_type=F32)


def _silu(x):
    return x * jax.nn.sigmoid(x)


def _layer_norm(y, g, b):
    mu = jnp.mean(y, axis=-1, keepdims=True)
    d = y - mu
    var = jnp.mean(d * d, axis=-1, keepdims=True)
    return d * lax.rsqrt(var + NORM_EPS) * g + b


def _mod_kernel(c_ref, w_ref, b_ref, win_ref, o_ref, win_out_ref):
    s = _silu(c_ref[...]).astype(BF16)
    o_ref[...] = _dot(s, w_ref[...].astype(BF16)) + b_ref[...]
    win_out_ref[...] = win_ref[...].astype(BF16)


def _mod_call(cond, w_ada, b_ada, w_in):
    n, d = cond.shape
    cols = w_ada.shape[1]
    tile = MOD_TILE
    n_steps = cols // tile
    slab = pl.BlockSpec((w_in.shape[0] // n_steps, w_in.shape[1]), lambda j: (j, 0))
    return pl.pallas_call(
        _mod_kernel,
        grid=(n_steps,),
        in_specs=[pl.BlockSpec((n, d), lambda j: (0, 0)),
                  pl.BlockSpec((d, tile), lambda j: (0, j)),
                  pl.BlockSpec((1, tile), lambda j: (0, j)),
                  slab],
        out_specs=[pl.BlockSpec((n, tile), lambda j: (0, j)), slab],
        out_shape=[jax.ShapeDtypeStruct((n, cols), F32), jax.ShapeDtypeStruct(w_in.shape, BF16)],
        compiler_params=_params(("arbitrary",)),
        name="mod",
    )(cond, w_ada, b_ada, w_in)


def _rope(x, cos_ref, sin_a_ref, sin_b_ref):
    parts = []
    for c in range(GROUP_W // 128):
        sl = slice(c * 128, (c + 1) * 128)
        xc = x[:, sl]
        parts.append(xc * cos_ref[:, sl]
                     + pltpu.roll(xc, 128 - 16, axis=1) * sin_a_ref[:, sl]
                     + pltpu.roll(xc, 16, axis=1) * sin_b_ref[:, sl])
    return jnp.concatenate(parts, axis=1)


def _inproj_kernel(*refs, latent):
    if latent:
        (x_ref, mod_ref, w_ref, cos_ref, sin_a_ref, sin_b_ref,
         q_ref, k_ref, v_ref, rq_ref, xf_ref, xb_ref, ri_ref, rg_ref) = refs
    else:
        (x_ref, mod_ref, w_ref,
         q_ref, k_ref, v_ref, rq_ref, xf_ref, xb_ref, ri_ref, rg_ref, kraw_ref, vraw_ref) = refs
    m = mod_ref[0]
    xm = (x_ref[...] * (1.0 + m[1:2, :]) + m[0:1, :]).astype(BF16)

    def proj(g):
        return _dot(xm, w_ref[:, g * GROUP_W:(g + 1) * GROUP_W])

    aq = proj(0)
    ak = proj(1)
    av = proj(2)
    if latent:
        aq = _rope(aq, cos_ref, sin_a_ref, sin_b_ref)
        ak = _rope(ak, cos_ref, sin_a_ref, sin_b_ref)
    else:
        rows = ak.shape[0]
        for j in range(2 * N_HEADS):
            kraw_ref[pl.ds(j, rows, stride=2 * N_HEADS), :] = ak[:, j * QK_DIM:(j + 1) * QK_DIM]
        for h in range(N_HEADS):
            vraw_ref[pl.ds(h, rows, stride=N_HEADS), :] = av[:, h * HEAD_DIM:(h + 1) * HEAD_DIM]
    q_ref[...] = (aq * (QK_DIM ** -0.5 * LOG2_E)).astype(BF16)
    k_ref[...] = ak.astype(BF16)
    v_ref[...] = av.astype(BF16)
    rq_ref[...] = proj(3)
    xf_ref[...] = proj(4)
    xb_ref[...] = proj(5)
    ri_ref[...] = proj(6).astype(BF16)
    rg_ref[...] = _silu(proj(7))


def _inproj_call(x2d, mod3, w_in, rope_tabs, *, latent, seq_len):
    t = x2d.shape[0]
    rb = ROWS_PROJ
    blocks_per_seq = seq_len // rb if latent else 1

    def mod_idx(i):
        return ((1 + i // blocks_per_seq) if latent else 0, 0, 0)

    row_spec = lambda w: pl.BlockSpec((rb, w), lambda i: (i, 0))
    in_specs = [row_spec(D_MODEL),
                pl.BlockSpec((1, N_MOD, D_MODEL), mod_idx),
                pl.BlockSpec(w_in.shape, lambda i: (0, 0))]
    args = [x2d, mod3, w_in]
    if latent:
        in_specs += [pl.BlockSpec((rb, GROUP_W), lambda i: (i % blocks_per_seq, 0))] * 3
        args += list(rope_tabs)
    dts = [BF16, BF16, BF16, F32, F32, F32, BF16, F32]
    out_specs = [row_spec(GROUP_W) for _ in dts]
    out_shape = [jax.ShapeDtypeStruct((t, GROUP_W), dt) for dt in dts]
    if not latent:
        out_specs += [pl.BlockSpec((rb * 2 * N_HEADS, QK_DIM), lambda i: (i, 0)),
                      pl.BlockSpec((rb * N_HEADS, HEAD_DIM), lambda i: (i, 0))]
        out_shape += [jax.ShapeDtypeStruct((t * 2 * N_HEADS, QK_DIM), F32),
                      jax.ShapeDtypeStruct((t * N_HEADS, HEAD_DIM), F32)]
    return pl.pallas_call(
        functools.partial(_inproj_kernel, latent=latent),
        grid=(t // rb,),
        in_specs=in_specs,
        out_specs=out_specs,
        out_shape=out_shape,
        compiler_params=_params(("arbitrary",)),
        name="inproj_lat" if latent else "inproj_ctx",
    )(*args)


def _attn_kernel(*refs, has_cache, n_casts):
    if has_cache:
        (q_ref, k_ref, v_ref, ck_ref, cv_ref, lq1, lk1, lq2, lk2, g_ref) = refs[:10]
        o_ref = refs[10 + n_casts]
        for src, dst in zip(refs[10:10 + n_casts], refs[11 + n_casts:]):
            dst[...] = src[...].astype(BF16)
    else:
        (q_ref, k_ref, v_ref, lq1, lk1, lq2, lk2, g_ref, o_ref) = refs
    lam = (jnp.exp(jnp.sum(lq1[...] * lk1[...], axis=-1, keepdims=True))
           - jnp.exp(jnp.sum(lq2[...] * lk2[...], axis=-1, keepdims=True)) + LAM_INIT)
    qb = q_ref.shape[1]
    lane = lax.broadcasted_iota(jnp.int32, (qb, HEAD_DIM), 1)
    first_map = lane < QK_DIM
    zero = jnp.zeros((), BF16)

    units = [(b, h) for b in range(q_ref.shape[0]) for h in range(N_HEADS)]

    def scores(u):
        b, h = units[u]
        sl = slice(h * HEAD_DIM, (h + 1) * HEAD_DIM)
        qh = q_ref[b, :, sl]
        qq = jnp.concatenate([jnp.where(first_map, qh, zero), jnp.where(first_map, zero, qh)], axis=0)
        s_n = _dot_nt(k_ref[b, :, sl], qq)
        s_c = None
        if has_cache:
            s_c = _dot_nt(ck_ref[b, sl, :].astype(BF16).T, qq)
        return s_n, s_c

    ahead = 2
    pending = [scores(u) for u in range(ahead)]
    for u, (b, h) in enumerate(units):
        sl = slice(h * HEAD_DIM, (h + 1) * HEAD_DIM)
        s_n, s_c = pending.pop(0)
        if u + ahead < len(units):
            pending.append(scores(u + ahead))
        mx = jnp.max(s_n, axis=0, keepdims=True)
        if has_cache:
            mx = jnp.maximum(mx, jnp.max(s_c, axis=0, keepdims=True))
        e_n = jnp.exp2(s_n - mx)
        den = jnp.sum(e_n, axis=0, keepdims=True)
        ev = _dot_tn(v_ref[b, :, sl], e_n.astype(BF16))
        if has_cache:
            e_c = jnp.exp2(s_c - mx)
            den = den + jnp.sum(e_c, axis=0, keepdims=True)
            vc = cv_ref[b, pl.ds(h, e_c.shape[0], stride=N_HEADS), :]
            ev = ev + _dot_tn(vc.astype(BF16), e_c.astype(BF16))
        inv = 1.0 / den
        o = ev[:, :qb] * inv[:, :qb] - ev[:, qb:] * (inv[:, qb:] * lam)
        o = o * lax.rsqrt(jnp.mean(o * o, axis=0, keepdims=True) + NORM_EPS)
        o_ref[b, :, sl] = (o.T * g_ref[:, sl] * (1.0 - LAM_INIT)).astype(BF16)


def _attn_call(q, k, v, cache, lams, att_g, casts=()):
    b, l, _ = q.shape
    qb = min(Q_ROWS, l)
    nq = l // qb
    nb = SHORT_SEQS_PER_STEP if nq == 1 and b % SHORT_SEQS_PER_STEP == 0 else 1
    steps = b // nb
    slab = lambda a: pl.BlockSpec((a.shape[0] // (steps * nq), a.shape[1]), lambda i, j: (i * nq + j, 0))
    full = lambda a: pl.BlockSpec((nb,) + a.shape[1:], lambda i, j: (i,) + (0,) * (a.ndim - 1))
    in_specs = [pl.BlockSpec((nb, qb, GROUP_W), lambda i, j: (i, j, 0)), full(k), full(v)]
    args = [q, k, v]
    if cache is not None:
        in_specs += [full(cache[0]), full(cache[1])]
        args += list(cache)
    in_specs += [pl.BlockSpec((1, QK_DIM), lambda i, j: (0, 0))] * 4
    in_specs += [pl.BlockSpec((1, GROUP_W), lambda i, j: (0, 0))]
    args += list(lams) + [att_g] + list(casts)
    in_specs += [slab(a) for a in casts]
    return pl.pallas_call(
        functools.partial(_attn_kernel, has_cache=cache is not None, n_casts=len(casts)),
        grid=(steps, nq),
        in_specs=in_specs,
        out_specs=[pl.BlockSpec((nb, qb, GROUP_W), lambda i, j: (i, j, 0))] + [slab(a) for a in casts],
        out_shape=[jax.ShapeDtypeStruct((b, l, GROUP_W), BF16)] + [jax.ShapeDtypeStruct(a.shape, BF16) for a in casts],
        compiler_params=_params(("arbitrary", "arbitrary")),
        name="attn_lat" if cache is not None else "attn_ctx",
    )(*args)


HGRN_DIAG = 128
HGRN_MASKED_LEVELS = 5
HGRN_VREG_LEVELS = 3


def _hgrn_masks():
    r = np.arange(HGRN_DIAG)[:, None]
    c = np.arange(HGRN_DIAG)[None, :]
    levels = []
    for lv in range(HGRN_MASKED_LEVELS):
        same = (r >> (lv + 1)) == (c >> (lv + 1))
        levels.append(same & (((r >> lv) & 1) == 1) & (((c >> lv) & 1) == 0))
    fwd = [r == c] + levels
    bwd = [m.T for m in levels]
    return np.stack(fwd + bwd).astype(np.float32)


def _hgrn_block(q, k, g, v, st, mask_ref, sub, *, reverse):
    n = HGRN_ROWS
    n_diag = n // HGRN_DIAG

    def diag_scores(qe, ke, m):
        return [_dot_nt(qe[i * HGRN_DIAG:(i + 1) * HGRN_DIAG], ke[i * HGRN_DIAG:(i + 1) * HGRN_DIAG]).astype(BF16) * m
                for i in range(n_diag)]

    def level_mask(lv):
        return mask_ref[(1 + HGRN_MASKED_LEVELS + lv) if reverse else (1 + lv)]

    g3 = g.reshape(n // 8, 8, HEAD_DIM)
    zero3 = jnp.zeros_like(g3)
    pre3, suf3 = (zero3, g3) if reverse else (g3, zero3)
    tot3 = g3
    qb, kb = q.astype(BF16), k.astype(BF16)
    acc = diag_scores(qb, kb, mask_ref[0])
    for lv in range(HGRN_VREG_LEVELS):
        b = 1 << lv
        upper = ((sub >> lv) & 1) == 1
        e = jnp.exp2(jnp.where(upper, pre3, suf3)).reshape(n, HEAD_DIM)
        eb = e.astype(BF16)
        part = diag_scores(qb * eb, kb * eb, level_mask(lv))
        acc = [a + p for a, p in zip(acc, part)]
        sib = jnp.where(upper, pltpu.roll(tot3, b, axis=1), pltpu.roll(tot3, 8 - b, axis=1))
        pre3 = pre3 + jnp.where(upper, sib, 0.0)
        suf3 = suf3 + jnp.where(upper, 0.0, sib)
        tot3 = tot3 + sib

    pieces = lambda x3: [x3[i] for i in range(n // 8)]
    pre8, suf8, tot8 = pieces(pre3), pieces(suf3), pieces(tot3)
    rows = lambda xs, lo, hi: jnp.concatenate(xs[lo // 8:hi // 8], axis=0) if hi - lo > 8 else xs[lo // 8]
    big_scores = {}
    for lv in range(HGRN_VREG_LEVELS, HGRN_LEVELS):
        b = 1 << lv
        pb = b // 8
        if lv < HGRN_MASKED_LEVELS:
            u = jnp.concatenate([(pre8 if (i // pb) % 2 else suf8)[i] for i in range(n // 8)], axis=0)
            e = jnp.exp2(u)
            eb = e.astype(BF16)
            part = diag_scores(qb * eb, kb * eb, level_mask(lv))
            acc = [a + p for a, p in zip(acc, part)]
        else:
            for j in range(n // (2 * b)):
                lo, mid, hi = 2 * b * j, 2 * b * j + b, 2 * b * (j + 1)
                e_lo = jnp.exp2(rows(suf8, lo, mid))
                e_up = jnp.exp2(rows(pre8, mid, hi))
                x_lo = (qb if reverse else kb)[lo:mid] * e_lo.astype(BF16)
                x_up = (kb if reverse else qb)[mid:hi] * e_up.astype(BF16)
                big_scores[lv, j] = _dot_nt(x_lo, x_up) if reverse else _dot_nt(x_up, x_lo)
        for j in range(n // (2 * b)):
            lo_p, mid_p, hi_p = 2 * pb * j, 2 * pb * j + pb, 2 * pb * (j + 1)
            t_lo, t_up = tot8[lo_p], tot8[mid_p]
            t_new = t_lo + t_up
            for i in range(lo_p, mid_p):
                suf8[i] = suf8[i] + t_up
                tot8[i] = t_new
            for i in range(mid_p, hi_p):
                pre8[i] = pre8[i] + t_lo
                tot8[i] = t_new
    pre = jnp.concatenate(pre8, axis=0)
    suf = jnp.concatenate(suf8, axis=0)
    q_dec, k_dec = (suf, pre) if reverse else (pre, suf)
    q_in = qb * jnp.exp2(q_dec).astype(BF16)
    k_out = kb * jnp.exp2(k_dec).astype(BF16)
    decay = jnp.exp2(tot8[0][0:1, :])

    o_inter = _dot_nt(q_in, st.astype(BF16))
    ds = _dot_tn(v, k_out)
    o_diag = [_dot(acc[i].astype(BF16), v[i * HGRN_DIAG:(i + 1) * HGRN_DIAG]) for i in range(n_diag)]
    o_big = {}
    for (lv, j), a in big_scores.items():
        lo, mid, hi = (2 * j) << lv, (2 * j + 1) << lv, (2 * j + 2) << lv
        o_big[lv, j] = _dot(a.astype(BF16), v[mid:hi] if reverse else v[lo:mid])
    st_new = st * decay + ds
    o = jnp.concatenate(o_diag, axis=0) + o_inter
    for lv in range(HGRN_MASKED_LEVELS, HGRN_LEVELS):
        contrib = []
        for j in range(n >> (lv + 1)):
            zeros = jnp.zeros((1 << lv, HEAD_DIM), F32)
            contrib += [o_big[lv, j], zeros] if reverse else [zeros, o_big[lv, j]]
        o = o + jnp.concatenate(contrib, axis=0)
    return o, st_new


def _hgrn_kernel(*refs, n_blocks, has_state):
    if has_state:
        (q_ref, v_ref, xf_ref, xb_ref, gate_ref, lbf_ref, lbb_ref, g_ref, mask_ref, s0_ref,
         o_ref, acc_ref) = refs
    else:
        (q_ref, v_ref, xf_ref, xb_ref, gate_ref, lbf_ref, lbb_ref, g_ref, mask_ref,
         o_ref, sout_ref, acc_ref) = refs
    n = HGRN_ROWS
    sub = lax.broadcasted_iota(jnp.int32, (n // 8, 8, HEAD_DIM), 1)
    head_lanes = [slice(h * HEAD_DIM, (h + 1) * HEAD_DIM) for h in range(N_HEADS)]

    def lower_bound(ref, lanes):
        l0, l1 = ref[0:1, lanes], ref[1:2, lanes]
        mx = jnp.maximum(l0, l1)
        e0, e1 = jnp.exp(l0 - mx), jnp.exp(l1 - mx)
        return e0 / (e0 + e1)

    x_refs = (xf_ref, xb_ref)
    for b in range(q_ref.shape[0]):
        for h, lanes in enumerate(head_lanes):
            lbs = [lower_bound(lbf_ref, lanes), lower_bound(lbb_ref, lanes)]
            sts = [s0_ref[b, d, h].T if has_state else jnp.zeros((HEAD_DIM, HEAD_DIM), F32) for d in range(2)]
            for j in range(n_blocks):
                for d in range(2):
                    blk = (n_blocks - 1 - j) if d else j
                    rows = slice(blk * n, (blk + 1) * n)
                    f = lbs[d] + (1.0 - lbs[d]) * jax.nn.sigmoid(x_refs[d][b, rows, lanes])
                    o, sts[d] = _hgrn_block(q_ref[b, rows, lanes], 1.0 - f, jnp.log2(f), v_ref[b, rows, lanes],
                                            sts[d], mask_ref, sub, reverse=bool(d))
                    acc_ref[b, d, rows, lanes] = o
            if not has_state:
                for d in range(2):
                    sout_ref[b, d, h] = sts[d].T
    for b in range(q_ref.shape[0]):
        for lanes in head_lanes:
            o = acc_ref[b, 0, :, lanes] + acc_ref[b, 1, :, lanes]
            o = o * lax.rsqrt(jnp.mean(o * o, axis=-1, keepdims=True) + NORM_EPS)
            o_ref[b, :, lanes] = (o * g_ref[:, lanes] * gate_ref[b, :, lanes]).astype(BF16)


def _hgrn_call(rq, ri, xf, xb, gate, lbf, lbb, rnn_g, masks, s0):
    b, l, w = rq.shape
    n_blocks = l // HGRN_ROWS
    nb = SHORT_SEQS_PER_STEP if n_blocks == 1 and b % SHORT_SEQS_PER_STEP == 0 else 1
    seq = lambda: pl.BlockSpec((nb, l, w), lambda i: (i, 0, 0))
    const = lambda a: pl.BlockSpec(a.shape, lambda i: (0,) * a.ndim)
    state_spec = pl.BlockSpec((nb, 2, N_HEADS, HEAD_DIM, HEAD_DIM), lambda i: (i, 0, 0, 0, 0))
    in_specs = [seq(), seq(), seq(), seq(), seq(), const(lbf), const(lbb), const(rnn_g), const(masks)]
    args = [rq, ri, xf, xb, gate, lbf, lbb, rnn_g, masks]
    out_specs = [seq()]
    out_shape = [jax.ShapeDtypeStruct((b, l, w), BF16)]
    if s0 is not None:
        in_specs.append(state_spec)
        args.append(s0)
    else:
        out_specs.append(state_spec)
        out_shape.append(jax.ShapeDtypeStruct((b, 2, N_HEADS, HEAD_DIM, HEAD_DIM), F32))
    return pl.pallas_call(
        functools.partial(_hgrn_kernel, n_blocks=n_blocks, has_state=s0 is not None),
        grid=(b // nb,),
        in_specs=in_specs,
        out_specs=out_specs,
        out_shape=out_shape,
        scratch_shapes=[pltpu.VMEM((nb, 2, l, w), F32)],
        compiler_params=_params(("arbitrary",)),
        name="hgrn_lat" if s0 is not None else "hgrn_ctx",
    )(*args)


def _tail_kernel(att_ref, rnn_ref, x_ref, mod_ref, wo_ref, g1_ref, b1_ref, wu_ref, cw_ref, cb_ref, wd_ref,
                 g2_ref, b2_ref, o_ref, x1_ref, xm2_ref, hid_ref, *, seq_len):
    rows = x_ref.shape[0]
    m = mod_ref[0]
    slabs = [slice(r, r + NORM_ROWS) for r in range(0, rows, NORM_ROWS)]
    mixes = [_dot(att_ref[sl, :], wo_ref[0:GROUP_W, :]) + _dot(rnn_ref[sl, :], wo_ref[GROUP_W:2 * GROUP_W, :])
             for sl in slabs]
    for sl, mix in zip(slabs, mixes):
        x1 = _layer_norm(DEEPNORM_ALPHA * x_ref[sl, :] + m[2:3, :] * mix, g1_ref[...], b1_ref[...])
        x1_ref[sl, :] = x1
        xm2_ref[sl, :] = (x1 * (1.0 + m[4:5, :]) + m[3:4, :]).astype(BF16)

    sub = lax.broadcasted_iota(jnp.int32, (8, FF_TILE), 0)

    def shifted(h, shift, edge_row):
        r = pltpu.roll(h, shift, axis=0)
        parts = []
        for s in range(rows // seq_len):
            edge = s * seq_len + (edge_row // 8) * 8
            fixed = jnp.where(sub == edge_row % 8, 0.0, r[edge:edge + 8])
            parts += [r[s * seq_len:edge], fixed, r[edge + 8:(s + 1) * seq_len]]
        return jnp.concatenate([p for p in parts if p.shape[0]], axis=0)

    def conv(h, cols):
        prev = shifted(h, 1, 0)
        nxt = shifted(h, rows - 1, seq_len - 1)
        return prev * cw_ref[0:1, cols] + h * cw_ref[1:2, cols] + nxt * cw_ref[2:3, cols] + cb_ref[:, cols]

    for j in range(D_FF // FF_TILE):
        cols_a = slice(j * FF_TILE, (j + 1) * FF_TILE)
        cols_u = slice(D_FF + j * FF_TILE, D_FF + (j + 1) * FF_TILE)
        a = conv(_dot(xm2_ref[...], wu_ref[:, cols_a]), cols_a)
        u = conv(_dot(xm2_ref[...], wu_ref[:, cols_u]), cols_u)
        hid_ref[:, cols_a] = (_silu(a) * u).astype(BF16)
    ffns = [_dot(hid_ref[sl, :], wd_ref[...]) for sl in slabs]
    for sl, ffn in zip(slabs, ffns):
        y = DEEPNORM_ALPHA * x1_ref[sl, :] + m[5:6, :] * ffn
        o_ref[sl, :] = _layer_norm(y, g2_ref[...], b2_ref[...])


def _tail_call(att, rnn, x2d, mod3, w_out, ln1_g, ln1_b, w_up, conv_w, conv_b, w_down, ln2_g, ln2_b,
               *, latent, seq_len):
    t = x2d.shape[0]
    rb = ROWS_FFN
    assert rb % seq_len == 0

    def mod_idx(i):
        return ((1 + i * rb // seq_len) if latent else 0, 0, 0)

    row_spec = lambda w: pl.BlockSpec((rb, w), lambda i: (i, 0))
    resident = lambda a: pl.BlockSpec(a.shape, lambda i: (0, 0), pipeline_mode=pl.Buffered(1))
    return pl.pallas_call(
        functools.partial(_tail_kernel, seq_len=seq_len),
        grid=(t // rb,),
        in_specs=[row_spec(GROUP_W), row_spec(GROUP_W), row_spec(D_MODEL),
                  pl.BlockSpec((1, N_MOD, D_MODEL), mod_idx),
                  resident(w_out), resident(ln1_g), resident(ln1_b),
                  resident(w_up), resident(conv_w), resident(conv_b), resident(w_down),
                  resident(ln2_g), resident(ln2_b)],
        out_specs=row_spec(D_MODEL),
        out_shape=jax.ShapeDtypeStruct((t, D_MODEL), F32),
        scratch_shapes=[pltpu.VMEM((rb, D_MODEL), F32), pltpu.VMEM((rb, D_MODEL), BF16),
                        pltpu.VMEM((rb, D_FF), BF16)],
        compiler_params=_params(("arbitrary",)),
        name="tail_lat" if latent else "tail_ctx",
    )(att, rnn, x2d, mod3, w_out, ln1_g, ln1_b, w_up, conv_w, conv_b, w_down, ln2_g, ln2_b)


def _rope_tables(seq_len):
    quarter = QK_DIM // 4
    freqs = 1.0 / (ROPE_BASE ** (np.arange(quarter, dtype=np.float64) / quarter))
    t = np.arange(seq_len)
    ang_r = (t // GRID_W)[:, None] * freqs
    ang_c = (t % GRID_W)[:, None] * freqs
    zeros = np.zeros_like(ang_r)

    def tile(parts):
        return jnp.asarray(np.tile(np.concatenate(parts, axis=-1), (1, GROUP_W // QK_DIM)).astype(np.float32))

    cos = tile([np.cos(ang_r), np.cos(ang_r), np.cos(ang_c), np.cos(ang_c)])
    sin_a = tile([-np.sin(ang_r), zeros, -np.sin(ang_c), zeros])
    sin_b = tile([zeros, np.sin(ang_r), zeros, np.sin(ang_c)])
    return cos, sin_a, sin_b


def kernel(x_prompt, x_sample, cache_k, cache_v, state_rnn, c, c_ctx, w_ada, b_ada, w_in, lambda_q1, lambda_k1, lambda_q2, lambda_k2, lb_fwd_logits, lb_bwd_logits, att_norm_g, rnn_norm_g, w_out, ln1_g, ln1_b, w_up, conv_w, conv_b, w_down, ln2_g, ln2_b):
    assert w_ada.shape[0] == DEPTH
    bp, lp, d = x_prompt.shape
    bs, ls, _ = x_sample.shape
    past = cache_k.shape[2]

    assert 1 + bs <= COND_ROWS
    cond = jnp.zeros((COND_ROWS, d), F32).at[0].set(c_ctx).at[1:1 + bs].set(c)
    mod, w_in_b = _mod_call(cond, w_ada[0], b_ada, w_in[0])
    mod3 = mod.reshape(COND_ROWS, N_MOD, d)

    lams = (lambda_q1, lambda_k1, lambda_q2, lambda_k2)
    masks = jnp.asarray(_hgrn_masks(), dtype=BF16)
    xp2d = x_prompt.reshape(bp * lp, d)
    xs2d = x_sample.reshape(bs * ls, d)

    ctx_in = _inproj_call(xp2d, mod3, w_in_b, None, latent=False, seq_len=lp)
    lat_in = _inproj_call(xs2d, mod3, w_in_b, _rope_tables(ls), latent=True, seq_len=ls)
    k_raw, v_raw = ctx_in[8:]
    cache = (jnp.transpose(cache_k, (0, 1, 3, 4, 5, 2)).reshape(bs, GROUP_W, past),
             cache_v.reshape(bs, past * N_HEADS, HEAD_DIM))

    def heads(proj, b, l):
        return tuple(o.reshape(b, l, GROUP_W) for o in proj[:8])

    q, k, v, rq, xf, xb, ri, rg = heads(lat_in, bs, ls)
    att_s, w_out_b, w_up_b, w_down_b = _attn_call(q, k, v, cache, lams, att_norm_g,
                                                  casts=(w_out[0], w_up[0], w_down[0]))
    rnn_s, = _hgrn_call(rq, ri, xf, xb, rg, lb_fwd_logits, lb_bwd_logits, rnn_norm_g, masks,
                        state_rnn.reshape(bs, 2, N_HEADS, HEAD_DIM, HEAD_DIM))
    q, k, v, rq, xf, xb, ri, rg = heads(ctx_in, bp, lp)
    att_p, = _attn_call(q, k, v, None, lams, att_norm_g)
    rnn_p, s_new = _hgrn_call(rq, ri, xf, xb, rg, lb_fwd_logits, lb_bwd_logits, rnn_norm_g, masks, None)

    def tail(att, rnn, x2d, b, l, latent):
        y = _tail_call(att.reshape(b * l, GROUP_W), rnn.reshape(b * l, GROUP_W), x2d, mod3, w_out_b, ln1_g,
                       ln1_b, w_up_b, conv_w[0], conv_b, w_down_b, ln2_g, ln2_b, latent=latent, seq_len=l)
        return y.reshape(b, l, d)

    y_p = tail(att_p, rnn_p, xp2d, bp, lp, False)
    y_s = tail(att_s, rnn_s, xs2d, bs, ls, True)

    new_cache_k = k_raw.reshape(bp, DEPTH, lp, N_HEADS, 2, QK_DIM)
    new_cache_v = v_raw.reshape(bp, DEPTH, lp, N_HEADS, HEAD_DIM)
    new_state = s_new.reshape(bp, DEPTH, 2, N_HEADS, HEAD_DIM, HEAD_DIM)
    return (y_p, y_s, new_cache_k, new_cache_v, new_state)
```

```python
import functools
import math

import jax
import jax.numpy as jnp
import numpy as np
from jax import lax
from jax.experimental import pallas as pl
from jax.experimental.pallas import tpu as pltpu

D_MODEL = 1024
GRID_W = 64
N_HEADS = 4
HEAD_DIM = 128
QK_DIM = 64
GROUP_W = 512
N_GROUPS = 8
D_FF = 2816
N_MOD = 6
ROPE_BASE = 10000.0
DEPTH = 1
DEEPNORM_ALPHA = (2.0 * DEPTH) ** 0.25
NORM_EPS = 1e-5
LAM_INIT = 0.8 - 0.6 * math.exp(-0.3 * 0)
LOG2_E = math.log2(math.e)

V7X_VMEM_BYTES = 64 * 1024 * 1024
VMEM_LIMIT = V7X_VMEM_BYTES * 15 // 16
COND_ROWS = 16

MOD_TILE = 768
ROWS_PROJ = 512
PROJ_SLAB = 256
ROWS_FFN = 1024
FF_TILE = 256
NORM_ROWS = 256
Q_ROWS = 512
SHORT_SEQS_PER_STEP = 4
HGRN_ROWS = 256
HGRN_LEVELS = int(math.log2(HGRN_ROWS))

F32 = jnp.float32
BF16 = jnp.bfloat16


def _params(semantics):
    return pltpu.CompilerParams(dimension_semantics=semantics, vmem_limit_bytes=VMEM_LIMIT)


def _dot(a, b):
    return jnp.dot(a, b, preferred_element_type=F32)


def _dot_nt(a, b):
    return lax.dot_general(a, b, (((1,), (1,)), ((), ())), preferred_element_type=F32)


def _dot_tn(a, b):
    return lax.dot_general(a, b, (((0,), (0,)), ((), ())), preferred_element_type=F32)


def _silu(x):
    return x * jax.nn.sigmoid(x)


def _layer_norm(y, g, b):
    mu = jnp.mean(y, axis=-1, keepdims=True)
    d = y - mu
    var = jnp.mean(d * d, axis=-1, keepdims=True)
    return d * lax.rsqrt(var + NORM_EPS) * g + b


def _mod_kernel(c_ref, w_ref, b_ref, win_ref, o_ref, win_out_ref):
    s = _silu(c_ref[...]).astype(BF16)
    o_ref[...] = _dot(s, w_ref[...].astype(BF16)) + b_ref[...]
    win_out_ref[...] = win_ref[...].astype(BF16)


def _mod_call(cond, w_ada, b_ada, w_in):
    n, d = cond.shape
    cols = w_ada.shape[1]
    tile = MOD_TILE
    n_steps = cols // tile
    slab = pl.BlockSpec((w_in.shape[0] // n_steps, w_in.shape[1]), lambda j: (j, 0))
    return pl.pallas_call(
        _mod_kernel,
        grid=(n_steps,),
        in_specs=[pl.BlockSpec((n, d), lambda j: (0, 0)),
                  pl.BlockSpec((d, tile), lambda j: (0, j)),
                  pl.BlockSpec((1, tile), lambda j: (0, j)),
                  slab],
        out_specs=[pl.BlockSpec((n, tile), lambda j: (0, j)), slab],
        out_shape=[jax.ShapeDtypeStruct((n, cols), F32), jax.ShapeDtypeStruct(w_in.shape, BF16)],
        compiler_params=_params(("arbitrary",)),
        name="mod",
    )(cond, w_ada, b_ada, w_in)


def _rope(x, cos_ref, sin_a_ref, sin_b_ref):
    parts = []
    for c in range(GROUP_W // 128):
        sl = slice(c * 128, (c + 1) * 128)
        xc = x[:, sl]
        parts.append(xc * cos_ref[:, sl]
                     + pltpu.roll(xc, 128 - 16, axis=1) * sin_a_ref[:, sl]
                     + pltpu.roll(xc, 16, axis=1) * sin_b_ref[:, sl])
    return jnp.concatenate(parts, axis=1)


def _inproj_kernel(*refs, latent):
    if latent:
        (x_ref, mod_ref, w_ref, cos_ref, sin_a_ref, sin_b_ref,
         q_ref, k_ref, v_ref, rq_ref, xf_ref, xb_ref, ri_ref, rg_ref) = refs
    else:
        (x_ref, mod_ref, w_ref,
         q_ref, k_ref, v_ref, rq_ref, xf_ref, xb_ref, ri_ref, rg_ref, kraw_ref, vraw_ref) = refs
    m = mod_ref[0]
    for r in range(0, x_ref.shape[0], PROJ_SLAB):
        sl = slice(r, r + PROJ_SLAB)
        xm = (x_ref[sl, :] * (1.0 + m[1:2, :]) + m[0:1, :]).astype(BF16)

        def proj(g, xm=xm):
            return _dot(xm, w_ref[:, g * GROUP_W:(g + 1) * GROUP_W])

        aq = proj(0)
        ak = proj(1)
        av = proj(2)
        if latent:
            tabs = [t.at[sl, :] for t in (cos_ref, sin_a_ref, sin_b_ref)]
            aq = _rope(aq, *tabs)
            ak = _rope(ak, *tabs)
        else:
            for j in range(2 * N_HEADS):
                kraw_ref[pl.ds(r * 2 * N_HEADS + j, PROJ_SLAB, stride=2 * N_HEADS), :] = (
                    ak[:, j * QK_DIM:(j + 1) * QK_DIM])
            for h in range(N_HEADS):
                vraw_ref[pl.ds(r * N_HEADS + h, PROJ_SLAB, stride=N_HEADS), :] = (
                    av[:, h * HEAD_DIM:(h + 1) * HEAD_DIM])
        q_ref[sl, :] = (aq * (QK_DIM ** -0.5 * LOG2_E)).astype(BF16)
        k_ref[sl, :] = ak.astype(BF16)
        v_ref[sl, :] = av.astype(BF16)
        rq_ref[sl, :] = proj(3)
        xf_ref[sl, :] = proj(4)
        xb_ref[sl, :] = proj(5)
        ri_ref[sl, :] = proj(6).astype(BF16)
        rg_ref[sl, :] = _silu(proj(7))


def _inproj_call(x2d, mod3, w_in, rope_tabs, *, latent, seq_len):
    t = x2d.shape[0]
    rb = ROWS_PROJ
    blocks_per_seq = seq_len // rb if latent else 1

    def mod_idx(i):
        return ((1 + i // blocks_per_seq) if latent else 0, 0, 0)

    row_spec = lambda w: pl.BlockSpec((rb, w), lambda i: (i, 0))
    in_specs = [row_spec(D_MODEL),
                pl.BlockSpec((1, N_MOD, D_MODEL), mod_idx),
                pl.BlockSpec(w_in.shape, lambda i: (0, 0))]
    args = [x2d, mod3, w_in]
    if latent:
        in_specs += [pl.BlockSpec((rb, GROUP_W), lambda i: (i % blocks_per_seq, 0))] * 3
        args += list(rope_tabs)
    dts = [BF16, BF16, BF16, F32, F32, F32, BF16, F32]
    out_specs = [row_spec(GROUP_W) for _ in dts]
    out_shape = [jax.ShapeDtypeStruct((t, GROUP_W), dt) for dt in dts]
    if not latent:
        out_specs += [pl.BlockSpec((rb * 2 * N_HEADS, QK_DIM), lambda i: (i, 0)),
                      pl.BlockSpec((rb * N_HEADS, HEAD_DIM), lambda i: (i, 0))]
        out_shape += [jax.ShapeDtypeStruct((t * 2 * N_HEADS, QK_DIM), F32),
                      jax.ShapeDtypeStruct((t * N_HEADS, HEAD_DIM), F32)]
    return pl.pallas_call(
        functools.partial(_inproj_kernel, latent=latent),
        grid=(t // rb,),
        in_specs=in_specs,
        out_specs=out_specs,
        out_shape=out_shape,
        compiler_params=_params(("arbitrary",)),
        name="inproj_lat" if latent else "inproj_ctx",
    )(*args)


def _attn_kernel(*refs, has_cache, n_casts):
    if has_cache:
        (q_ref, k_ref, v_ref, ck_ref, cv_ref, lq1, lk1, lq2, lk2, g_ref) = refs[:10]
        o_ref = refs[10 + n_casts]
        for src, dst in zip(refs[10:10 + n_casts], refs[11 + n_casts:]):
            dst[...] = src[...].astype(BF16)
    else:
        (q_ref, k_ref, v_ref, lq1, lk1, lq2, lk2, g_ref, o_ref) = refs
    lam = (jnp.exp(jnp.sum(lq1[...] * lk1[...], axis=-1, keepdims=True))
           - jnp.exp(jnp.sum(lq2[...] * lk2[...], axis=-1, keepdims=True)) + LAM_INIT)
    qb = q_ref.shape[1]
    lane = lax.broadcasted_iota(jnp.int32, (qb, HEAD_DIM), 1)
    first_map = lane < QK_DIM
    zero = jnp.zeros((), BF16)

    units = [(b, h) for b in range(q_ref.shape[0]) for h in range(N_HEADS)]

    def scores(u):
        b, h = units[u]
        sl = slice(h * HEAD_DIM, (h + 1) * HEAD_DIM)
        qh = q_ref[b, :, sl]
        qq = jnp.concatenate([jnp.where(first_map, qh, zero), jnp.where(first_map, zero, qh)], axis=0)
        s_n = _dot_nt(k_ref[b, :, sl], qq)
        s_c = None
        if has_cache:
            s_c = _dot_nt(ck_ref[b, sl, :].astype(BF16).T, qq)
        return s_n, s_c

    ahead = 2
    pending = [scores(u) for u in range(ahead)]
    for u, (b, h) in enumerate(units):
        sl = slice(h * HEAD_DIM, (h + 1) * HEAD_DIM)
        s_n, s_c = pending.pop(0)
        if u + ahead < len(units):
            pending.append(scores(u + ahead))
        mx = jnp.max(s_n, axis=0, keepdims=True)
        if has_cache:
            mx = jnp.maximum(mx, jnp.max(s_c, axis=0, keepdims=True))
        e_n = jnp.exp2(s_n - mx)
        den = jnp.sum(e_n, axis=0, keepdims=True)
        ev = _dot_tn(v_ref[b, :, sl], e_n.astype(BF16))
        if has_cache:
            e_c = jnp.exp2(s_c - mx)
            den = den + jnp.sum(e_c, axis=0, keepdims=True)
            vc = cv_ref[b, pl.ds(h, e_c.shape[0], stride=N_HEADS), :]
            ev = ev + _dot_tn(vc.astype(BF16), e_c.astype(BF16))
        inv = 1.0 / den
        o = ev[:, :qb] * inv[:, :qb] - ev[:, qb:] * (inv[:, qb:] * lam)
        o = o * lax.rsqrt(jnp.mean(o * o, axis=0, keepdims=True) + NORM_EPS)
        o_ref[b, :, sl] = (o.T * g_ref[:, sl] * (1.0 - LAM_INIT)).astype(BF16)


def _attn_call(q, k, v, cache, lams, att_g, casts=()):
    b, l, _ = q.shape
    qb = min(Q_ROWS, l)
    nq = l // qb
    nb = SHORT_SEQS_PER_STEP if nq == 1 and b % SHORT_SEQS_PER_STEP == 0 else 1
    steps = b // nb
    slab = lambda a: pl.BlockSpec((a.shape[0] // (steps * nq), a.shape[1]), lambda i, j: (i * nq + j, 0))
    full = lambda a: pl.BlockSpec((nb,) + a.shape[1:], lambda i, j: (i,) + (0,) * (a.ndim - 1))
    in_specs = [pl.BlockSpec((nb, qb, GROUP_W), lambda i, j: (i, j, 0)), full(k), full(v)]
    args = [q, k, v]
    if cache is not None:
        in_specs += [full(cache[0]), full(cache[1])]
        args += list(cache)
    in_specs += [pl.BlockSpec((1, QK_DIM), lambda i, j: (0, 0))] * 4
    in_specs += [pl.BlockSpec((1, GROUP_W), lambda i, j: (0, 0))]
    args += list(lams) + [att_g] + list(casts)
    in_specs += [slab(a) for a in casts]
    return pl.pallas_call(
        functools.partial(_attn_kernel, has_cache=cache is not None, n_casts=len(casts)),
        grid=(steps, nq),
        in_specs=in_specs,
        out_specs=[pl.BlockSpec((nb, qb, GROUP_W), lambda i, j: (i, j, 0))] + [slab(a) for a in casts],
        out_shape=[jax.ShapeDtypeStruct((b, l, GROUP_W), BF16)] + [jax.ShapeDtypeStruct(a.shape, BF16) for a in casts],
        compiler_params=_params(("arbitrary", "arbitrary")),
        name="attn_lat" if cache is not None else "attn_ctx",
    )(*args)


HGRN_DIAG = 128
HGRN_MASKED_LEVELS = 5
HGRN_VREG_LEVELS = 3


def _hgrn_masks():
    r = np.arange(HGRN_DIAG)[:, None]
    c = np.arange(HGRN_DIAG)[None, :]
    levels = []
    for lv in range(HGRN_MASKED_LEVELS):
        same = (r >> (lv + 1)) == (c >> (lv + 1))
        levels.append(same & (((r >> lv) & 1) == 1) & (((c >> lv) & 1) == 0))
    fwd = [r == c] + levels
    bwd = [m.T for m in levels]
    return np.stack(fwd + bwd).astype(np.float32)


def _hgrn_block(q, k, g, v, st, mask_ref, sub, *, reverse):
    n = HGRN_ROWS
    n_diag = n // HGRN_DIAG

    def diag_scores(qe, ke, m):
        return [_dot_nt(qe[i * HGRN_DIAG:(i + 1) * HGRN_DIAG], ke[i * HGRN_DIAG:(i + 1) * HGRN_DIAG]).astype(BF16) * m
                for i in range(n_diag)]

    def level_mask(lv):
        return mask_ref[(1 + HGRN_MASKED_LEVELS + lv) if reverse else (1 + lv)]

    g3 = g.reshape(n // 8, 8, HEAD_DIM)
    zero3 = jnp.zeros_like(g3)
    pre3, suf3 = (zero3, g3) if reverse else (g3, zero3)
    tot3 = g3
    qb, kb = q.astype(BF16), k.astype(BF16)
    acc = diag_scores(qb, kb, mask_ref[0])
    for lv in range(HGRN_VREG_LEVELS):
        b = 1 << lv
        upper = ((sub >> lv) & 1) == 1
        e = jnp.exp2(jnp.where(upper, pre3, suf3)).reshape(n, HEAD_DIM)
        eb = e.astype(BF16)
        part = diag_scores(qb * eb, kb * eb, level_mask(lv))
        acc = [a + p for a, p in zip(acc, part)]
        sib = jnp.where(upper, pltpu.roll(tot3, b, axis=1), pltpu.roll(tot3, 8 - b, axis=1))
        pre3 = pre3 + jnp.where(upper, sib, 0.0)
        suf3 = suf3 + jnp.where(upper, 0.0, sib)
        tot3 = tot3 + sib

    pieces = lambda x3: [x3[i] for i in range(n // 8)]
    pre8, suf8, tot8 = pieces(pre3), pieces(suf3), pieces(tot3)
    rows = lambda xs, lo, hi: jnp.concatenate(xs[lo // 8:hi // 8], axis=0) if hi - lo > 8 else xs[lo // 8]
    big_scores = {}
    for lv in range(HGRN_VREG_LEVELS, HGRN_LEVELS):
        b = 1 << lv
        pb = b // 8
        if lv < HGRN_MASKED_LEVELS:
            u = jnp.concatenate([(pre8 if (i // pb) % 2 else suf8)[i] for i in range(n // 8)], axis=0)
            e = jnp.exp2(u)
            eb = e.astype(BF16)
            part = diag_scores(qb * eb, kb * eb, level_mask(lv))
            acc = [a + p for a, p in zip(acc, part)]
        else:
            for j in range(n // (2 * b)):
                lo, mid, hi = 2 * b * j, 2 * b * j + b, 2 * b * (j + 1)
                e_lo = jnp.exp2(rows(suf8, lo, mid))
                e_up = jnp.exp2(rows(pre8, mid, hi))
                x_lo = (qb if reverse else kb)[lo:mid] * e_lo.astype(BF16)
                x_up = (kb if reverse else qb)[mid:hi] * e_up.astype(BF16)
                big_scores[lv, j] = _dot_nt(x_lo, x_up) if reverse else _dot_nt(x_up, x_lo)
        for j in range(n // (2 * b)):
            lo_p, mid_p, hi_p = 2 * pb * j, 2 * pb * j + pb, 2 * pb * (j + 1)
            t_lo, t_up = tot8[lo_p], tot8[mid_p]
            t_new = t_lo + t_up
            for i in range(lo_p, mid_p):
                suf8[i] = suf8[i] + t_up
                tot8[i] = t_new
            for i in range(mid_p, hi_p):
                pre8[i] = pre8[i] + t_lo
                tot8[i] = t_new
    pre = jnp.concatenate(pre8, axis=0)
    suf = jnp.concatenate(suf8, axis=0)
    q_dec, k_dec = (suf, pre) if reverse else (pre, suf)
    q_in = qb * jnp.exp2(q_dec).astype(BF16)
    k_out = kb * jnp.exp2(k_dec).astype(BF16)
    decay = jnp.exp2(tot8[0][0:1, :])

    o_inter = _dot_nt(q_in, st.astype(BF16))
    ds = _dot_tn(v, k_out)
    o_diag = [_dot(acc[i].astype(BF16), v[i * HGRN_DIAG:(i + 1) * HGRN_DIAG]) for i in range(n_diag)]
    o_big = {}
    for (lv, j), a in big_scores.items():
        lo, mid, hi = (2 * j) << lv, (2 * j + 1) << lv, (2 * j + 2) << lv
        o_big[lv, j] = _dot(a.astype(BF16), v[mid:hi] if reverse else v[lo:mid])
    st_new = st * decay + ds
    o = jnp.concatenate(o_diag, axis=0) + o_inter
    for lv in range(HGRN_MASKED_LEVELS, HGRN_LEVELS):
        contrib = []
        for j in range(n >> (lv + 1)):
            zeros = jnp.zeros((1 << lv, HEAD_DIM), F32)
            contrib += [o_big[lv, j], zeros] if reverse else [zeros, o_big[lv, j]]
        o = o + jnp.concatenate(contrib, axis=0)
    return o, st_new


def _hgrn_kernel(*refs, n_blocks, has_state):
    if has_state:
        (q_ref, v_ref, xf_ref, xb_ref, gate_ref, lbf_ref, lbb_ref, g_ref, mask_ref, s0_ref,
         o_ref, acc_ref) = refs
    else:
        (q_ref, v_ref, xf_ref, xb_ref, gate_ref, lbf_ref, lbb_ref, g_ref, mask_ref,
         o_ref, sout_ref, acc_ref) = refs
    n = HGRN_ROWS
    sub = lax.broadcasted_iota(jnp.int32, (n // 8, 8, HEAD_DIM), 1)
    head_lanes = [slice(h * HEAD_DIM, (h + 1) * HEAD_DIM) for h in range(N_HEADS)]

    def lower_bound(ref, lanes):
        l0, l1 = ref[0:1, lanes], ref[1:2, lanes]
        mx = jnp.maximum(l0, l1)
        e0, e1 = jnp.exp(l0 - mx), jnp.exp(l1 - mx)
        return e0 / (e0 + e1)

    x_refs = (xf_ref, xb_ref)
    for b in range(q_ref.shape[0]):
        for h, lanes in enumerate(head_lanes):
            lbs = [lower_bound(lbf_ref, lanes), lower_bound(lbb_ref, lanes)]
            sts = [s0_ref[b, d, h].T if has_state else jnp.zeros((HEAD_DIM, HEAD_DIM), F32) for d in range(2)]
            for j in range(n_blocks):
                for d in range(2):
                    blk = (n_blocks - 1 - j) if d else j
                    rows = slice(blk * n, (blk + 1) * n)
                    f = lbs[d] + (1.0 - lbs[d]) * jax.nn.sigmoid(x_refs[d][b, rows, lanes])
                    o, sts[d] = _hgrn_block(q_ref[b, rows, lanes], 1.0 - f, jnp.log2(f), v_ref[b, rows, lanes],
                                            sts[d], mask_ref, sub, reverse=bool(d))
                    acc_ref[b, d, rows, lanes] = o
            if not has_state:
                for d in range(2):
                    sout_ref[b, d, h] = sts[d].T
    for b in range(q_ref.shape[0]):
        for lanes in head_lanes:
            o = acc_ref[b, 0, :, lanes] + acc_ref[b, 1, :, lanes]
            o = o * lax.rsqrt(jnp.mean(o * o, axis=-1, keepdims=True) + NORM_EPS)
            o_ref[b, :, lanes] = (o * g_ref[:, lanes] * gate_ref[b, :, lanes]).astype(BF16)


def _hgrn_call(rq, ri, xf, xb, gate, lbf, lbb, rnn_g, masks, s0):
    b, l, w = rq.shape
    n_blocks = l // HGRN_ROWS
    nb = SHORT_SEQS_PER_STEP if n_blocks == 1 and b % SHORT_SEQS_PER_STEP == 0 else 1
    seq = lambda: pl.BlockSpec((nb, l, w), lambda i: (i, 0, 0))
    const = lambda a: pl.BlockSpec(a.shape, lambda i: (0,) * a.ndim)
    state_spec = pl.BlockSpec((nb, 2, N_HEADS, HEAD_DIM, HEAD_DIM), lambda i: (i, 0, 0, 0, 0))
    in_specs = [seq(), seq(), seq(), seq(), seq(), const(lbf), const(lbb), const(rnn_g), const(masks)]
    args = [rq, ri, xf, xb, gate, lbf, lbb, rnn_g, masks]
    out_specs = [seq()]
    out_shape = [jax.ShapeDtypeStruct((b, l, w), BF16)]
    if s0 is not None:
        in_specs.append(state_spec)
        args.append(s0)
    else:
        out_specs.append(state_spec)
        out_shape.append(jax.ShapeDtypeStruct((b, 2, N_HEADS, HEAD_DIM, HEAD_DIM), F32))
    return pl.pallas_call(
        functools.partial(_hgrn_kernel, n_blocks=n_blocks, has_state=s0 is not None),
        grid=(b // nb,),
        in_specs=in_specs,
        out_specs=out_specs,
        out_shape=out_shape,
        scratch_shapes=[pltpu.VMEM((nb, 2, l, w), F32)],
        compiler_params=_params(("arbitrary",)),
        name="hgrn_lat" if s0 is not None else "hgrn_ctx",
    )(*args)


def _tail_kernel(att_ref, rnn_ref, x_ref, mod_ref, wo_ref, g1_ref, b1_ref, wu_ref, cw_ref, cb_ref, wd_ref,
                 g2_ref, b2_ref, o_ref, x1_ref, xm2_ref, hid_ref, *, seq_len):
    rows = x_ref.shape[0]
    m = mod_ref[0]
    slabs = [slice(r, r + NORM_ROWS) for r in range(0, rows, NORM_ROWS)]
    mixes = [_dot(att_ref[sl, :], wo_ref[0:GROUP_W, :]) + _dot(rnn_ref[sl, :], wo_ref[GROUP_W:2 * GROUP_W, :])
             for sl in slabs]
    for sl, mix in zip(slabs, mixes):
        x1 = _layer_norm(DEEPNORM_ALPHA * x_ref[sl, :] + m[2:3, :] * mix, g1_ref[...], b1_ref[...])
        x1_ref[sl, :] = x1
        xm2_ref[sl, :] = (x1 * (1.0 + m[4:5, :]) + m[3:4, :]).astype(BF16)

    sub = lax.broadcasted_iota(jnp.int32, (8, FF_TILE), 0)

    def shifted(h, shift, edge_row):
        r = pltpu.roll(h, shift, axis=0)
        parts = []
        for s in range(rows // seq_len):
            edge = s * seq_len + (edge_row // 8) * 8
            fixed = jnp.where(sub == edge_row % 8, 0.0, r[edge:edge + 8])
            parts += [r[s * seq_len:edge], fixed, r[edge + 8:(s + 1) * seq_len]]
        return jnp.concatenate([p for p in parts if p.shape[0]], axis=0)

    def conv(h, cols):
        prev = shifted(h, 1, 0)
        nxt = shifted(h, rows - 1, seq_len - 1)
        return prev * cw_ref[0:1, cols] + h * cw_ref[1:2, cols] + nxt * cw_ref[2:3, cols] + cb_ref[:, cols]

    for j in range(D_FF // FF_TILE):
        cols_a = slice(j * FF_TILE, (j + 1) * FF_TILE)
        cols_u = slice(D_FF + j * FF_TILE, D_FF + (j + 1) * FF_TILE)
        a = conv(_dot(xm2_ref[...], wu_ref[:, cols_a]), cols_a)
        u = conv(_dot(xm2_ref[...], wu_ref[:, cols_u]), cols_u)
        hid_ref[:, cols_a] = (_silu(a) * u).astype(BF16)
    ffns = [_dot(hid_ref[sl, :], wd_ref[...]) for sl in slabs]
    for sl, ffn in zip(slabs, ffns):
        y = DEEPNORM_ALPHA * x1_ref[sl, :] + m[5:6, :] * ffn
        o_ref[sl, :] = _layer_norm(y, g2_ref[...], b2_ref[...])


def _tail_call(att, rnn, x2d, mod3, w_out, ln1_g, ln1_b, w_up, conv_w, conv_b, w_down, ln2_g, ln2_b,
               *, latent, seq_len):
    t = x2d.shape[0]
    rb = ROWS_FFN
    assert rb % seq_len == 0

    def mod_idx(i):
        return ((1 + i * rb // seq_len) if latent else 0, 0, 0)

    row_spec = lambda w: pl.BlockSpec((rb, w), lambda i: (i, 0))
    resident = lambda a: pl.BlockSpec(a.shape, lambda i: (0, 0), pipeline_mode=pl.Buffered(1))
    return pl.pallas_call(
        functools.partial(_tail_kernel, seq_len=seq_len),
        grid=(t // rb,),
        in_specs=[row_spec(GROUP_W), row_spec(GROUP_W), row_spec(D_MODEL),
                  pl.BlockSpec((1, N_MOD, D_MODEL), mod_idx),
                  resident(w_out), resident(ln1_g), resident(ln1_b),
                  resident(w_up), resident(conv_w), resident(conv_b), resident(w_down),
                  resident(ln2_g), resident(ln2_b)],
        out_specs=row_spec(D_MODEL),
        out_shape=jax.ShapeDtypeStruct((t, D_MODEL), F32),
        scratch_shapes=[pltpu.VMEM((rb, D_MODEL), F32), pltpu.VMEM((rb, D_MODEL), BF16),
                        pltpu.VMEM((rb, D_FF), BF16)],
        compiler_params=_params(("arbitrary",)),
        name="tail_lat" if latent else "tail_ctx",
    )(att, rnn, x2d, mod3, w_out, ln1_g, ln1_b, w_up, conv_w, conv_b, w_down, ln2_g, ln2_b)


def _rope_tables(seq_len):
    quarter = QK_DIM // 4
    freqs = 1.0 / (ROPE_BASE ** (np.arange(quarter, dtype=np.float64) / quarter))
    t = np.arange(seq_len)
    ang_r = (t // GRID_W)[:, None] * freqs
    ang_c = (t % GRID_W)[:, None] * freqs
    zeros = np.zeros_like(ang_r)

    def tile(parts):
        return jnp.asarray(np.tile(np.concatenate(parts, axis=-1), (1, GROUP_W // QK_DIM)).astype(np.float32))

    cos = tile([np.cos(ang_r), np.cos(ang_r), np.cos(ang_c), np.cos(ang_c)])
    sin_a = tile([-np.sin(ang_r), zeros, -np.sin(ang_c), zeros])
    sin_b = tile([zeros, np.sin(ang_r), zeros, np.sin(ang_c)])
    return cos, sin_a, sin_b


def kernel(x_prompt, x_sample, cache_k, cache_v, state_rnn, c, c_ctx, w_ada, b_ada, w_in, lambda_q1, lambda_k1, lambda_q2, lambda_k2, lb_fwd_logits, lb_bwd_logits, att_norm_g, rnn_norm_g, w_out, ln1_g, ln1_b, w_up, conv_w, conv_b, w_down, ln2_g, ln2_b):
    assert w_ada.shape[0] == DEPTH
    bp, lp, d = x_prompt.shape
    bs, ls, _ = x_sample.shape
    past = cache_k.shape[2]

    assert 1 + bs <= COND_ROWS
    cond = jnp.zeros((COND_ROWS, d), F32).at[0].set(c_ctx).at[1:1 + bs].set(c)
    mod, w_in_b = _mod_call(cond, w_ada[0], b_ada, w_in[0])
    mod3 = mod.reshape(COND_ROWS, N_MOD, d)

    lams = (lambda_q1, lambda_k1, lambda_q2, lambda_k2)
    masks = jnp.asarray(_hgrn_masks(), dtype=BF16)
    xp2d = x_prompt.reshape(bp * lp, d)
    xs2d = x_sample.reshape(bs * ls, d)

    ctx_in = _inproj_call(xp2d, mod3, w_in_b, None, latent=False, seq_len=lp)
    lat_in = _inproj_call(xs2d, mod3, w_in_b, _rope_tables(ls), latent=True, seq_len=ls)
    k_raw, v_raw = ctx_in[8:]
    cache = (jnp.transpose(cache_k, (0, 1, 3, 4, 5, 2)).reshape(bs, GROUP_W, past),
             cache_v.reshape(bs, past * N_HEADS, HEAD_DIM))

    def heads(proj, b, l):
        return tuple(o.reshape(b, l, GROUP_W) for o in proj[:8])

    q, k, v, rq, xf, xb, ri, rg = heads(lat_in, bs, ls)
    att_s, w_out_b, w_up_b, w_down_b = _attn_call(q, k, v, cache, lams, att_norm_g,
                                                  casts=(w_out[0], w_up[0], w_down[0]))
    rnn_s, = _hgrn_call(rq, ri, xf, xb, rg, lb_fwd_logits, lb_bwd_logits, rnn_norm_g, masks,
                        state_rnn.reshape(bs, 2, N_HEADS, HEAD_DIM, HEAD_DIM))
    q, k, v, rq, xf, xb, ri, rg = heads(ctx_in, bp, lp)
    att_p, = _attn_call(q, k, v, None, lams, att_norm_g)
    rnn_p, s_new = _hgrn_call(rq, ri, xf, xb, rg, lb_fwd_logits, lb_bwd_logits, rnn_norm_g, masks, None)

    def tail(att, rnn, x2d, b, l, latent):
        y = _tail_call(att.reshape(b * l, GROUP_W), rnn.reshape(b * l, GROUP_W), x2d, mod3, w_out_b, ln1_g,
                       ln1_b, w_up_b, conv_w[0], conv_b, w_down_b, ln2_g, ln2_b, latent=latent, seq_len=l)
        return y.reshape(b, l, d)

    y_p = tail(att_p, rnn_p, xp2d, bp, lp, False)
    y_s = tail(att_s, rnn_s, xs2d, bs, ls, True)

    new_cache_k = k_raw.reshape(bp, DEPTH, lp, N_HEADS, 2, QK_DIM)
    new_cache_v = v_raw.reshape(bp, DEPTH, lp, N_HEADS, HEAD_DIM)
    new_state = s_new.reshape(bp, DEPTH, 2, N_HEADS, HEAD_DIM, HEAD_DIM)
    return (y_p, y_s, new_cache_k, new_cache_v, new_state)
```

```python
import functools
import math

import jax
import jax.numpy as jnp
import numpy as np
from jax import lax
from jax.experimental import pallas as pl
from jax.experimental.pallas import tpu as pltpu

D_MODEL = 1024
GRID_W = 64
N_HEADS = 4
HEAD_DIM = 128
QK_DIM = 64
GROUP_W = 512
N_GROUPS = 8
D_FF = 2816
N_MOD = 6
ROPE_BASE = 10000.0
DEPTH = 1
DEEPNORM_ALPHA = (2.0 * DEPTH) ** 0.25
NORM_EPS = 1e-5
LAM_INIT = 0.8 - 0.6 * math.exp(-0.3 * 0)
LOG2_E = math.log2(math.e)

V7X_VMEM_BYTES = 64 * 1024 * 1024
VMEM_LIMIT = V7X_VMEM_BYTES * 15 // 16
MOD_GROUP = 8

MOD_TILE = 768
ROWS_PROJ = 512
ROWS_FFN = 1024
FF_TILE = 256
NORM_ROWS = 256
Q_ROWS = 512
SHORT_SEQS_PER_STEP = 2
HGRN_ROWS = 256
HGRN_LEVELS = int(math.log2(HGRN_ROWS))

F32 = jnp.float32
BF16 = jnp.bfloat16


def _params(semantics):
    return pltpu.CompilerParams(dimension_semantics=semantics, vmem_limit_bytes=VMEM_LIMIT)


def _dot(a, b):
    return jnp.dot(a, b, preferred_element_type=F32)


def _dot_nt(a, b):
    return lax.dot_general(a, b, (((1,), (1,)), ((), ())), preferred_element_type=F32)


def _dot_tn(a, b):
    return lax.dot_general(a, b, (((0,), (0,)), ((), ())), preferred_element_type=F32)


def _silu(x):
    return x * jax.nn.sigmoid(x)


def _layer_norm(y, g, b):
    mu = jnp.mean(y, axis=-1, keepdims=True)
    d = y - mu
    var = jnp.mean(d * d, axis=-1, keepdims=True)
    return d * lax.rsqrt(var + NORM_EPS) * g + b


def _mod_kernel(c_ref, cctx_ref, w_ref, b_ref, win_ref, o_ref, win_out_ref):
    cond = jnp.concatenate([c_ref[...], jnp.broadcast_to(cctx_ref[...], c_ref.shape)], axis=0)
    s = _silu(cond).astype(BF16)
    o_ref[...] = _dot(s, w_ref[...].astype(BF16)) + b_ref[...]
    win_out_ref[...] = win_ref[...].astype(BF16)


def _mod_call(c, c_ctx, w_ada, b_ada, w_in):
    assert c.shape[0] == MOD_GROUP
    n, d = 2 * MOD_GROUP, c.shape[1]
    cols = w_ada.shape[1]
    tile = MOD_TILE
    n_steps = cols // tile
    slab = pl.BlockSpec((w_in.shape[0] // n_steps, w_in.shape[1]), lambda j: (j, 0))
    return pl.pallas_call(
        _mod_kernel,
        grid=(n_steps,),
        in_specs=[pl.BlockSpec(c.shape, lambda j: (0, 0)),
                  pl.BlockSpec(c_ctx.shape, lambda j: (0, 0)),
                  pl.BlockSpec((d, tile), lambda j: (0, j)),
                  pl.BlockSpec((1, tile), lambda j: (0, j)),
                  slab],
        out_specs=[pl.BlockSpec((n, tile), lambda j: (0, j)), slab],
        out_shape=[jax.ShapeDtypeStruct((n, cols), F32), jax.ShapeDtypeStruct(w_in.shape, BF16)],
        compiler_params=_params(("arbitrary",)),
        name="mod",
    )(c, c_ctx, w_ada, b_ada, w_in)


def _mod_vector(mod_ref, k, blocks_per_row):
    row = 0 if blocks_per_row is None else pl.program_id(0) // blocks_per_row
    return mod_ref[pl.ds(row, 1), k * D_MODEL:(k + 1) * D_MODEL]


def _rope(x, cos_ref, sin_a_ref, sin_b_ref):
    parts = []
    for c in range(GROUP_W // 128):
        sl = slice(c * 128, (c + 1) * 128)
        xc = x[:, sl]
        parts.append(xc * cos_ref[:, sl]
                     + pltpu.roll(xc, 128 - 16, axis=1) * sin_a_ref[:, sl]
                     + pltpu.roll(xc, 16, axis=1) * sin_b_ref[:, sl])
    return jnp.concatenate(parts, axis=1)


def _inproj_kernel(*refs, latent, blocks_per_row):
    if latent:
        (x_ref, mod_ref, w_ref, cos_ref, sin_a_ref, sin_b_ref,
         q_ref, k_ref, v_ref, rq_ref, xf_ref, xb_ref, ri_ref, rg_ref) = refs
    else:
        (x_ref, mod_ref, w_ref,
         q_ref, k_ref, v_ref, rq_ref, xf_ref, xb_ref, ri_ref, rg_ref, kraw_ref, vraw_ref) = refs
    shift1, scale1 = (_mod_vector(mod_ref, k, blocks_per_row) for k in (0, 1))
    xm = (x_ref[...] * (1.0 + scale1) + shift1).astype(BF16)

    def proj(g):
        return _dot(xm, w_ref[:, g * GROUP_W:(g + 1) * GROUP_W])

    aq = proj(0)
    ak = proj(1)
    av = proj(2)
    if latent:
        aq = _rope(aq, cos_ref, sin_a_ref, sin_b_ref)
        ak = _rope(ak, cos_ref, sin_a_ref, sin_b_ref)
    else:
        rows = ak.shape[0]
        for j in range(2 * N_HEADS):
            kraw_ref[pl.ds(j, rows, stride=2 * N_HEADS), :] = ak[:, j * QK_DIM:(j + 1) * QK_DIM]
        for h in range(N_HEADS):
            vraw_ref[pl.ds(h, rows, stride=N_HEADS), :] = av[:, h * HEAD_DIM:(h + 1) * HEAD_DIM]
    q_ref[...] = (aq * (QK_DIM ** -0.5 * LOG2_E)).astype(BF16)
    k_ref[...] = ak.astype(BF16)
    v_ref[...] = av.astype(BF16)
    rq_ref[...] = proj(3)
    xf_ref[...] = proj(4)
    xb_ref[...] = proj(5)
    ri_ref[...] = proj(6).astype(BF16)
    rg_ref[...] = _silu(proj(7))


def _inproj_call(x2d, mod, w_in, rope_tabs, *, latent, seq_len):
    t = x2d.shape[0]
    rb = ROWS_PROJ
    blocks_per_seq = seq_len // rb if latent else 1

    row_spec = lambda w: pl.BlockSpec((rb, w), lambda i: (i, 0))
    in_specs = [row_spec(D_MODEL),
                pl.BlockSpec((MOD_GROUP, mod.shape[1]), lambda i: (0 if latent else 1, 0)),
                pl.BlockSpec(w_in.shape, lambda i: (0, 0))]
    args = [x2d, mod, w_in]
    if latent:
        in_specs += [pl.BlockSpec((rb, GROUP_W), lambda i: (i % blocks_per_seq, 0))] * 3
        args += list(rope_tabs)
    dts = [BF16, BF16, BF16, F32, F32, F32, BF16, F32]
    out_specs = [row_spec(GROUP_W) for _ in dts]
    out_shape = [jax.ShapeDtypeStruct((t, GROUP_W), dt) for dt in dts]
    if not latent:
        out_specs += [pl.BlockSpec((rb * 2 * N_HEADS, QK_DIM), lambda i: (i, 0)),
                      pl.BlockSpec((rb * N_HEADS, HEAD_DIM), lambda i: (i, 0))]
        out_shape += [jax.ShapeDtypeStruct((t * 2 * N_HEADS, QK_DIM), F32),
                      jax.ShapeDtypeStruct((t * N_HEADS, HEAD_DIM), F32)]
    return pl.pallas_call(
        functools.partial(_inproj_kernel, latent=latent, blocks_per_row=blocks_per_seq if latent else None),
        grid=(t // rb,),
        in_specs=in_specs,
        out_specs=out_specs,
        out_shape=out_shape,
        compiler_params=_params(("arbitrary",)),
        name="inproj_lat" if latent else "inproj_ctx",
    )(*args)


def _attn_kernel(*refs, has_cache, n_casts):
    if has_cache:
        (q_ref, k_ref, v_ref, ck_ref, cv_ref, lq1, lk1, lq2, lk2, g_ref) = refs[:10]
        o_ref = refs[10 + n_casts]
        for src, dst in zip(refs[10:10 + n_casts], refs[11 + n_casts:]):
            dst[...] = src[...].astype(BF16)
    else:
        (q_ref, k_ref, v_ref, lq1, lk1, lq2, lk2, g_ref, o_ref) = refs
    lam = (jnp.exp(jnp.sum(lq1[...] * lk1[...], axis=-1, keepdims=True))
           - jnp.exp(jnp.sum(lq2[...] * lk2[...], axis=-1, keepdims=True)) + LAM_INIT)
    qb = q_ref.shape[1]
    lane = lax.broadcasted_iota(jnp.int32, (qb, HEAD_DIM), 1)
    first_map = lane < QK_DIM
    zero = jnp.zeros((), BF16)

    units = [(b, h) for b in range(q_ref.shape[0]) for h in range(N_HEADS)]

    def scores(u):
        b, h = units[u]
        sl = slice(h * HEAD_DIM, (h + 1) * HEAD_DIM)
        qh = q_ref[b, :, sl]
        qq = jnp.concatenate([jnp.where(first_map, qh, zero), jnp.where(first_map, zero, qh)], axis=0)
        s_n = _dot_nt(k_ref[b, :, sl], qq)
        s_c = None
        if has_cache:
            s_c = _dot_nt(ck_ref[b, sl, :].astype(BF16).T, qq)
        return s_n, s_c

    ahead = 2
    pending = [scores(u) for u in range(ahead)]
    for u, (b, h) in enumerate(units):
        sl = slice(h * HEAD_DIM, (h + 1) * HEAD_DIM)
        s_n, s_c = pending.pop(0)
        if u + ahead < len(units):
            pending.append(scores(u + ahead))
        mx = jnp.max(s_n, axis=0, keepdims=True)
        if has_cache:
            mx = jnp.maximum(mx, jnp.max(s_c, axis=0, keepdims=True))
        e_n = jnp.exp2(s_n - mx)
        den = jnp.sum(e_n, axis=0, keepdims=True)
        ev = _dot_tn(v_ref[b, :, sl], e_n.astype(BF16))
        if has_cache:
            e_c = jnp.exp2(s_c - mx)
            den = den + jnp.sum(e_c, axis=0, keepdims=True)
            vc = cv_ref[b, pl.ds(h, e_c.shape[0], stride=N_HEADS), :]
            ev = ev + _dot_tn(vc.astype(BF16), e_c.astype(BF16))
        inv = 1.0 / den
        o = ev[:, :qb] * inv[:, :qb] - ev[:, qb:] * (inv[:, qb:] * lam)
        o = o * lax.rsqrt(jnp.mean(o * o, axis=0, keepdims=True) + NORM_EPS)
        o_ref[b, :, sl] = (o.T * g_ref[:, sl] * (1.0 - LAM_INIT)).astype(BF16)


def _attn_call(q, k, v, cache, lams, att_g, casts=()):
    b, l, _ = q.shape
    qb = min(Q_ROWS, l)
    nq = l // qb
    nb = SHORT_SEQS_PER_STEP if nq == 1 and b % SHORT_SEQS_PER_STEP == 0 else 1
    steps = b // nb
    slab = lambda a: pl.BlockSpec((a.shape[0] // (steps * nq), a.shape[1]), lambda i, j: (i * nq + j, 0))
    full = lambda a: pl.BlockSpec((nb,) + a.shape[1:], lambda i, j: (i,) + (0,) * (a.ndim - 1))
    in_specs = [pl.BlockSpec((nb, qb, GROUP_W), lambda i, j: (i, j, 0)), full(k), full(v)]
    args = [q, k, v]
    if cache is not None:
        in_specs += [full(cache[0]), full(cache[1])]
        args += list(cache)
    in_specs += [pl.BlockSpec((1, QK_DIM), lambda i, j: (0, 0))] * 4
    in_specs += [pl.BlockSpec((1, GROUP_W), lambda i, j: (0, 0))]
    args += list(lams) + [att_g] + list(casts)
    in_specs += [slab(a) for a in casts]
    return pl.pallas_call(
        functools.partial(_attn_kernel, has_cache=cache is not None, n_casts=len(casts)),
        grid=(steps, nq),
        in_specs=in_specs,
        out_specs=[pl.BlockSpec((nb, qb, GROUP_W), lambda i, j: (i, j, 0))] + [slab(a) for a in casts],
        out_shape=[jax.ShapeDtypeStruct((b, l, GROUP_W), BF16)] + [jax.ShapeDtypeStruct(a.shape, BF16) for a in casts],
        compiler_params=_params(("arbitrary", "arbitrary")),
        name="attn_lat" if cache is not None else "attn_ctx",
    )(*args)


HGRN_DIAG = 128
HGRN_MASKED_LEVELS = 5
HGRN_VREG_LEVELS = 3


def _hgrn_masks():
    r = np.arange(HGRN_DIAG)[:, None]
    c = np.arange(HGRN_DIAG)[None, :]
    levels = []
    for lv in range(HGRN_MASKED_LEVELS):
        same = (r >> (lv + 1)) == (c >> (lv + 1))
        levels.append(same & (((r >> lv) & 1) == 1) & (((c >> lv) & 1) == 0))
    fwd = [r == c] + levels
    bwd = [m.T for m in levels]
    return np.stack(fwd + bwd).astype(np.float32)


def _hgrn_block(q, k, g, v, st, mask_ref, sub, *, reverse):
    n = HGRN_ROWS
    n_diag = n // HGRN_DIAG

    def diag_scores(qe, ke, m):
        return [_dot_nt(qe[i * HGRN_DIAG:(i + 1) * HGRN_DIAG], ke[i * HGRN_DIAG:(i + 1) * HGRN_DIAG]).astype(BF16) * m
                for i in range(n_diag)]

    def level_mask(lv):
        return mask_ref[(1 + HGRN_MASKED_LEVELS + lv) if reverse else (1 + lv)]

    g3 = g.reshape(n // 8, 8, HEAD_DIM)
    zero3 = jnp.zeros_like(g3)
    pre3, suf3 = (zero3, g3) if reverse else (g3, zero3)
    tot3 = g3
    qb, kb = q.astype(BF16), k.astype(BF16)
    acc = diag_scores(qb, kb, mask_ref[0])
    for lv in range(HGRN_VREG_LEVELS):
        b = 1 << lv
        upper = ((sub >> lv) & 1) == 1
        e = jnp.exp2(jnp.where(upper, pre3, suf3)).reshape(n, HEAD_DIM)
        eb = e.astype(BF16)
        part = diag_scores(qb * eb, kb * eb, level_mask(lv))
        acc = [a + p for a, p in zip(acc, part)]
        sib = jnp.where(upper, pltpu.roll(tot3, b, axis=1), pltpu.roll(tot3, 8 - b, axis=1))
        pre3 = pre3 + jnp.where(upper, sib, 0.0)
        suf3 = suf3 + jnp.where(upper, 0.0, sib)
        tot3 = tot3 + sib

    pieces = lambda x3: [x3[i] for i in range(n // 8)]
    pre8, suf8, tot8 = pieces(pre3), pieces(suf3), pieces(tot3)
    rows = lambda xs, lo, hi: jnp.concatenate(xs[lo // 8:hi // 8], axis=0) if hi - lo > 8 else xs[lo // 8]
    big_scores = {}
    for lv in range(HGRN_VREG_LEVELS, HGRN_LEVELS):
        b = 1 << lv
        pb = b // 8
        if lv < HGRN_MASKED_LEVELS:
            u = jnp.concatenate([(pre8 if (i // pb) % 2 else suf8)[i] for i in range(n // 8)], axis=0)
            e = jnp.exp2(u)
            eb = e.astype(BF16)
            part = diag_scores(qb * eb, kb * eb, level_mask(lv))
            acc = [a + p for a, p in zip(acc, part)]
        else:
            for j in range(n // (2 * b)):
                lo, mid, hi = 2 * b * j, 2 * b * j + b, 2 * b * (j + 1)
                e_lo = jnp.exp2(rows(suf8, lo, mid))
                e_up = jnp.exp2(rows(pre8, mid, hi))
                x_lo = (qb if reverse else kb)[lo:mid] * e_lo.astype(BF16)
                x_up = (kb if reverse else qb)[mid:hi] * e_up.astype(BF16)
                big_scores[lv, j] = _dot_nt(x_lo, x_up) if reverse else _dot_nt(x_up, x_lo)
        for j in range(n // (2 * b)):
            lo_p, mid_p, hi_p = 2 * pb * j, 2 * pb * j + pb, 2 * pb * (j + 1)
            t_lo, t_up = tot8[lo_p], tot8[mid_p]
            t_new = t_lo + t_up
            for i in range(lo_p, mid_p):
                suf8[i] = suf8[i] + t_up
                tot8[i] = t_new
            for i in range(mid_p, hi_p):
                pre8[i] = pre8[i] + t_lo
                tot8[i] = t_new
    pre = jnp.concatenate(pre8, axis=0)
    suf = jnp.concatenate(suf8, axis=0)
    q_dec, k_dec = (suf, pre) if reverse else (pre, suf)
    q_in = qb * jnp.exp2(q_dec).astype(BF16)
    k_out = kb * jnp.exp2(k_dec).astype(BF16)
    decay = jnp.exp2(tot8[0][0:1, :])

    o_inter = _dot_nt(q_in, st.astype(BF16))
    ds = _dot_tn(v, k_out)
    o_diag = [_dot(acc[i].astype(BF16), v[i * HGRN_DIAG:(i + 1) * HGRN_DIAG]) for i in range(n_diag)]
    o_big = {}
    for (lv, j), a in big_scores.items():
        lo, mid, hi = (2 * j) << lv, (2 * j + 1) << lv, (2 * j + 2) << lv
        o_big[lv, j] = _dot(a.astype(BF16), v[mid:hi] if reverse else v[lo:mid])
    st_new = st * decay + ds
    o = jnp.concatenate(o_diag, axis=0) + o_inter
    for lv in range(HGRN_MASKED_LEVELS, HGRN_LEVELS):
        contrib = []
        for j in range(n >> (lv + 1)):
            zeros = jnp.zeros((1 << lv, HEAD_DIM), F32)
            contrib += [o_big[lv, j], zeros] if reverse else [zeros, o_big[lv, j]]
        o = o + jnp.concatenate(contrib, axis=0)
    return o, st_new


def _hgrn_kernel(*refs, n_blocks, has_state):
    if has_state:
        (q_ref, v_ref, xf_ref, xb_ref, gate_ref, lbf_ref, lbb_ref, g_ref, mask_ref, s0_ref,
         o_ref, acc_ref) = refs
    else:
        (q_ref, v_ref, xf_ref, xb_ref, gate_ref, lbf_ref, lbb_ref, g_ref, mask_ref,
         o_ref, sout_ref, acc_ref) = refs
    n = HGRN_ROWS
    sub = lax.broadcasted_iota(jnp.int32, (n // 8, 8, HEAD_DIM), 1)
    head_lanes = [slice(h * HEAD_DIM, (h + 1) * HEAD_DIM) for h in range(N_HEADS)]

    def lower_bound(ref, lanes):
        l0, l1 = ref[0:1, lanes], ref[1:2, lanes]
        mx = jnp.maximum(l0, l1)
        e0, e1 = jnp.exp(l0 - mx), jnp.exp(l1 - mx)
        return e0 / (e0 + e1)

    x_refs = (xf_ref, xb_ref)
    for b in range(q_ref.shape[0]):
        for h, lanes in enumerate(head_lanes):
            lbs = [lower_bound(lbf_ref, lanes), lower_bound(lbb_ref, lanes)]
            sts = [s0_ref[b, d, h].T if has_state else jnp.zeros((HEAD_DIM, HEAD_DIM), F32) for d in range(2)]
            for j in range(n_blocks):
                for d in range(2):
                    blk = (n_blocks - 1 - j) if d else j
                    rows = slice(blk * n, (blk + 1) * n)
                    f = lbs[d] + (1.0 - lbs[d]) * jax.nn.sigmoid(x_refs[d][b, rows, lanes])
                    o, sts[d] = _hgrn_block(q_ref[b, rows, lanes], 1.0 - f, jnp.log2(f), v_ref[b, rows, lanes],
                                            sts[d], mask_ref, sub, reverse=bool(d))
                    acc_ref[b, d, rows, lanes] = o
            if not has_state:
                for d in range(2):
                    sout_ref[b, d, h] = sts[d].T
    for b in range(q_ref.shape[0]):
        for lanes in head_lanes:
            o = acc_ref[b, 0, :, lanes] + acc_ref[b, 1, :, lanes]
            o = o * lax.rsqrt(jnp.mean(o * o, axis=-1, keepdims=True) + NORM_EPS)
            o_ref[b, :, lanes] = (o * g_ref[:, lanes] * gate_ref[b, :, lanes]).astype(BF16)


def _hgrn_call(rq, ri, xf, xb, gate, lbf, lbb, rnn_g, masks, s0):
    b, l, w = rq.shape
    n_blocks = l // HGRN_ROWS
    nb = SHORT_SEQS_PER_STEP if n_blocks == 1 and b % SHORT_SEQS_PER_STEP == 0 else 1
    seq = lambda: pl.BlockSpec((nb, l, w), lambda i: (i, 0, 0))
    const = lambda a: pl.BlockSpec(a.shape, lambda i: (0,) * a.ndim)
    state_spec = pl.BlockSpec((nb, 2, N_HEADS, HEAD_DIM, HEAD_DIM), lambda i: (i, 0, 0, 0, 0))
    in_specs = [seq(), seq(), seq(), seq(), seq(), const(lbf), const(lbb), const(rnn_g), const(masks)]
    args = [rq, ri, xf, xb, gate, lbf, lbb, rnn_g, masks]
    out_specs = [seq()]
    out_shape = [jax.ShapeDtypeStruct((b, l, w), BF16)]
    if s0 is not None:
        in_specs.append(state_spec)
        args.append(s0)
    else:
        out_specs.append(state_spec)
        out_shape.append(jax.ShapeDtypeStruct((b, 2, N_HEADS, HEAD_DIM, HEAD_DIM), F32))
    return pl.pallas_call(
        functools.partial(_hgrn_kernel, n_blocks=n_blocks, has_state=s0 is not None),
        grid=(b // nb,),
        in_specs=in_specs,
        out_specs=out_specs,
        out_shape=out_shape,
        scratch_shapes=[pltpu.VMEM((nb, 2, l, w), F32)],
        compiler_params=_params(("arbitrary",)),
        name="hgrn_lat" if s0 is not None else "hgrn_ctx",
    )(*args)


def _tail_kernel(att_ref, rnn_ref, x_ref, mod_ref, wo_ref, g1_ref, b1_ref, wu_ref, cw_ref, cb_ref, wd_ref,
                 g2_ref, b2_ref, o_ref, x1_ref, xm2_ref, hid_ref, *, seq_len, blocks_per_row):
    rows = x_ref.shape[0]
    gate1, shift2, scale2, gate2 = (_mod_vector(mod_ref, k, blocks_per_row) for k in (2, 3, 4, 5))
    slabs = [slice(r, r + NORM_ROWS) for r in range(0, rows, NORM_ROWS)]
    mixes = [_dot(att_ref[sl, :], wo_ref[0:GROUP_W, :]) + _dot(rnn_ref[sl, :], wo_ref[GROUP_W:2 * GROUP_W, :])
             for sl in slabs]
    for sl, mix in zip(slabs, mixes):
        x1 = _layer_norm(DEEPNORM_ALPHA * x_ref[sl, :] + gate1 * mix, g1_ref[...], b1_ref[...])
        x1_ref[sl, :] = x1
        xm2_ref[sl, :] = (x1 * (1.0 + scale2) + shift2).astype(BF16)

    sub = lax.broadcasted_iota(jnp.int32, (8, FF_TILE), 0)

    def shifted(h, shift, edge_row):
        r = pltpu.roll(h, shift, axis=0)
        parts = []
        for s in range(rows // seq_len):
            edge = s * seq_len + (edge_row // 8) * 8
            fixed = jnp.where(sub == edge_row % 8, 0.0, r[edge:edge + 8])
            parts += [r[s * seq_len:edge], fixed, r[edge + 8:(s + 1) * seq_len]]
        return jnp.concatenate([p for p in parts if p.shape[0]], axis=0)

    def conv(h, cols):
        prev = shifted(h, 1, 0)
        nxt = shifted(h, rows - 1, seq_len - 1)
        return prev * cw_ref[0:1, cols] + h * cw_ref[1:2, cols] + nxt * cw_ref[2:3, cols] + cb_ref[:, cols]

    for j in range(D_FF // FF_TILE):
        cols_a = slice(j * FF_TILE, (j + 1) * FF_TILE)
        cols_u = slice(D_FF + j * FF_TILE, D_FF + (j + 1) * FF_TILE)
        a = conv(_dot(xm2_ref[...], wu_ref[:, cols_a]), cols_a)
        u = conv(_dot(xm2_ref[...], wu_ref[:, cols_u]), cols_u)
        hid_ref[:, cols_a] = (_silu(a) * u).astype(BF16)
    ffns = [_dot(hid_ref[sl, :], wd_ref[...]) for sl in slabs]
    for sl, ffn in zip(slabs, ffns):
        y = DEEPNORM_ALPHA * x1_ref[sl, :] + gate2 * ffn
        o_ref[sl, :] = _layer_norm(y, g2_ref[...], b2_ref[...])


def _tail_call(att, rnn, x2d, mod, w_out, ln1_g, ln1_b, w_up, conv_w, conv_b, w_down, ln2_g, ln2_b,
               *, latent, seq_len):
    t = x2d.shape[0]
    rb = ROWS_FFN
    assert rb % seq_len == 0

    assert not latent or rb == seq_len
    row_spec = lambda w: pl.BlockSpec((rb, w), lambda i: (i, 0))
    resident = lambda a: pl.BlockSpec(a.shape, lambda i: (0, 0), pipeline_mode=pl.Buffered(1))
    return pl.pallas_call(
        functools.partial(_tail_kernel, seq_len=seq_len, blocks_per_row=1 if latent else None),
        grid=(t // rb,),
        in_specs=[row_spec(GROUP_W), row_spec(GROUP_W), row_spec(D_MODEL),
                  pl.BlockSpec((MOD_GROUP, mod.shape[1]), lambda i: (0 if latent else 1, 0)),
                  resident(w_out), resident(ln1_g), resident(ln1_b),
                  resident(w_up), resident(conv_w), resident(conv_b), resident(w_down),
                  resident(ln2_g), resident(ln2_b)],
        out_specs=row_spec(D_MODEL),
        out_shape=jax.ShapeDtypeStruct((t, D_MODEL), F32),
        scratch_shapes=[pltpu.VMEM((rb, D_MODEL), F32), pltpu.VMEM((rb, D_MODEL), BF16),
                        pltpu.VMEM((rb, D_FF), BF16)],
        compiler_params=_params(("arbitrary",)),
        name="tail_lat" if latent else "tail_ctx",
    )(att, rnn, x2d, mod, w_out, ln1_g, ln1_b, w_up, conv_w, conv_b, w_down, ln2_g, ln2_b)


def _rope_tables(seq_len):
    quarter = QK_DIM // 4
    freqs = 1.0 / (ROPE_BASE ** (np.arange(quarter, dtype=np.float64) / quarter))
    t = np.arange(seq_len)
    ang_r = (t // GRID_W)[:, None] * freqs
    ang_c = (t % GRID_W)[:, None] * freqs
    zeros = np.zeros_like(ang_r)

    def tile(parts):
        return jnp.asarray(np.tile(np.concatenate(parts, axis=-1), (1, GROUP_W // QK_DIM)).astype(np.float32))

    cos = tile([np.cos(ang_r), np.cos(ang_r), np.cos(ang_c), np.cos(ang_c)])
    sin_a = tile([-np.sin(ang_r), zeros, -np.sin(ang_c), zeros])
    sin_b = tile([zeros, np.sin(ang_r), zeros, np.sin(ang_c)])
    return cos, sin_a, sin_b


def kernel(x_prompt, x_sample, cache_k, cache_v, state_rnn, c, c_ctx, w_ada, b_ada, w_in, lambda_q1, lambda_k1, lambda_q2, lambda_k2, lb_fwd_logits, lb_bwd_logits, att_norm_g, rnn_norm_g, w_out, ln1_g, ln1_b, w_up, conv_w, conv_b, w_down, ln2_g, ln2_b):
    assert w_ada.shape[0] == DEPTH
    bp, lp, d = x_prompt.shape
    bs, ls, _ = x_sample.shape
    past = cache_k.shape[2]

    mod, w_in_b = _mod_call(c, c_ctx.reshape(1, d), w_ada[0], b_ada, w_in[0])

    lams = (lambda_q1, lambda_k1, lambda_q2, lambda_k2)
    masks = jnp.asarray(_hgrn_masks(), dtype=BF16)
    xp2d = x_prompt.reshape(bp * lp, d)
    xs2d = x_sample.reshape(bs * ls, d)

    ctx_in = _inproj_call(xp2d, mod, w_in_b, None, latent=False, seq_len=lp)
    lat_in = _inproj_call(xs2d, mod, w_in_b, _rope_tables(ls), latent=True, seq_len=ls)
    k_raw, v_raw = ctx_in[8:]
    cache = (jnp.transpose(cache_k, (0, 1, 3, 4, 5, 2)).reshape(bs, GROUP_W, past),
             cache_v.reshape(bs, past * N_HEADS, HEAD_DIM))

    def heads(proj, b, l):
        return tuple(o.reshape(b, l, GROUP_W) for o in proj[:8])

    q, k, v, rq, xf, xb, ri, rg = heads(lat_in, bs, ls)
    att_s, w_out_b, w_up_b, w_down_b = _attn_call(q, k, v, cache, lams, att_norm_g,
                                                  casts=(w_out[0], w_up[0], w_down[0]))
    rnn_s, = _hgrn_call(rq, ri, xf, xb, rg, lb_fwd_logits, lb_bwd_logits, rnn_norm_g, masks,
                        state_rnn.reshape(bs, 2, N_HEADS, HEAD_DIM, HEAD_DIM))
    q, k, v, rq, xf, xb, ri, rg = heads(ctx_in, bp, lp)
    att_p, = _attn_call(q, k, v, None, lams, att_norm_g)
    rnn_p, s_new = _hgrn_call(rq, ri, xf, xb, rg, lb_fwd_logits, lb_bwd_logits, rnn_norm_g, masks, None)

    def tail(att, rnn, x2d, b, l, latent):
        y = _tail_call(att.reshape(b * l, GROUP_W), rnn.reshape(b * l, GROUP_W), x2d, mod, w_out_b, ln1_g,
                       ln1_b, w_up_b, conv_w[0], conv_b, w_down_b, ln2_g, ln2_b, latent=latent, seq_len=l)
        return y.reshape(b, l, d)

    y_p = tail(att_p, rnn_p, xp2d, bp, lp, False)
    y_s = tail(att_s, rnn_s, xs2d, bs, ls, True)

    new_cache_k = k_raw.reshape(bp, DEPTH, lp, N_HEADS, 2, QK_DIM)
    new_cache_v = v_raw.reshape(bp, DEPTH, lp, N_HEADS, HEAD_DIM)
    new_state = s_new.reshape(bp, DEPTH, 2, N_HEADS, HEAD_DIM, HEAD_DIM)
    return (y_p, y_s, new_cache_k, new_cache_v, new_state)
```

```python
import functools
import math

import jax
import jax.numpy as jnp
import numpy as np
from jax import lax
from jax.experimental import pallas as pl
from jax.experimental.pallas import tpu as pltpu

D_MODEL = 1024
GRID_W = 64
N_HEADS = 4
HEAD_DIM = 128
QK_DIM = 64
GROUP_W = 512
N_GROUPS = 8
D_FF = 2816
ROPE_BASE = 10000.0
DEPTH = 1
DEEPNORM_ALPHA = (2.0 * DEPTH) ** 0.25
NORM_EPS = 1e-5
LAM_INIT = 0.8 - 0.6 * math.exp(-0.3 * 0)
LOG2_E = math.log2(math.e)

V7X_VMEM_BYTES = 64 * 1024 * 1024
VMEM_LIMIT = V7X_VMEM_BYTES * 15 // 16
MOD_GROUP = 8

MOD_TILE = 1536
ROWS_PROJ = 512
ROWS_FFN = 1024
FF_TILE = 256
NORM_ROWS = 256
Q_ROWS = 512
SHORT_SEQS_PER_STEP = 2
HGRN_ROWS = 256
HGRN_LEVELS = int(math.log2(HGRN_ROWS))

F32 = jnp.float32
BF16 = jnp.bfloat16


def _params(semantics):
    return pltpu.CompilerParams(dimension_semantics=semantics, vmem_limit_bytes=VMEM_LIMIT)


def _dot(a, b):
    return jnp.dot(a, b, preferred_element_type=F32)


def _dot_nt(a, b):
    return lax.dot_general(a, b, (((1,), (1,)), ((), ())), preferred_element_type=F32)


def _dot_tn(a, b):
    return lax.dot_general(a, b, (((0,), (0,)), ((), ())), preferred_element_type=F32)


def _silu(x):
    return x * jax.nn.sigmoid(x)


def _layer_norm(y, g, b):
    mu = jnp.mean(y, axis=-1, keepdims=True)
    d = y - mu
    var = jnp.mean(d * d, axis=-1, keepdims=True)
    return d * lax.rsqrt(var + NORM_EPS) * g + b


def _mod_kernel(c_ref, cctx_ref, w_ref, b_ref, win_ref, o_ref, win_out_ref):
    cond = jnp.concatenate([c_ref[...], jnp.broadcast_to(cctx_ref[...], c_ref.shape)], axis=0)
    s = _silu(cond).astype(BF16)
    o_ref[...] = _dot(s, w_ref[...].astype(BF16)) + b_ref[...]
    win_out_ref[...] = win_ref[...].astype(BF16)


def _mod_call(c, c_ctx, w_ada, b_ada, w_in):
    assert c.shape[0] == MOD_GROUP
    n, d = 2 * MOD_GROUP, c.shape[1]
    cols = w_ada.shape[1]
    tile = MOD_TILE
    n_steps = cols // tile
    slab = pl.BlockSpec((w_in.shape[0] // n_steps, w_in.shape[1]), lambda j: (j, 0))
    return pl.pallas_call(
        _mod_kernel,
        grid=(n_steps,),
        in_specs=[pl.BlockSpec(c.shape, lambda j: (0, 0)),
                  pl.BlockSpec(c_ctx.shape, lambda j: (0, 0)),
                  pl.BlockSpec((d, tile), lambda j: (0, j)),
                  pl.BlockSpec((1, tile), lambda j: (0, j)),
                  slab],
        out_specs=[pl.BlockSpec((n, tile), lambda j: (0, j)), slab],
        out_shape=[jax.ShapeDtypeStruct((n, cols), F32), jax.ShapeDtypeStruct(w_in.shape, BF16)],
        compiler_params=_params(("arbitrary",)),
        name="mod",
    )(c, c_ctx, w_ada, b_ada, w_in)


def _mod_vector(mod_ref, k, blocks_per_row):
    row = 0 if blocks_per_row is None else pl.program_id(0) // blocks_per_row
    return mod_ref[pl.ds(row, 1), k * D_MODEL:(k + 1) * D_MODEL]


def _rope(x, cos_ref, sin_a_ref, sin_b_ref):
    parts = []
    for c in range(GROUP_W // 128):
        sl = slice(c * 128, (c + 1) * 128)
        xc = x[:, sl]
        parts.append(xc * cos_ref[:, sl]
                     + pltpu.roll(xc, 128 - 16, axis=1) * sin_a_ref[:, sl]
                     + pltpu.roll(xc, 16, axis=1) * sin_b_ref[:, sl])
    return jnp.concatenate(parts, axis=1)


def _inproj_kernel(*refs, latent, blocks_per_row):
    if latent:
        (x_ref, mod_ref, w_ref, cos_ref, sin_a_ref, sin_b_ref,
         q_ref, k_ref, v_ref, rq_ref, xf_ref, xb_ref, ri_ref, rg_ref) = refs
    else:
        (x_ref, mod_ref, w_ref,
         q_ref, k_ref, v_ref, rq_ref, xf_ref, xb_ref, ri_ref, rg_ref, kraw_ref, vraw_ref) = refs
    shift1, scale1 = (_mod_vector(mod_ref, k, blocks_per_row) for k in (0, 1))
    xm = (x_ref[...] * (1.0 + scale1) + shift1).astype(BF16)

    def proj(g):
        return _dot(xm, w_ref[:, g * GROUP_W:(g + 1) * GROUP_W])

    aq = proj(0)
    ak = proj(1)
    av = proj(2)
    if latent:
        aq = _rope(aq, cos_ref, sin_a_ref, sin_b_ref)
        ak = _rope(ak, cos_ref, sin_a_ref, sin_b_ref)
    else:
        rows = ak.shape[0]
        for j in range(2 * N_HEADS):
            kraw_ref[pl.ds(j, rows, stride=2 * N_HEADS), :] = ak[:, j * QK_DIM:(j + 1) * QK_DIM]
        for h in range(N_HEADS):
            vraw_ref[pl.ds(h, rows, stride=N_HEADS), :] = av[:, h * HEAD_DIM:(h + 1) * HEAD_DIM]
    q_ref[...] = (aq * (QK_DIM ** -0.5 * LOG2_E)).astype(BF16)
    k_ref[...] = ak.astype(BF16)
    v_ref[...] = av.astype(BF16)
    rq_ref[...] = proj(3)
    xf_ref[...] = proj(4)
    xb_ref[...] = proj(5)
    ri_ref[...] = proj(6).astype(BF16)
    rg_ref[...] = _silu(proj(7))


def _inproj_call(x2d, mod, w_in, rope_tabs, *, latent, seq_len):
    t = x2d.shape[0]
    rb = ROWS_PROJ
    blocks_per_seq = seq_len // rb if latent else 1

    row_spec = lambda w: pl.BlockSpec((rb, w), lambda i: (i, 0))
    in_specs = [row_spec(D_MODEL),
                pl.BlockSpec((MOD_GROUP, mod.shape[1]), lambda i: (0 if latent else 1, 0)),
                pl.BlockSpec(w_in.shape, lambda i: (0, 0))]
    args = [x2d, mod, w_in]
    if latent:
        in_specs += [pl.BlockSpec((rb, GROUP_W), lambda i: (i % blocks_per_seq, 0))] * 3
        args += list(rope_tabs)
    dts = [BF16, BF16, BF16, F32, F32, F32, BF16, F32]
    out_specs = [row_spec(GROUP_W) for _ in dts]
    out_shape = [jax.ShapeDtypeStruct((t, GROUP_W), dt) for dt in dts]
    if not latent:
        out_specs += [pl.BlockSpec((rb * 2 * N_HEADS, QK_DIM), lambda i: (i, 0)),
                      pl.BlockSpec((rb * N_HEADS, HEAD_DIM), lambda i: (i, 0))]
        out_shape += [jax.ShapeDtypeStruct((t * 2 * N_HEADS, QK_DIM), F32),
                      jax.ShapeDtypeStruct((t * N_HEADS, HEAD_DIM), F32)]
    return pl.pallas_call(
        functools.partial(_inproj_kernel, latent=latent, blocks_per_row=blocks_per_seq if latent else None),
        grid=(t // rb,),
        in_specs=in_specs,
        out_specs=out_specs,
        out_shape=out_shape,
        compiler_params=_params(("arbitrary",)),
        name="inproj_lat" if latent else "inproj_ctx",
    )(*args)


def _attn_kernel(*refs, has_cache, n_casts):
    if has_cache:
        (q_ref, k_ref, v_ref, ck_ref, cv_ref, lq1, lk1, lq2, lk2, g_ref) = refs[:10]
        o_ref = refs[10 + n_casts]
        for src, dst in zip(refs[10:10 + n_casts], refs[11 + n_casts:]):
            dst[...] = src[...].astype(BF16)
    else:
        (q_ref, k_ref, v_ref, lq1, lk1, lq2, lk2, g_ref, o_ref) = refs
    lam = (jnp.exp(jnp.sum(lq1[...] * lk1[...], axis=-1, keepdims=True))
           - jnp.exp(jnp.sum(lq2[...] * lk2[...], axis=-1, keepdims=True)) + LAM_INIT)
    qb = q_ref.shape[1]
    lane = lax.broadcasted_iota(jnp.int32, (qb, HEAD_DIM), 1)
    first_map = lane < QK_DIM
    zero = jnp.zeros((), BF16)

    units = [(b, h) for b in range(q_ref.shape[0]) for h in range(N_HEADS)]

    def scores(u):
        b, h = units[u]
        sl = slice(h * HEAD_DIM, (h + 1) * HEAD_DIM)
        qh = q_ref[b, :, sl]
        qq = jnp.concatenate([jnp.where(first_map, qh, zero), jnp.where(first_map, zero, qh)], axis=0)
        s_n = _dot_nt(k_ref[b, :, sl], qq)
        s_c = None
        if has_cache:
            s_c = _dot_nt(ck_ref[b, sl, :].astype(BF16).T, qq)
        return s_n, s_c

    ahead = 2
    pending = [scores(u) for u in range(ahead)]
    for u, (b, h) in enumerate(units):
        sl = slice(h * HEAD_DIM, (h + 1) * HEAD_DIM)
        s_n, s_c = pending.pop(0)
        if u + ahead < len(units):
            pending.append(scores(u + ahead))
        mx = jnp.max(s_n, axis=0, keepdims=True)
        if has_cache:
            mx = jnp.maximum(mx, jnp.max(s_c, axis=0, keepdims=True))
        e_n = jnp.exp2(s_n - mx)
        den = jnp.sum(e_n, axis=0, keepdims=True)
        ev = _dot_tn(v_ref[b, :, sl], e_n.astype(BF16))
        if has_cache:
            e_c = jnp.exp2(s_c - mx)
            den = den + jnp.sum(e_c, axis=0, keepdims=True)
            vc = cv_ref[b, pl.ds(h, e_c.shape[0], stride=N_HEADS), :]
            ev = ev + _dot_tn(vc.astype(BF16), e_c.astype(BF16))
        inv = 1.0 / den
        o = ev[:, :qb] * inv[:, :qb] - ev[:, qb:] * (inv[:, qb:] * lam)
        o = o * lax.rsqrt(jnp.mean(o * o, axis=0, keepdims=True) + NORM_EPS)
        o_ref[b, :, sl] = (o.T * g_ref[:, sl] * (1.0 - LAM_INIT)).astype(BF16)


def _attn_call(q, k, v, cache, lams, att_g, casts=()):
    b, l, _ = q.shape
    qb = min(Q_ROWS, l)
    nq = l // qb
    nb = SHORT_SEQS_PER_STEP if nq == 1 and b % SHORT_SEQS_PER_STEP == 0 else 1
    steps = b // nb
    slab = lambda a: pl.BlockSpec((a.shape[0] // (steps * nq), a.shape[1]), lambda i, j: (i * nq + j, 0))
    full = lambda a: pl.BlockSpec((nb,) + a.shape[1:], lambda i, j: (i,) + (0,) * (a.ndim - 1))
    in_specs = [pl.BlockSpec((nb, qb, GROUP_W), lambda i, j: (i, j, 0)), full(k), full(v)]
    args = [q, k, v]
    if cache is not None:
        in_specs += [full(cache[0]), full(cache[1])]
        args += list(cache)
    in_specs += [pl.BlockSpec((1, QK_DIM), lambda i, j: (0, 0))] * 4
    in_specs += [pl.BlockSpec((1, GROUP_W), lambda i, j: (0, 0))]
    args += list(lams) + [att_g] + list(casts)
    in_specs += [slab(a) for a in casts]
    return pl.pallas_call(
        functools.partial(_attn_kernel, has_cache=cache is not None, n_casts=len(casts)),
        grid=(steps, nq),
        in_specs=in_specs,
        out_specs=[pl.BlockSpec((nb, qb, GROUP_W), lambda i, j: (i, j, 0))] + [slab(a) for a in casts],
        out_shape=[jax.ShapeDtypeStruct((b, l, GROUP_W), BF16)] + [jax.ShapeDtypeStruct(a.shape, BF16) for a in casts],
        compiler_params=_params(("arbitrary", "arbitrary")),
        name="attn_lat" if cache is not None else "attn_ctx",
    )(*args)


HGRN_DIAG = 128
HGRN_MASKED_LEVELS = 5
HGRN_VREG_LEVELS = 3


def _hgrn_masks():
    r = np.arange(HGRN_DIAG)[:, None]
    c = np.arange(HGRN_DIAG)[None, :]
    levels = []
    for lv in range(HGRN_MASKED_LEVELS):
        same = (r >> (lv + 1)) == (c >> (lv + 1))
        levels.append(same & (((r >> lv) & 1) == 1) & (((c >> lv) & 1) == 0))
    fwd = [r == c] + levels
    bwd = [m.T for m in levels]
    return np.stack(fwd + bwd).astype(np.float32)


def _hgrn_block(q, k, g, v, st, mask_ref, sub, *, reverse):
    n = HGRN_ROWS
    n_diag = n // HGRN_DIAG

    def diag_scores(qe, ke, m):
        return [_dot_nt(qe[i * HGRN_DIAG:(i + 1) * HGRN_DIAG], ke[i * HGRN_DIAG:(i + 1) * HGRN_DIAG]).astype(BF16) * m
                for i in range(n_diag)]

    def level_mask(lv):
        return mask_ref[(1 + HGRN_MASKED_LEVELS + lv) if reverse else (1 + lv)]

    g3 = g.reshape(n // 8, 8, HEAD_DIM)
    zero3 = jnp.zeros_like(g3)
    pre3, suf3 = (zero3, g3) if reverse else (g3, zero3)
    tot3 = g3
    qb, kb = q.astype(BF16), k.astype(BF16)
    acc = diag_scores(qb, kb, mask_ref[0])
    for lv in range(HGRN_VREG_LEVELS):
        b = 1 << lv
        upper = ((sub >> lv) & 1) == 1
        e = jnp.exp2(jnp.where(upper, pre3, suf3)).reshape(n, HEAD_DIM)
        eb = e.astype(BF16)
        part = diag_scores(qb * eb, kb * eb, level_mask(lv))
        acc = [a + p for a, p in zip(acc, part)]
        sib = jnp.where(upper, pltpu.roll(tot3, b, axis=1), pltpu.roll(tot3, 8 - b, axis=1))
        pre3 = pre3 + jnp.where(upper, sib, 0.0)
        suf3 = suf3 + jnp.where(upper, 0.0, sib)
        tot3 = tot3 + sib

    pieces = lambda x3: [x3[i] for i in range(n // 8)]
    pre8, suf8, tot8 = pieces(pre3), pieces(suf3), pieces(tot3)
    rows = lambda xs, lo, hi: jnp.concatenate(xs[lo // 8:hi // 8], axis=0) if hi - lo > 8 else xs[lo // 8]
    big_scores = {}
    for lv in range(HGRN_VREG_LEVELS, HGRN_LEVELS):
        b = 1 << lv
        pb = b // 8
        if lv < HGRN_MASKED_LEVELS:
            u = jnp.concatenate([(pre8 if (i // pb) % 2 else suf8)[i] for i in range(n // 8)], axis=0)
            e = jnp.exp2(u)
            eb = e.astype(BF16)
            part = diag_scores(qb * eb, kb * eb, level_mask(lv))
            acc = [a + p for a, p in zip(acc, part)]
        else:
            for j in range(n // (2 * b)):
                lo, mid, hi = 2 * b * j, 2 * b * j + b, 2 * b * (j + 1)
                e_lo = jnp.exp2(rows(suf8, lo, mid))
                e_up = jnp.exp2(rows(pre8, mid, hi))
                x_lo = (qb if reverse else kb)[lo:mid] * e_lo.astype(BF16)
                x_up = (kb if reverse else qb)[mid:hi] * e_up.astype(BF16)
                big_scores[lv, j] = _dot_nt(x_lo, x_up) if reverse else _dot_nt(x_up, x_lo)
        for j in range(n // (2 * b)):
            lo_p, mid_p, hi_p = 2 * pb * j, 2 * pb * j + pb, 2 * pb * (j + 1)
            t_lo, t_up = tot8[lo_p], tot8[mid_p]
            t_new = t_lo + t_up
            for i in range(lo_p, mid_p):
                suf8[i] = suf8[i] + t_up
                tot8[i] = t_new
            for i in range(mid_p, hi_p):
                pre8[i] = pre8[i] + t_lo
                tot8[i] = t_new
    pre = jnp.concatenate(pre8, axis=0)
    suf = jnp.concatenate(suf8, axis=0)
    q_dec, k_dec = (suf, pre) if reverse else (pre, suf)
    q_in = qb * jnp.exp2(q_dec).astype(BF16)
    k_out = kb * jnp.exp2(k_dec).astype(BF16)
    decay = jnp.exp2(tot8[0][0:1, :])

    o_inter = _dot_nt(q_in, st.astype(BF16))
    ds = _dot_tn(v, k_out)
    o_diag = [_dot(acc[i].astype(BF16), v[i * HGRN_DIAG:(i + 1) * HGRN_DIAG]) for i in range(n_diag)]
    o_big = {}
    for (lv, j), a in big_scores.items():
        lo, mid, hi = (2 * j) << lv, (2 * j + 1) << lv, (2 * j + 2) << lv
        o_big[lv, j] = _dot(a.astype(BF16), v[mid:hi] if reverse else v[lo:mid])
    st_new = st * decay + ds
    o = jnp.concatenate(o_diag, axis=0) + o_inter
    for lv in range(HGRN_MASKED_LEVELS, HGRN_LEVELS):
        contrib = []
        for j in range(n >> (lv + 1)):
            zeros = jnp.zeros((1 << lv, HEAD_DIM), F32)
            contrib += [o_big[lv, j], zeros] if reverse else [zeros, o_big[lv, j]]
        o = o + jnp.concatenate(contrib, axis=0)
    return o, st_new


def _hgrn_kernel(*refs, n_blocks, has_state):
    if has_state:
        (q_ref, v_ref, xf_ref, xb_ref, gate_ref, lbf_ref, lbb_ref, g_ref, mask_ref, s0_ref,
         o_ref, acc_ref) = refs
    else:
        (q_ref, v_ref, xf_ref, xb_ref, gate_ref, lbf_ref, lbb_ref, g_ref, mask_ref,
         o_ref, sout_ref, acc_ref) = refs
    n = HGRN_ROWS
    sub = lax.broadcasted_iota(jnp.int32, (n // 8, 8, HEAD_DIM), 1)
    head_lanes = [slice(h * HEAD_DIM, (h + 1) * HEAD_DIM) for h in range(N_HEADS)]

    def lower_bound(ref, lanes):
        l0, l1 = ref[0:1, lanes], ref[1:2, lanes]
        mx = jnp.maximum(l0, l1)
        e0, e1 = jnp.exp(l0 - mx), jnp.exp(l1 - mx)
        return e0 / (e0 + e1)

    x_refs = (xf_ref, xb_ref)
    for b in range(q_ref.shape[0]):
        for h, lanes in enumerate(head_lanes):
            lbs = [lower_bound(lbf_ref, lanes), lower_bound(lbb_ref, lanes)]
            sts = [s0_ref[b, d, h].T if has_state else jnp.zeros((HEAD_DIM, HEAD_DIM), F32) for d in range(2)]
            for j in range(n_blocks):
                for d in range(2):
                    blk = (n_blocks - 1 - j) if d else j
                    rows = slice(blk * n, (blk + 1) * n)
                    f = lbs[d] + (1.0 - lbs[d]) * jax.nn.sigmoid(x_refs[d][b, rows, lanes])
                    o, sts[d] = _hgrn_block(q_ref[b, rows, lanes], 1.0 - f, jnp.log2(f), v_ref[b, rows, lanes],
                                            sts[d], mask_ref, sub, reverse=bool(d))
                    acc_ref[b, d, rows, lanes] = o
            if not has_state:
                for d in range(2):
                    sout_ref[b, d, h] = sts[d].T
    for b in range(q_ref.shape[0]):
        for lanes in head_lanes:
            o = acc_ref[b, 0, :, lanes] + acc_ref[b, 1, :, lanes]
            o = o * lax.rsqrt(jnp.mean(o * o, axis=-1, keepdims=True) + NORM_EPS)
            o_ref[b, :, lanes] = (o * g_ref[:, lanes] * gate_ref[b, :, lanes]).astype(BF16)


def _hgrn_call(rq, ri, xf, xb, gate, lbf, lbb, rnn_g, masks, s0):
    b, l, w = rq.shape
    n_blocks = l // HGRN_ROWS
    nb = SHORT_SEQS_PER_STEP if n_blocks == 1 and b % SHORT_SEQS_PER_STEP == 0 else 1
    seq = lambda: pl.BlockSpec((nb, l, w), lambda i: (i, 0, 0))
    const = lambda a: pl.BlockSpec(a.shape, lambda i: (0,) * a.ndim)
    state_spec = pl.BlockSpec((nb, 2, N_HEADS, HEAD_DIM, HEAD_DIM), lambda i: (i, 0, 0, 0, 0))
    in_specs = [seq(), seq(), seq(), seq(), seq(), const(lbf), const(lbb), const(rnn_g), const(masks)]
    args = [rq, ri, xf, xb, gate, lbf, lbb, rnn_g, masks]
    out_specs = [seq()]
    out_shape = [jax.ShapeDtypeStruct((b, l, w), BF16)]
    if s0 is not None:
        in_specs.append(state_spec)
        args.append(s0)
    else:
        out_specs.append(state_spec)
        out_shape.append(jax.ShapeDtypeStruct((b, 2, N_HEADS, HEAD_DIM, HEAD_DIM), F32))
    return pl.pallas_call(
        functools.partial(_hgrn_kernel, n_blocks=n_blocks, has_state=s0 is not None),
        grid=(b // nb,),
        in_specs=in_specs,
        out_specs=out_specs,
        out_shape=out_shape,
        scratch_shapes=[pltpu.VMEM((nb, 2, l, w), F32)],
        compiler_params=_params(("arbitrary",)),
        name="hgrn_lat" if s0 is not None else "hgrn_ctx",
    )(*args)


def _tail_kernel(att_ref, rnn_ref, x_ref, mod_ref, wo_ref, g1_ref, b1_ref, wu_ref, cw_ref, cb_ref, wd_ref,
                 g2_ref, b2_ref, o_ref, x1_ref, xm2_ref, hid_ref, *, seq_len, blocks_per_row):
    rows = x_ref.shape[0]
    gate1, shift2, scale2, gate2 = (_mod_vector(mod_ref, k, blocks_per_row) for k in (2, 3, 4, 5))
    slabs = [slice(r, r + NORM_ROWS) for r in range(0, rows, NORM_ROWS)]
    mixes = [_dot(att_ref[sl, :], wo_ref[0:GROUP_W, :]) + _dot(rnn_ref[sl, :], wo_ref[GROUP_W:2 * GROUP_W, :])
             for sl in slabs]
    for sl, mix in zip(slabs, mixes):
        x1 = _layer_norm(DEEPNORM_ALPHA * x_ref[sl, :] + gate1 * mix, g1_ref[...], b1_ref[...])
        x1_ref[sl, :] = x1
        xm2_ref[sl, :] = (x1 * (1.0 + scale2) + shift2).astype(BF16)

    sub = lax.broadcasted_iota(jnp.int32, (8, FF_TILE), 0)

    def shifted(h, shift, edge_row):
        r = pltpu.roll(h, shift, axis=0)
        parts = []
        for s in range(rows // seq_len):
            edge = s * seq_len + (edge_row // 8) * 8
            fixed = jnp.where(sub == edge_row % 8, 0.0, r[edge:edge + 8])
            parts += [r[s * seq_len:edge], fixed, r[edge + 8:(s + 1) * seq_len]]
        return jnp.concatenate([p for p in parts if p.shape[0]], axis=0)

    def conv(h, cols):
        prev = shifted(h, 1, 0)
        nxt = shifted(h, rows - 1, seq_len - 1)
        return prev * cw_ref[0:1, cols] + h * cw_ref[1:2, cols] + nxt * cw_ref[2:3, cols] + cb_ref[:, cols]

    for j in range(D_FF // FF_TILE):
        cols_a = slice(j * FF_TILE, (j + 1) * FF_TILE)
        cols_u = slice(D_FF + j * FF_TILE, D_FF + (j + 1) * FF_TILE)
        a = conv(_dot(xm2_ref[...], wu_ref[:, cols_a]), cols_a)
        u = conv(_dot(xm2_ref[...], wu_ref[:, cols_u]), cols_u)
        hid_ref[:, cols_a] = (_silu(a) * u).astype(BF16)
    ffns = [_dot(hid_ref[sl, :], wd_ref[...]) for sl in slabs]
    for sl, ffn in zip(slabs, ffns):
        y = DEEPNORM_ALPHA * x1_ref[sl, :] + gate2 * ffn
        o_ref[sl, :] = _layer_norm(y, g2_ref[...], b2_ref[...])


def _tail_call(att, rnn, x2d, mod, w_out, ln1_g, ln1_b, w_up, conv_w, conv_b, w_down, ln2_g, ln2_b,
               *, latent, seq_len):
    t = x2d.shape[0]
    rb = ROWS_FFN
    assert rb % seq_len == 0

    assert not latent or rb == seq_len
    row_spec = lambda w: pl.BlockSpec((rb, w), lambda i: (i, 0))
    resident = lambda a: pl.BlockSpec(a.shape, lambda i: (0, 0), pipeline_mode=pl.Buffered(1))
    return pl.pallas_call(
        functools.partial(_tail_kernel, seq_len=seq_len, blocks_per_row=1 if latent else None),
        grid=(t // rb,),
        in_specs=[row_spec(GROUP_W), row_spec(GROUP_W), row_spec(D_MODEL),
                  pl.BlockSpec((MOD_GROUP, mod.shape[1]), lambda i: (0 if latent else 1, 0)),
                  resident(w_out), resident(ln1_g), resident(ln1_b),
                  resident(w_up), resident(conv_w), resident(conv_b), resident(w_down),
                  resident(ln2_g), resident(ln2_b)],
        out_specs=row_spec(D_MODEL),
        out_shape=jax.ShapeDtypeStruct((t, D_MODEL), F32),
        scratch_shapes=[pltpu.VMEM((rb, D_MODEL), F32), pltpu.VMEM((rb, D_MODEL), BF16),
                        pltpu.VMEM((rb, D_FF), BF16)],
        compiler_params=_params(("arbitrary",)),
        name="tail_lat" if latent else "tail_ctx",
    )(att, rnn, x2d, mod, w_out, ln1_g, ln1_b, w_up, conv_w, conv_b, w_down, ln2_g, ln2_b)


def _rope_tables(seq_len):
    quarter = QK_DIM // 4
    freqs = 1.0 / (ROPE_BASE ** (np.arange(quarter, dtype=np.float64) / quarter))
    t = np.arange(seq_len)
    ang_r = (t // GRID_W)[:, None] * freqs
    ang_c = (t % GRID_W)[:, None] * freqs
    zeros = np.zeros_like(ang_r)

    def tile(parts):
        return jnp.asarray(np.tile(np.concatenate(parts, axis=-1), (1, GROUP_W // QK_DIM)).astype(np.float32))

    cos = tile([np.cos(ang_r), np.cos(ang_r), np.cos(ang_c), np.cos(ang_c)])
    sin_a = tile([-np.sin(ang_r), zeros, -np.sin(ang_c), zeros])
    sin_b = tile([zeros, np.sin(ang_r), zeros, np.sin(ang_c)])
    return cos, sin_a, sin_b


def kernel(x_prompt, x_sample, cache_k, cache_v, state_rnn, c, c_ctx, w_ada, b_ada, w_in, lambda_q1, lambda_k1, lambda_q2, lambda_k2, lb_fwd_logits, lb_bwd_logits, att_norm_g, rnn_norm_g, w_out, ln1_g, ln1_b, w_up, conv_w, conv_b, w_down, ln2_g, ln2_b):
    assert w_ada.shape[0] == DEPTH
    bp, lp, d = x_prompt.shape
    bs, ls, _ = x_sample.shape
    past = cache_k.shape[2]

    mod, w_in_b = _mod_call(c, c_ctx.reshape(1, d), w_ada[0], b_ada, w_in[0])

    lams = (lambda_q1, lambda_k1, lambda_q2, lambda_k2)
    masks = jnp.asarray(_hgrn_masks(), dtype=BF16)
    xp2d = x_prompt.reshape(bp * lp, d)
    xs2d = x_sample.reshape(bs * ls, d)

    ctx_in = _inproj_call(xp2d, mod, w_in_b, None, latent=False, seq_len=lp)
    lat_in = _inproj_call(xs2d, mod, w_in_b, _rope_tables(ls), latent=True, seq_len=ls)
    k_raw, v_raw = ctx_in[8:]
    cache = (jnp.transpose(cache_k, (0, 1, 3, 4, 5, 2)).reshape(bs, GROUP_W, past),
             cache_v.reshape(bs, past * N_HEADS, HEAD_DIM))

    def heads(proj, b, l):
        return tuple(o.reshape(b, l, GROUP_W) for o in proj[:8])

    q, k, v, rq, xf, xb, ri, rg = heads(lat_in, bs, ls)
    att_s, w_out_b, w_up_b, w_down_b = _attn_call(q, k, v, cache, lams, att_norm_g,
                                                  casts=(w_out[0], w_up[0], w_down[0]))
    rnn_s, = _hgrn_call(rq, ri, xf, xb, rg, lb_fwd_logits, lb_bwd_logits, rnn_norm_g, masks,
                        state_rnn.reshape(bs, 2, N_HEADS, HEAD_DIM, HEAD_DIM))
    q, k, v, rq, xf, xb, ri, rg = heads(ctx_in, bp, lp)
    att_p, = _attn_call(q, k, v, None, lams, att_norm_g)
    rnn_p, s_new = _hgrn_call(rq, ri, xf, xb, rg, lb_fwd_logits, lb_bwd_logits, rnn_norm_g, masks, None)

    def tail(att, rnn, x2d, b, l, latent):
        y = _tail_call(att.reshape(b * l, GROUP_W), rnn.reshape(b * l, GROUP_W), x2d, mod, w_out_b, ln1_g,
                       ln1_b, w_up_b, conv_w[0], conv_b, w_down_b, ln2_g, ln2_b, latent=latent, seq_len=l)
        return y.reshape(b, l, d)

    y_p = tail(att_p, rnn_p, xp2d, bp, lp, False)
    y_s = tail(att_s, rnn_s, xs2d, bs, ls, True)

    new_cache_k = k_raw.reshape(bp, DEPTH, lp, N_HEADS, 2, QK_DIM)
    new_cache_v = v_raw.reshape(bp, DEPTH, lp, N_HEADS, HEAD_DIM)
    new_state = s_new.reshape(bp, DEPTH, 2, N_HEADS, HEAD_DIM, HEAD_DIM)
    return (y_p, y_s, new_cache_k, new_cache_v, new_state)
```

```python
import functools
import math

import jax
import jax.numpy as jnp
import numpy as np
from jax import lax
from jax.experimental import pallas as pl
from jax.experimental.pallas import tpu as pltpu

D_MODEL = 1024
GRID_W = 64
N_HEADS = 4
HEAD_DIM = 128
QK_DIM = 64
GROUP_W = 512
N_GROUPS = 8
D_FF = 2816
ROPE_BASE = 10000.0
DEPTH = 1
DEEPNORM_ALPHA = (2.0 * DEPTH) ** 0.25
NORM_EPS = 1e-5
LAM_INIT = 0.8 - 0.6 * math.exp(-0.3 * 0)
LOG2_E = math.log2(math.e)

V7X_VMEM_BYTES = 64 * 1024 * 1024
VMEM_LIMIT = V7X_VMEM_BYTES * 15 // 16
MOD_GROUP = 8

MOD_TILE = 1536
ROWS_PROJ = 512
ROWS_FFN = 1024
FF_TILE = 256
NORM_ROWS = 256
Q_ROWS = 512
SHORT_SEQS_PER_STEP = 2
HGRN_ROWS = 256
HGRN_LEVELS = int(math.log2(HGRN_ROWS))

F32 = jnp.float32
BF16 = jnp.bfloat16


def _params(semantics):
    return pltpu.CompilerParams(dimension_semantics=semantics, vmem_limit_bytes=VMEM_LIMIT)


def _dot(a, b):
    return jnp.dot(a, b, preferred_element_type=F32)


def _dot_nt(a, b):
    return lax.dot_general(a, b, (((1,), (1,)), ((), ())), preferred_element_type=F32)


def _dot_tn(a, b):
    return lax.dot_general(a, b, (((0,), (0,)), ((), ())), preferred_element_type=F32)


def _silu(x):
    return x * jax.nn.sigmoid(x)


def _layer_norm(y, g, b):
    mu = jnp.mean(y, axis=-1, keepdims=True)
    d = y - mu
    var = jnp.mean(d * d, axis=-1, keepdims=True)
    return d * lax.rsqrt(var + NORM_EPS) * g + b


def _mod_kernel(c_ref, cctx_ref, w_ref, b_ref, win_ref, o_ref, win_out_ref):
    cond = jnp.concatenate([c_ref[...], jnp.broadcast_to(cctx_ref[...], c_ref.shape)], axis=0)
    s = _silu(cond).astype(BF16)
    o_ref[...] = _dot(s, w_ref[...].astype(BF16)) + b_ref[...]
    win_out_ref[...] = win_ref[...].astype(BF16)


def _mod_call(c, c_ctx, w_ada, b_ada, w_in):
    assert c.shape[0] == MOD_GROUP
    n, d = 2 * MOD_GROUP, c.shape[1]
    cols = w_ada.shape[1]
    tile = MOD_TILE
    n_steps = cols // tile
    slab = pl.BlockSpec((w_in.shape[0] // n_steps, w_in.shape[1]), lambda j: (j, 0))
    return pl.pallas_call(
        _mod_kernel,
        grid=(n_steps,),
        in_specs=[pl.BlockSpec(c.shape, lambda j: (0, 0)),
                  pl.BlockSpec(c_ctx.shape, lambda j: (0, 0)),
                  pl.BlockSpec((d, tile), lambda j: (0, j)),
                  pl.BlockSpec((1, tile), lambda j: (0, j)),
                  slab],
        out_specs=[pl.BlockSpec((n, tile), lambda j: (0, j)), slab],
        out_shape=[jax.ShapeDtypeStruct((n, cols), F32), jax.ShapeDtypeStruct(w_in.shape, BF16)],
        compiler_params=_params(("arbitrary",)),
        name="mod",
    )(c, c_ctx, w_ada, b_ada, w_in)


def _mod_vector(mod_ref, k, blocks_per_row):
    row = 0 if blocks_per_row is None else pl.program_id(0) // blocks_per_row
    return mod_ref[pl.ds(row, 1), k * D_MODEL:(k + 1) * D_MODEL]


def _rope(x, cos_ref, sin_a_ref, sin_b_ref):
    parts = []
    for c in range(GROUP_W // 128):
        sl = slice(c * 128, (c + 1) * 128)
        xc = x[:, sl]
        parts.append(xc * cos_ref[:, sl]
                     + pltpu.roll(xc, 128 - 16, axis=1) * sin_a_ref[:, sl]
                     + pltpu.roll(xc, 16, axis=1) * sin_b_ref[:, sl])
    return jnp.concatenate(parts, axis=1)


def _inproj_kernel(*refs, latent, blocks_per_row):
    if latent:
        (x_ref, mod_ref, w_ref, cos_ref, sin_a_ref, sin_b_ref,
         q_ref, k_ref, v_ref, rq_ref, xf_ref, xb_ref, ri_ref, rg_ref) = refs
    else:
        (x_ref, mod_ref, w_ref,
         q_ref, k_ref, v_ref, rq_ref, xf_ref, xb_ref, ri_ref, rg_ref, kraw_ref, vraw_ref) = refs
    shift1, scale1 = (_mod_vector(mod_ref, k, blocks_per_row) for k in (0, 1))
    xm = (x_ref[...] * (1.0 + scale1) + shift1).astype(BF16)

    def proj(g):
        return _dot(xm, w_ref[:, g * GROUP_W:(g + 1) * GROUP_W])

    aq = proj(0)
    ak = proj(1)
    av = proj(2)
    if latent:
        aq = _rope(aq, cos_ref, sin_a_ref, sin_b_ref)
        ak = _rope(ak, cos_ref, sin_a_ref, sin_b_ref)
    else:
        rows = ak.shape[0]
        for j in range(2 * N_HEADS):
            kraw_ref[pl.ds(j, rows, stride=2 * N_HEADS), :] = ak[:, j * QK_DIM:(j + 1) * QK_DIM]
        for h in range(N_HEADS):
            vraw_ref[pl.ds(h, rows, stride=N_HEADS), :] = av[:, h * HEAD_DIM:(h + 1) * HEAD_DIM]
    q_ref[...] = (aq * (QK_DIM ** -0.5 * LOG2_E)).astype(BF16)
    k_ref[...] = ak.astype(BF16)
    v_ref[...] = av.astype(BF16)
    rq_ref[...] = proj(3)
    xf_ref[...] = proj(4)
    xb_ref[...] = proj(5)
    ri_ref[...] = proj(6).astype(BF16)
    rg_ref[...] = _silu(proj(7))


def _inproj_call(x2d, mod, w_in, rope_tabs, *, latent, seq_len):
    t = x2d.shape[0]
    rb = ROWS_PROJ
    blocks_per_seq = seq_len // rb if latent else 1

    row_spec = lambda w: pl.BlockSpec((rb, w), lambda i: (i, 0))
    in_specs = [row_spec(D_MODEL),
                pl.BlockSpec((MOD_GROUP, mod.shape[1]), lambda i: (0 if latent else 1, 0)),
                pl.BlockSpec(w_in.shape, lambda i: (0, 0))]
    args = [x2d, mod, w_in]
    if latent:
        in_specs += [pl.BlockSpec((rb, GROUP_W), lambda i: (i % blocks_per_seq, 0))] * 3
        args += list(rope_tabs)
    dts = [BF16, BF16, BF16, F32, F32, F32, BF16, F32]
    out_specs = [row_spec(GROUP_W) for _ in dts]
    out_shape = [jax.ShapeDtypeStruct((t, GROUP_W), dt) for dt in dts]
    if not latent:
        out_specs += [pl.BlockSpec((rb * 2 * N_HEADS, QK_DIM), lambda i: (i, 0)),
                      pl.BlockSpec((rb * N_HEADS, HEAD_DIM), lambda i: (i, 0))]
        out_shape += [jax.ShapeDtypeStruct((t * 2 * N_HEADS, QK_DIM), F32),
                      jax.ShapeDtypeStruct((t * N_HEADS, HEAD_DIM), F32)]
    return pl.pallas_call(
        functools.partial(_inproj_kernel, latent=latent, blocks_per_row=blocks_per_seq if latent else None),
        grid=(t // rb,),
        in_specs=in_specs,
        out_specs=out_specs,
        out_shape=out_shape,
        compiler_params=_params(("arbitrary",)),
        name="inproj_lat" if latent else "inproj_ctx",
    )(*args)


def _attn_kernel(*refs, has_cache, n_casts):
    if has_cache:
        (q_ref, k_ref, v_ref, ck_ref, cv_ref, lq1, lk1, lq2, lk2, g_ref) = refs[:10]
        o_ref = refs[10 + n_casts]
        for src, dst in zip(refs[10:10 + n_casts], refs[11 + n_casts:]):
            dst[...] = src[...].astype(BF16)
    else:
        (q_ref, k_ref, v_ref, lq1, lk1, lq2, lk2, g_ref, o_ref) = refs
    lam = (jnp.exp(jnp.sum(lq1[...] * lk1[...], axis=-1, keepdims=True))
           - jnp.exp(jnp.sum(lq2[...] * lk2[...], axis=-1, keepdims=True)) + LAM_INIT)
    qb = q_ref.shape[1]
    lane = lax.broadcasted_iota(jnp.int32, (qb, HEAD_DIM), 1)
    first_map = lane < QK_DIM
    zero = jnp.zeros((), BF16)

    units = [(b, h) for b in range(q_ref.shape[0]) for h in range(N_HEADS)]

    def scores(u):
        b, h = units[u]
        sl = slice(h * HEAD_DIM, (h + 1) * HEAD_DIM)
        qh = q_ref[b, :, sl]
        qq = jnp.concatenate([jnp.where(first_map, qh, zero), jnp.where(first_map, zero, qh)], axis=0)
        s_n = _dot_nt(k_ref[b, :, sl], qq)
        s_c = None
        if has_cache:
            s_c = _dot_nt(ck_ref[b, sl, :].astype(BF16).T, qq)
        return s_n, s_c

    ahead = 2
    pending = [scores(u) for u in range(ahead)]
    for u, (b, h) in enumerate(units):
        sl = slice(h * HEAD_DIM, (h + 1) * HEAD_DIM)
        s_n, s_c = pending.pop(0)
        if u + ahead < len(units):
            pending.append(scores(u + ahead))
        mx = jnp.max(s_n, axis=0, keepdims=True)
        if has_cache:
            mx = jnp.maximum(mx, jnp.max(s_c, axis=0, keepdims=True))
        e_n = jnp.exp2(s_n - mx)
        den = jnp.sum(e_n, axis=0, keepdims=True)
        ev = _dot_tn(v_ref[b, :, sl], e_n.astype(BF16))
        if has_cache:
            e_c = jnp.exp2(s_c - mx)
            den = den + jnp.sum(e_c, axis=0, keepdims=True)
            vc = cv_ref[b, pl.ds(h, e_c.shape[0], stride=N_HEADS), :]
            ev = ev + _dot_tn(vc.astype(BF16), e_c.astype(BF16))
        inv = 1.0 / den
        o = ev[:, :qb] * inv[:, :qb] - ev[:, qb:] * (inv[:, qb:] * lam)
        o = o * lax.rsqrt(jnp.mean(o * o, axis=0, keepdims=True) + NORM_EPS)
        o_ref[b, :, sl] = (o.T * g_ref[:, sl] * (1.0 - LAM_INIT)).astype(BF16)


def _attn_call(q, k, v, cache, lams, att_g, casts=()):
    b, l, _ = q.shape
    qb = min(Q_ROWS, l)
    nq = l // qb
    nb = SHORT_SEQS_PER_STEP if nq == 1 and b % SHORT_SEQS_PER_STEP == 0 else 1
    steps = b // nb
    slab = lambda a: pl.BlockSpec((a.shape[0] // (steps * nq), a.shape[1]), lambda i, j: (i * nq + j, 0))
    full = lambda a: pl.BlockSpec((nb,) + a.shape[1:], lambda i, j: (i,) + (0,) * (a.ndim - 1))
    in_specs = [pl.BlockSpec((nb, qb, GROUP_W), lambda i, j: (i, j, 0)), full(k), full(v)]
    args = [q, k, v]
    if cache is not None:
        in_specs += [full(cache[0]), full(cache[1])]
        args += list(cache)
    in_specs += [pl.BlockSpec((1, QK_DIM), lambda i, j: (0, 0))] * 4
    in_specs += [pl.BlockSpec((1, GROUP_W), lambda i, j: (0, 0))]
    args += list(lams) + [att_g] + list(casts)
    in_specs += [slab(a) for a in casts]
    return pl.pallas_call(
        functools.partial(_attn_kernel, has_cache=cache is not None, n_casts=len(casts)),
        grid=(steps, nq),
        in_specs=in_specs,
        out_specs=[pl.BlockSpec((nb, qb, GROUP_W), lambda i, j: (i, j, 0))] + [slab(a) for a in casts],
        out_shape=[jax.ShapeDtypeStruct((b, l, GROUP_W), BF16)] + [jax.ShapeDtypeStruct(a.shape, BF16) for a in casts],
        compiler_params=_params(("arbitrary", "arbitrary")),
        name="attn_lat" if cache is not None else "attn_ctx",
    )(*args)


HGRN_DIAG = 128
HGRN_MASKED_LEVELS = 5
HGRN_VREG_LEVELS = 3


def _hgrn_masks():
    r = np.arange(HGRN_DIAG)[:, None]
    c = np.arange(HGRN_DIAG)[None, :]
    levels = []
    for lv in range(HGRN_MASKED_LEVELS):
        same = (r >> (lv + 1)) == (c >> (lv + 1))
        levels.append(same & (((r >> lv) & 1) == 1) & (((c >> lv) & 1) == 0))
    fwd = [r == c] + levels
    bwd = [m.T for m in levels]
    return np.stack(fwd + bwd).astype(np.float32)


def _hgrn_block(q, k, g, v, st, mask_ref, sub, *, reverse):
    n = HGRN_ROWS
    n_diag = n // HGRN_DIAG

    def diag_scores(qe, ke, m):
        return [_dot_nt(qe[i * HGRN_DIAG:(i + 1) * HGRN_DIAG], ke[i * HGRN_DIAG:(i + 1) * HGRN_DIAG]).astype(BF16) * m
                for i in range(n_diag)]

    def level_mask(lv):
        return mask_ref[(1 + HGRN_MASKED_LEVELS + lv) if reverse else (1 + lv)]

    g3 = g.reshape(n // 8, 8, HEAD_DIM)
    zero3 = jnp.zeros_like(g3)
    pre3, suf3 = (zero3, g3) if reverse else (g3, zero3)
    tot3 = g3
    qb, kb = q.astype(BF16), k.astype(BF16)
    acc = diag_scores(qb, kb, mask_ref[0])
    for lv in range(HGRN_VREG_LEVELS):
        b = 1 << lv
        upper = ((sub >> lv) & 1) == 1
        e = jnp.exp2(jnp.where(upper, pre3, suf3)).reshape(n, HEAD_DIM)
        eb = e.astype(BF16)
        part = diag_scores(qb * eb, kb * eb, level_mask(lv))
        acc = [a + p for a, p in zip(acc, part)]
        sib = jnp.where(upper, pltpu.roll(tot3, b, axis=1), pltpu.roll(tot3, 8 - b, axis=1))
        pre3 = pre3 + jnp.where(upper, sib, 0.0)
        suf3 = suf3 + jnp.where(upper, 0.0, sib)
        tot3 = tot3 + sib

    pieces = lambda x3: [x3[i] for i in range(n // 8)]
    pre8, suf8, tot8 = pieces(pre3), pieces(suf3), pieces(tot3)
    rows = lambda xs, lo, hi: jnp.concatenate(xs[lo // 8:hi // 8], axis=0) if hi - lo > 8 else xs[lo // 8]
    big_scores = {}
    for lv in range(HGRN_VREG_LEVELS, HGRN_LEVELS):
        b = 1 << lv
        pb = b // 8
        if lv < HGRN_MASKED_LEVELS:
            u = jnp.concatenate([(pre8 if (i // pb) % 2 else suf8)[i] for i in range(n // 8)], axis=0)
            e = jnp.exp2(u)
            eb = e.astype(BF16)
            part = diag_scores(qb * eb, kb * eb, level_mask(lv))
            acc = [a + p for a, p in zip(acc, part)]
        else:
            for j in range(n // (2 * b)):
                lo, mid, hi = 2 * b * j, 2 * b * j + b, 2 * b * (j + 1)
                e_lo = jnp.exp2(rows(suf8, lo, mid))
                e_up = jnp.exp2(rows(pre8, mid, hi))
                x_lo = (qb if reverse else kb)[lo:mid] * e_lo.astype(BF16)
                x_up = (kb if reverse else qb)[mid:hi] * e_up.astype(BF16)
                big_scores[lv, j] = _dot_nt(x_lo, x_up) if reverse else _dot_nt(x_up, x_lo)
        for j in range(n // (2 * b)):
            lo_p, mid_p, hi_p = 2 * pb * j, 2 * pb * j + pb, 2 * pb * (j + 1)
            t_lo, t_up = tot8[lo_p], tot8[mid_p]
            t_new = t_lo + t_up
            for i in range(lo_p, mid_p):
                suf8[i] = suf8[i] + t_up
                tot8[i] = t_new
            for i in range(mid_p, hi_p):
                pre8[i] = pre8[i] + t_lo
                tot8[i] = t_new
    pre = jnp.concatenate(pre8, axis=0)
    suf = jnp.concatenate(suf8, axis=0)
    q_dec, k_dec = (suf, pre) if reverse else (pre, suf)
    q_in = qb * jnp.exp2(q_dec).astype(BF16)
    k_out = kb * jnp.exp2(k_dec).astype(BF16)
    decay = jnp.exp2(tot8[0][0:1, :])

    o_inter = _dot_nt(q_in, st.astype(BF16))
    ds = _dot_tn(v, k_out)
    o_diag = [_dot(acc[i].astype(BF16), v[i * HGRN_DIAG:(i + 1) * HGRN_DIAG]) for i in range(n_diag)]
    o_big = {}
    for (lv, j), a in big_scores.items():
        lo, mid, hi = (2 * j) << lv, (2 * j + 1) << lv, (2 * j + 2) << lv
        o_big[lv, j] = _dot(a.astype(BF16), v[mid:hi] if reverse else v[lo:mid])
    st_new = st * decay + ds
    o = jnp.concatenate(o_diag, axis=0) + o_inter
    for lv in range(HGRN_MASKED_LEVELS, HGRN_LEVELS):
        contrib = []
        for j in range(n >> (lv + 1)):
            zeros = jnp.zeros((1 << lv, HEAD_DIM), F32)
            contrib += [o_big[lv, j], zeros] if reverse else [zeros, o_big[lv, j]]
        o = o + jnp.concatenate(contrib, axis=0)
    return o, st_new


def _hgrn_kernel(*refs, n_blocks, has_state):
    if has_state:
        (q_ref, v_ref, xf_ref, xb_ref, gate_ref, lbf_ref, lbb_ref, g_ref, mask_ref, s0_ref,
         o_ref, acc_ref) = refs
    else:
        (q_ref, v_ref, xf_ref, xb_ref, gate_ref, lbf_ref, lbb_ref, g_ref, mask_ref,
         o_ref, sout_ref, acc_ref) = refs
    n = HGRN_ROWS
    sub = lax.broadcasted_iota(jnp.int32, (n // 8, 8, HEAD_DIM), 1)
    head_lanes = [slice(h * HEAD_DIM, (h + 1) * HEAD_DIM) for h in range(N_HEADS)]

    def lower_bound(ref, lanes):
        l0, l1 = ref[0:1, lanes], ref[1:2, lanes]
        mx = jnp.maximum(l0, l1)
        e0, e1 = jnp.exp(l0 - mx), jnp.exp(l1 - mx)
        return e0 / (e0 + e1)

    x_refs = (xf_ref, xb_ref)
    chains = [(b, h, lanes) for b in range(q_ref.shape[0]) for h, lanes in enumerate(head_lanes)]
    lbs = [[lower_bound(lbf_ref, lanes), lower_bound(lbb_ref, lanes)] for lanes in head_lanes]
    sts = {(b, h): [s0_ref[b, d, h].T if has_state else jnp.zeros((HEAD_DIM, HEAD_DIM), F32) for d in range(2)]
           for b, h, _ in chains}
    for j in range(n_blocks):
        for b, h, lanes in chains:
            for d in range(2):
                blk = (n_blocks - 1 - j) if d else j
                rows = slice(blk * n, (blk + 1) * n)
                f = lbs[h][d] + (1.0 - lbs[h][d]) * jax.nn.sigmoid(x_refs[d][b, rows, lanes])
                o, sts[b, h][d] = _hgrn_block(q_ref[b, rows, lanes], 1.0 - f, jnp.log2(f), v_ref[b, rows, lanes],
                                              sts[b, h][d], mask_ref, sub, reverse=bool(d))
                acc_ref[b, d, rows, lanes] = o
    if not has_state:
        for b, h, _ in chains:
            for d in range(2):
                sout_ref[b, d, h] = sts[b, h][d].T
    for b in range(q_ref.shape[0]):
        for lanes in head_lanes:
            o = acc_ref[b, 0, :, lanes] + acc_ref[b, 1, :, lanes]
            o = o * lax.rsqrt(jnp.mean(o * o, axis=-1, keepdims=True) + NORM_EPS)
            o_ref[b, :, lanes] = (o * g_ref[:, lanes] * gate_ref[b, :, lanes]).astype(BF16)


def _hgrn_call(rq, ri, xf, xb, gate, lbf, lbb, rnn_g, masks, s0):
    b, l, w = rq.shape
    n_blocks = l // HGRN_ROWS
    nb = SHORT_SEQS_PER_STEP if n_blocks == 1 and b % SHORT_SEQS_PER_STEP == 0 else 1
    seq = lambda: pl.BlockSpec((nb, l, w), lambda i: (i, 0, 0))
    const = lambda a: pl.BlockSpec(a.shape, lambda i: (0,) * a.ndim)
    state_spec = pl.BlockSpec((nb, 2, N_HEADS, HEAD_DIM, HEAD_DIM), lambda i: (i, 0, 0, 0, 0))
    in_specs = [seq(), seq(), seq(), seq(), seq(), const(lbf), const(lbb), const(rnn_g), const(masks)]
    args = [rq, ri, xf, xb, gate, lbf, lbb, rnn_g, masks]
    out_specs = [seq()]
    out_shape = [jax.ShapeDtypeStruct((b, l, w), BF16)]
    if s0 is not None:
        in_specs.append(state_spec)
        args.append(s0)
    else:
        out_specs.append(state_spec)
        out_shape.append(jax.ShapeDtypeStruct((b, 2, N_HEADS, HEAD_DIM, HEAD_DIM), F32))
    return pl.pallas_call(
        functools.partial(_hgrn_kernel, n_blocks=n_blocks, has_state=s0 is not None),
        grid=(b // nb,),
        in_specs=in_specs,
        out_specs=out_specs,
        out_shape=out_shape,
        scratch_shapes=[pltpu.VMEM((nb, 2, l, w), F32)],
        compiler_params=_params(("arbitrary",)),
        name="hgrn_lat" if s0 is not None else "hgrn_ctx",
    )(*args)


def _tail_kernel(att_ref, rnn_ref, x_ref, mod_ref, wo_ref, g1_ref, b1_ref, wu_ref, cw_ref, cb_ref, wd_ref,
                 g2_ref, b2_ref, o_ref, x1_ref, xm2_ref, hid_ref, *, seq_len, blocks_per_row):
    rows = x_ref.shape[0]
    gate1, shift2, scale2, gate2 = (_mod_vector(mod_ref, k, blocks_per_row) for k in (2, 3, 4, 5))
    slabs = [slice(r, r + NORM_ROWS) for r in range(0, rows, NORM_ROWS)]
    mixes = [_dot(att_ref[sl, :], wo_ref[0:GROUP_W, :]) + _dot(rnn_ref[sl, :], wo_ref[GROUP_W:2 * GROUP_W, :])
             for sl in slabs]
    for sl, mix in zip(slabs, mixes):
        x1 = _layer_norm(DEEPNORM_ALPHA * x_ref[sl, :] + gate1 * mix, g1_ref[...], b1_ref[...])
        x1_ref[sl, :] = x1
        xm2_ref[sl, :] = (x1 * (1.0 + scale2) + shift2).astype(BF16)

    sub = lax.broadcasted_iota(jnp.int32, (8, FF_TILE), 0)

    def shifted(h, shift, edge_row):
        r = pltpu.roll(h, shift, axis=0)
        parts = []
        for s in range(rows // seq_len):
            edge = s * seq_len + (edge_row // 8) * 8
            fixed = jnp.where(sub == edge_row % 8, 0.0, r[edge:edge + 8])
            parts += [r[s * seq_len:edge], fixed, r[edge + 8:(s + 1) * seq_len]]
        return jnp.concatenate([p for p in parts if p.shape[0]], axis=0)

    def conv(h, cols):
        prev = shifted(h, 1, 0)
        nxt = shifted(h, rows - 1, seq_len - 1)
        return prev * cw_ref[0:1, cols] + h * cw_ref[1:2, cols] + nxt * cw_ref[2:3, cols] + cb_ref[:, cols]

    for j in range(D_FF // FF_TILE):
        cols_a = slice(j * FF_TILE, (j + 1) * FF_TILE)
        cols_u = slice(D_FF + j * FF_TILE, D_FF + (j + 1) * FF_TILE)
        a = conv(_dot(xm2_ref[...], wu_ref[:, cols_a]), cols_a)
        u = conv(_dot(xm2_ref[...], wu_ref[:, cols_u]), cols_u)
        hid_ref[:, cols_a] = (_silu(a) * u).astype(BF16)
    ffns = [_dot(hid_ref[sl, :], wd_ref[...]) for sl in slabs]
    for sl, ffn in zip(slabs, ffns):
        y = DEEPNORM_ALPHA * x1_ref[sl, :] + gate2 * ffn
        o_ref[sl, :] = _layer_norm(y, g2_ref[...], b2_ref[...])


def _tail_call(att, rnn, x2d, mod, w_out, ln1_g, ln1_b, w_up, conv_w, conv_b, w_down, ln2_g, ln2_b,
               *, latent, seq_len):
    t = x2d.shape[0]
    rb = ROWS_FFN
    assert rb % seq_len == 0

    assert not latent or rb == seq_len
    row_spec = lambda w: pl.BlockSpec((rb, w), lambda i: (i, 0))
    resident = lambda a: pl.BlockSpec(a.shape, lambda i: (0, 0), pipeline_mode=pl.Buffered(1))
    return pl.pallas_call(
        functools.partial(_tail_kernel, seq_len=seq_len, blocks_per_row=1 if latent else None),
        grid=(t // rb,),
        in_specs=[row_spec(GROUP_W), row_spec(GROUP_W), row_spec(D_MODEL),
                  pl.BlockSpec((MOD_GROUP, mod.shape[1]), lambda i: (0 if latent else 1, 0)),
                  resident(w_out), resident(ln1_g), resident(ln1_b),
                  resident(w_up), resident(conv_w), resident(conv_b), resident(w_down),
                  resident(ln2_g), resident(ln2_b)],
        out_specs=row_spec(D_MODEL),
        out_shape=jax.ShapeDtypeStruct((t, D_MODEL), F32),
        scratch_shapes=[pltpu.VMEM((rb, D_MODEL), F32), pltpu.VMEM((rb, D_MODEL), BF16),
                        pltpu.VMEM((rb, D_FF), BF16)],
        compiler_params=_params(("arbitrary",)),
        name="tail_lat" if latent else "tail_ctx",
    )(att, rnn, x2d, mod, w_out, ln1_g, ln1_b, w_up, conv_w, conv_b, w_down, ln2_g, ln2_b)


def _rope_tables(seq_len):
    quarter = QK_DIM // 4
    freqs = 1.0 / (ROPE_BASE ** (np.arange(quarter, dtype=np.float64) / quarter))
    t = np.arange(seq_len)
    ang_r = (t // GRID_W)[:, None] * freqs
    ang_c = (t % GRID_W)[:, None] * freqs
    zeros = np.zeros_like(ang_r)

    def tile(parts):
        return jnp.asarray(np.tile(np.concatenate(parts, axis=-1), (1, GROUP_W // QK_DIM)).astype(np.float32))

    cos = tile([np.cos(ang_r), np.cos(ang_r), np.cos(ang_c), np.cos(ang_c)])
    sin_a = tile([-np.sin(ang_r), zeros, -np.sin(ang_c), zeros])
    sin_b = tile([zeros, np.sin(ang_r), zeros, np.sin(ang_c)])
    return cos, sin_a, sin_b


def kernel(x_prompt, x_sample, cache_k, cache_v, state_rnn, c, c_ctx, w_ada, b_ada, w_in, lambda_q1, lambda_k1, lambda_q2, lambda_k2, lb_fwd_logits, lb_bwd_logits, att_norm_g, rnn_norm_g, w_out, ln1_g, ln1_b, w_up, conv_w, conv_b, w_down, ln2_g, ln2_b):
    assert w_ada.shape[0] == DEPTH
    bp, lp, d = x_prompt.shape
    bs, ls, _ = x_sample.shape
    past = cache_k.shape[2]

    mod, w_in_b = _mod_call(c, c_ctx.reshape(1, d), w_ada[0], b_ada, w_in[0])

    lams = (lambda_q1, lambda_k1, lambda_q2, lambda_k2)
    masks = jnp.asarray(_hgrn_masks(), dtype=BF16)
    xp2d = x_prompt.reshape(bp * lp, d)
    xs2d = x_sample.reshape(bs * ls, d)

    ctx_in = _inproj_call(xp2d, mod, w_in_b, None, latent=False, seq_len=lp)
    lat_in = _inproj_call(xs2d, mod, w_in_b, _rope_tables(ls), latent=True, seq_len=ls)
    k_raw, v_raw = ctx_in[8:]
    cache = (jnp.transpose(cache_k, (0, 1, 3, 4, 5, 2)).reshape(bs, GROUP_W, past),
             cache_v.reshape(bs, past * N_HEADS, HEAD_DIM))

    def heads(proj, b, l):
        return tuple(o.reshape(b, l, GROUP_W) for o in proj[:8])

    q, k, v, rq, xf, xb, ri, rg = heads(lat_in, bs, ls)
    att_s, w_out_b, w_up_b, w_down_b = _attn_call(q, k, v, cache, lams, att_norm_g,
                                                  casts=(w_out[0], w_up[0], w_down[0]))
    rnn_s, = _hgrn_call(rq, ri, xf, xb, rg, lb_fwd_logits, lb_bwd_logits, rnn_norm_g, masks,
                        state_rnn.reshape(bs, 2, N_HEADS, HEAD_DIM, HEAD_DIM))
    q, k, v, rq, xf, xb, ri, rg = heads(ctx_in, bp, lp)
    att_p, = _attn_call(q, k, v, None, lams, att_norm_g)
    rnn_p, s_new = _hgrn_call(rq, ri, xf, xb, rg, lb_fwd_logits, lb_bwd_logits, rnn_norm_g, masks, None)

    def tail(att, rnn, x2d, b, l, latent):
        y = _tail_call(att.reshape(b * l, GROUP_W), rnn.reshape(b * l, GROUP_W), x2d, mod, w_out_b, ln1_g,
                       ln1_b, w_up_b, conv_w[0], conv_b, w_down_b, ln2_g, ln2_b, latent=latent, seq_len=l)
        return y.reshape(b, l, d)

    y_p = tail(att_p, rnn_p, xp2d, bp, lp, False)
    y_s = tail(att_s, rnn_s, xs2d, bs, ls, True)

    new_cache_k = k_raw.reshape(bp, DEPTH, lp, N_HEADS, 2, QK_DIM)
    new_cache_v = v_raw.reshape(bp, DEPTH, lp, N_HEADS, HEAD_DIM)
    new_state = s_new.reshape(bp, DEPTH, 2, N_HEADS, HEAD_DIM, HEAD_DIM)
    return (y_p, y_s, new_cache_k, new_cache_v, new_state)
```

```python
import functools
import math

import jax
import jax.numpy as jnp
import numpy as np
from jax import lax
from jax.experimental import pallas as pl
from jax.experimental.pallas import tpu as pltpu

D_MODEL = 1024
GRID_W = 64
N_HEADS = 4
HEAD_DIM = 128
QK_DIM = 64
GROUP_W = 512
N_GROUPS = 8
D_FF = 2816
ROPE_BASE = 10000.0
DEPTH = 1
DEEPNORM_ALPHA = (2.0 * DEPTH) ** 0.25
NORM_EPS = 1e-5
LAM_INIT = 0.8 - 0.6 * math.exp(-0.3 * 0)
LOG2_E = math.log2(math.e)

V7X_VMEM_BYTES = 64 * 1024 * 1024
VMEM_LIMIT = V7X_VMEM_BYTES * 15 // 16
MOD_GROUP = 8

MOD_TILE = 1536
ROWS_PROJ = 512
ROWS_FFN = 1024
FF_TILE = 256
NORM_ROWS = 256
Q_ROWS = 512
SHORT_SEQS_PER_STEP = 2
HGRN_ROWS = 256
HGRN_LEVELS = int(math.log2(HGRN_ROWS))

F32 = jnp.float32
BF16 = jnp.bfloat16


def _params(semantics):
    return pltpu.CompilerParams(dimension_semantics=semantics, vmem_limit_bytes=VMEM_LIMIT)


def _dot(a, b):
    return jnp.dot(a, b, preferred_element_type=F32)


def _dot_nt(a, b):
    return lax.dot_general(a, b, (((1,), (1,)), ((), ())), preferred_element_type=F32)


def _dot_tn(a, b):
    return lax.dot_general(a, b, (((0,), (0,)), ((), ())), preferred_element_type=F32)


def _silu(x):
    return x * jax.nn.sigmoid(x)


def _layer_norm(y, g, b):
    mu = jnp.mean(y, axis=-1, keepdims=True)
    d = y - mu
    var = jnp.mean(d * d, axis=-1, keepdims=True)
    return d * lax.rsqrt(var + NORM_EPS) * g + b


def _mod_kernel(c_ref, cctx_ref, w_ref, b_ref, win_ref, o_ref, win_out_ref):
    cond = jnp.concatenate([c_ref[...], jnp.broadcast_to(cctx_ref[...], c_ref.shape)], axis=0)
    s = _silu(cond).astype(BF16)
    o_ref[...] = _dot(s, w_ref[...].astype(BF16)) + b_ref[...]
    win_out_ref[...] = win_ref[...].astype(BF16)


def _mod_call(c, c_ctx, w_ada, b_ada, w_in):
    assert c.shape[0] == MOD_GROUP
    n, d = 2 * MOD_GROUP, c.shape[1]
    cols = w_ada.shape[1]
    tile = MOD_TILE
    n_steps = cols // tile
    slab = pl.BlockSpec((w_in.shape[0] // n_steps, w_in.shape[1]), lambda j: (j, 0))
    return pl.pallas_call(
        _mod_kernel,
        grid=(n_steps,),
        in_specs=[pl.BlockSpec(c.shape, lambda j: (0, 0)),
                  pl.BlockSpec(c_ctx.shape, lambda j: (0, 0)),
                  pl.BlockSpec((d, tile), lambda j: (0, j)),
                  pl.BlockSpec((1, tile), lambda j: (0, j)),
                  slab],
        out_specs=[pl.BlockSpec((n, tile), lambda j: (0, j)), slab],
        out_shape=[jax.ShapeDtypeStruct((n, cols), F32), jax.ShapeDtypeStruct(w_in.shape, BF16)],
        compiler_params=_params(("arbitrary",)),
        name="mod",
    )(c, c_ctx, w_ada, b_ada, w_in)


def _mod_vector(mod_ref, k, blocks_per_row):
    row = 0 if blocks_per_row is None else pl.program_id(0) // blocks_per_row
    return mod_ref[pl.ds(row, 1), k * D_MODEL:(k + 1) * D_MODEL]


def _rope(x, cos_ref, sin_a_ref, sin_b_ref):
    parts = []
    for c in range(GROUP_W // 128):
        sl = slice(c * 128, (c + 1) * 128)
        xc = x[:, sl]
        parts.append(xc * cos_ref[:, sl]
                     + pltpu.roll(xc, 128 - 16, axis=1) * sin_a_ref[:, sl]
                     + pltpu.roll(xc, 16, axis=1) * sin_b_ref[:, sl])
    return jnp.concatenate(parts, axis=1)


def _inproj_kernel(*refs, latent, blocks_per_row):
    if latent:
        (x_ref, mod_ref, w_ref, cos_ref, sin_a_ref, sin_b_ref,
         q_ref, k_ref, v_ref, rq_ref, xf_ref, xb_ref, ri_ref, rg_ref) = refs
    else:
        (x_ref, mod_ref, w_ref,
         q_ref, k_ref, v_ref, rq_ref, xf_ref, xb_ref, ri_ref, rg_ref, kraw_ref, vraw_ref) = refs
    shift1, scale1 = (_mod_vector(mod_ref, k, blocks_per_row) for k in (0, 1))
    xm = (x_ref[...] * (1.0 + scale1) + shift1).astype(BF16)

    def proj(g):
        return _dot(xm, w_ref[:, g * GROUP_W:(g + 1) * GROUP_W])

    aq = proj(0)
    ak = proj(1)
    av = proj(2)
    if latent:
        aq = _rope(aq, cos_ref, sin_a_ref, sin_b_ref)
        ak = _rope(ak, cos_ref, sin_a_ref, sin_b_ref)
    else:
        rows = ak.shape[0]
        for j in range(2 * N_HEADS):
            kraw_ref[pl.ds(j, rows, stride=2 * N_HEADS), :] = ak[:, j * QK_DIM:(j + 1) * QK_DIM]
        for h in range(N_HEADS):
            vraw_ref[pl.ds(h, rows, stride=N_HEADS), :] = av[:, h * HEAD_DIM:(h + 1) * HEAD_DIM]
    q_ref[...] = (aq * (QK_DIM ** -0.5 * LOG2_E)).astype(BF16)
    k_ref[...] = ak.astype(BF16)
    v_ref[...] = av.astype(BF16)
    rq_ref[...] = proj(3)
    xf_ref[...] = proj(4)
    xb_ref[...] = proj(5)
    ri_ref[...] = proj(6).astype(BF16)
    rg_ref[...] = _silu(proj(7))


def _inproj_call(x2d, mod, w_in, rope_tabs, *, latent, seq_len):
    t = x2d.shape[0]
    rb = ROWS_PROJ
    blocks_per_seq = seq_len // rb if latent else 1

    row_spec = lambda w: pl.BlockSpec((rb, w), lambda i: (i, 0))
    in_specs = [row_spec(D_MODEL),
                pl.BlockSpec((MOD_GROUP, mod.shape[1]), lambda i: (0 if latent else 1, 0)),
                pl.BlockSpec(w_in.shape, lambda i: (0, 0))]
    args = [x2d, mod, w_in]
    if latent:
        in_specs += [pl.BlockSpec((rb, GROUP_W), lambda i: (i % blocks_per_seq, 0))] * 3
        args += list(rope_tabs)
    dts = [BF16, BF16, BF16, F32, F32, F32, BF16, F32]
    out_specs = [row_spec(GROUP_W) for _ in dts]
    out_shape = [jax.ShapeDtypeStruct((t, GROUP_W), dt) for dt in dts]
    if not latent:
        out_specs += [pl.BlockSpec((rb * 2 * N_HEADS, QK_DIM), lambda i: (i, 0)),
                      pl.BlockSpec((rb * N_HEADS, HEAD_DIM), lambda i: (i, 0))]
        out_shape += [jax.ShapeDtypeStruct((t * 2 * N_HEADS, QK_DIM), F32),
                      jax.ShapeDtypeStruct((t * N_HEADS, HEAD_DIM), F32)]
    return pl.pallas_call(
        functools.partial(_inproj_kernel, latent=latent, blocks_per_row=blocks_per_seq if latent else None),
        grid=(t // rb,),
        in_specs=in_specs,
        out_specs=out_specs,
        out_shape=out_shape,
        compiler_params=_params(("arbitrary",)),
        name="inproj_lat" if latent else "inproj_ctx",
    )(*args)


def _attn_kernel(*refs, has_cache, n_casts):
    if has_cache:
        (q_ref, k_ref, v_ref, ck_ref, cv_ref, lq1, lk1, lq2, lk2, g_ref) = refs[:10]
        o_ref = refs[10 + n_casts]
        for src, dst in zip(refs[10:10 + n_casts], refs[11 + n_casts:]):
            dst[...] = src[...].astype(BF16)
    else:
        (q_ref, k_ref, v_ref, lq1, lk1, lq2, lk2, g_ref, o_ref) = refs
    lam = (jnp.exp(jnp.sum(lq1[...] * lk1[...], axis=-1, keepdims=True))
           - jnp.exp(jnp.sum(lq2[...] * lk2[...], axis=-1, keepdims=True)) + LAM_INIT)
    qb = q_ref.shape[1]
    lane = lax.broadcasted_iota(jnp.int32, (qb, HEAD_DIM), 1)
    first_map = lane < QK_DIM
    zero = jnp.zeros((), BF16)

    units = [(b, h) for b in range(q_ref.shape[0]) for h in range(N_HEADS)]

    def scores(u):
        b, h = units[u]
        sl = slice(h * HEAD_DIM, (h + 1) * HEAD_DIM)
        qh = q_ref[b, :, sl]
        qq = jnp.concatenate([jnp.where(first_map, qh, zero), jnp.where(first_map, zero, qh)], axis=0)
        s_n = _dot_nt(k_ref[b, :, sl], qq)
        s_c = None
        if has_cache:
            s_c = _dot_nt(ck_ref[b, sl, :].astype(BF16).T, qq)
        return s_n, s_c

    ahead = 2
    pending = [scores(u) for u in range(ahead)]
    for u, (b, h) in enumerate(units):
        sl = slice(h * HEAD_DIM, (h + 1) * HEAD_DIM)
        s_n, s_c = pending.pop(0)
        if u + ahead < len(units):
            pending.append(scores(u + ahead))
        mx = jnp.max(s_n, axis=0, keepdims=True)
        if has_cache:
            mx = jnp.maximum(mx, jnp.max(s_c, axis=0, keepdims=True))
        e_n = jnp.exp2(s_n - mx)
        den = jnp.sum(e_n, axis=0, keepdims=True)
        ev = _dot_tn(v_ref[b, :, sl], e_n.astype(BF16))
        if has_cache:
            e_c = jnp.exp2(s_c - mx)
            den = den + jnp.sum(e_c, axis=0, keepdims=True)
            vc = cv_ref[b, pl.ds(h, e_c.shape[0], stride=N_HEADS), :]
            ev = ev + _dot_tn(vc.astype(BF16), e_c.astype(BF16))
        inv = 1.0 / den
        o = ev[:, :qb] * inv[:, :qb] - ev[:, qb:] * (inv[:, qb:] * lam)
        o = o * lax.rsqrt(jnp.mean(o * o, axis=0, keepdims=True) + NORM_EPS)
        o_ref[b, :, sl] = (o.T * g_ref[:, sl] * (1.0 - LAM_INIT)).astype(BF16)


def _attn_call(q, k, v, cache, lams, att_g, casts=()):
    b, l, _ = q.shape
    qb = min(Q_ROWS, l)
    nq = l // qb
    nb = SHORT_SEQS_PER_STEP if nq == 1 and b % SHORT_SEQS_PER_STEP == 0 else 1
    steps = b // nb
    slab = lambda a: pl.BlockSpec((a.shape[0] // (steps * nq), a.shape[1]), lambda i, j: (i * nq + j, 0))
    full = lambda a: pl.BlockSpec((nb,) + a.shape[1:], lambda i, j: (i,) + (0,) * (a.ndim - 1))
    in_specs = [pl.BlockSpec((nb, qb, GROUP_W), lambda i, j: (i, j, 0)), full(k), full(v)]
    args = [q, k, v]
    if cache is not None:
        in_specs += [full(cache[0]), full(cache[1])]
        args += list(cache)
    in_specs += [pl.BlockSpec((1, QK_DIM), lambda i, j: (0, 0))] * 4
    in_specs += [pl.BlockSpec((1, GROUP_W), lambda i, j: (0, 0))]
    args += list(lams) + [att_g] + list(casts)
    in_specs += [slab(a) for a in casts]
    return pl.pallas_call(
        functools.partial(_attn_kernel, has_cache=cache is not None, n_casts=len(casts)),
        grid=(steps, nq),
        in_specs=in_specs,
        out_specs=[pl.BlockSpec((nb, qb, GROUP_W), lambda i, j: (i, j, 0))] + [slab(a) for a in casts],
        out_shape=[jax.ShapeDtypeStruct((b, l, GROUP_W), BF16)] + [jax.ShapeDtypeStruct(a.shape, BF16) for a in casts],
        compiler_params=_params(("arbitrary", "arbitrary")),
        name="attn_lat" if cache is not None else "attn_ctx",
    )(*args)


HGRN_DIAG = 128
HGRN_MASKED_LEVELS = 5
HGRN_VREG_LEVELS = 3


def _hgrn_masks():
    r = np.arange(HGRN_DIAG)[:, None]
    c = np.arange(HGRN_DIAG)[None, :]
    levels = []
    for lv in range(HGRN_MASKED_LEVELS):
        same = (r >> (lv + 1)) == (c >> (lv + 1))
        levels.append(same & (((r >> lv) & 1) == 1) & (((c >> lv) & 1) == 0))
    fwd = [r == c] + levels
    bwd = [m.T for m in levels]
    return np.stack(fwd + bwd).astype(np.float32)


def _hgrn_block(q, k, g, v, st, mask_ref, sub, *, reverse):
    n = HGRN_ROWS
    n_diag = n // HGRN_DIAG

    def diag_scores(qe, ke, m):
        return [_dot_nt(qe[i * HGRN_DIAG:(i + 1) * HGRN_DIAG], ke[i * HGRN_DIAG:(i + 1) * HGRN_DIAG]).astype(BF16) * m
                for i in range(n_diag)]

    def level_mask(lv):
        return mask_ref[(1 + HGRN_MASKED_LEVELS + lv) if reverse else (1 + lv)]

    g3 = g.reshape(n // 8, 8, HEAD_DIM)
    zero3 = jnp.zeros_like(g3)
    pre3, suf3 = (zero3, g3) if reverse else (g3, zero3)
    tot3 = g3
    qb, kb = q.astype(BF16), k.astype(BF16)
    acc = diag_scores(qb, kb, mask_ref[0])
    for lv in range(HGRN_VREG_LEVELS):
        b = 1 << lv
        upper = ((sub >> lv) & 1) == 1
        e = jnp.exp2(jnp.where(upper, pre3, suf3)).reshape(n, HEAD_DIM)
        eb = e.astype(BF16)
        part = diag_scores(qb * eb, kb * eb, level_mask(lv))
        acc = [a + p for a, p in zip(acc, part)]
        sib = jnp.where(upper, pltpu.roll(tot3, b, axis=1), pltpu.roll(tot3, 8 - b, axis=1))
        pre3 = pre3 + jnp.where(upper, sib, 0.0)
        suf3 = suf3 + jnp.where(upper, 0.0, sib)
        tot3 = tot3 + sib

    pieces = lambda x3: [x3[i] for i in range(n // 8)]
    pre8, suf8, tot8 = pieces(pre3), pieces(suf3), pieces(tot3)
    rows = lambda xs, lo, hi: jnp.concatenate(xs[lo // 8:hi // 8], axis=0) if hi - lo > 8 else xs[lo // 8]
    big_scores = {}
    for lv in range(HGRN_VREG_LEVELS, HGRN_LEVELS):
        b = 1 << lv
        pb = b // 8
        if lv < HGRN_MASKED_LEVELS:
            u = jnp.concatenate([(pre8 if (i // pb) % 2 else suf8)[i] for i in range(n // 8)], axis=0)
            e = jnp.exp2(u)
            eb = e.astype(BF16)
            part = diag_scores(qb * eb, kb * eb, level_mask(lv))
            acc = [a + p for a, p in zip(acc, part)]
        else:
            for j in range(n // (2 * b)):
                lo, mid, hi = 2 * b * j, 2 * b * j + b, 2 * b * (j + 1)
                e_lo = jnp.exp2(rows(suf8, lo, mid))
                e_up = jnp.exp2(rows(pre8, mid, hi))
                x_lo = (qb if reverse else kb)[lo:mid] * e_lo.astype(BF16)
                x_up = (kb if reverse else qb)[mid:hi] * e_up.astype(BF16)
                big_scores[lv, j] = _dot_nt(x_lo, x_up) if reverse else _dot_nt(x_up, x_lo)
        for j in range(n // (2 * b)):
            lo_p, mid_p, hi_p = 2 * pb * j, 2 * pb * j + pb, 2 * pb * (j + 1)
            t_lo, t_up = tot8[lo_p], tot8[mid_p]
            t_new = t_lo + t_up
            for i in range(lo_p, mid_p):
                suf8[i] = suf8[i] + t_up
                tot8[i] = t_new
            for i in range(mid_p, hi_p):
                pre8[i] = pre8[i] + t_lo
                tot8[i] = t_new
    pre = jnp.concatenate(pre8, axis=0)
    suf = jnp.concatenate(suf8, axis=0)
    q_dec, k_dec = (suf, pre) if reverse else (pre, suf)
    q_in = qb * jnp.exp2(q_dec).astype(BF16)
    k_out = kb * jnp.exp2(k_dec).astype(BF16)
    decay = jnp.exp2(tot8[0][0:1, :])

    o_inter = _dot_nt(q_in, st.astype(BF16))
    ds = _dot_tn(v, k_out)
    o_diag = [_dot(acc[i].astype(BF16), v[i * HGRN_DIAG:(i + 1) * HGRN_DIAG]) for i in range(n_diag)]
    o_big = {}
    for (lv, j), a in big_scores.items():
        lo, mid, hi = (2 * j) << lv, (2 * j + 1) << lv, (2 * j + 2) << lv
        o_big[lv, j] = _dot(a.astype(BF16), v[mid:hi] if reverse else v[lo:mid])
    st_new = st * decay + ds
    o = jnp.concatenate(o_diag, axis=0) + o_inter
    for lv in range(HGRN_MASKED_LEVELS, HGRN_LEVELS):
        contrib = []
        for j in range(n >> (lv + 1)):
            zeros = jnp.zeros((1 << lv, HEAD_DIM), F32)
            contrib += [o_big[lv, j], zeros] if reverse else [zeros, o_big[lv, j]]
        o = o + jnp.concatenate(contrib, axis=0)
    return o, st_new


def _hgrn_kernel(*refs, n_blocks, has_state):
    if has_state:
        (q_ref, v_ref, xf_ref, xb_ref, gate_ref, lbf_ref, lbb_ref, g_ref, mask_ref, s0_ref,
         o_ref, acc_ref) = refs
    else:
        (q_ref, v_ref, xf_ref, xb_ref, gate_ref, lbf_ref, lbb_ref, g_ref, mask_ref,
         o_ref, sout_ref, acc_ref) = refs
    n = HGRN_ROWS
    sub = lax.broadcasted_iota(jnp.int32, (n // 8, 8, HEAD_DIM), 1)
    head_lanes = [slice(h * HEAD_DIM, (h + 1) * HEAD_DIM) for h in range(N_HEADS)]

    def lower_bound(ref, lanes):
        l0, l1 = ref[0:1, lanes], ref[1:2, lanes]
        mx = jnp.maximum(l0, l1)
        e0, e1 = jnp.exp(l0 - mx), jnp.exp(l1 - mx)
        return e0 / (e0 + e1)

    x_refs = (xf_ref, xb_ref)
    chains = [(b, h, lanes) for b in range(q_ref.shape[0]) for h, lanes in enumerate(head_lanes)]
    lbs = [[lower_bound(lbf_ref, lanes), lower_bound(lbb_ref, lanes)] for lanes in head_lanes]
    sts = {(b, h): [s0_ref[b, d, h].T if has_state else jnp.zeros((HEAD_DIM, HEAD_DIM), F32) for d in range(2)]
           for b, h, _ in chains}
    for j in range(n_blocks):
        for b, h, lanes in chains:
            for d in range(2):
                blk = (n_blocks - 1 - j) if d else j
                rows = slice(blk * n, (blk + 1) * n)
                f = lbs[h][d] + (1.0 - lbs[h][d]) * jax.nn.sigmoid(x_refs[d][b, rows, lanes])
                o, sts[b, h][d] = _hgrn_block(q_ref[b, rows, lanes], 1.0 - f, jnp.log2(f), v_ref[b, rows, lanes],
                                              sts[b, h][d], mask_ref, sub, reverse=bool(d))
                acc_ref[b, d, rows, lanes] = o
    if not has_state:
        for b, h, _ in chains:
            for d in range(2):
                sout_ref[b, d, h] = sts[b, h][d].T
    for b in range(q_ref.shape[0]):
        for lanes in head_lanes:
            o = acc_ref[b, 0, :, lanes] + acc_ref[b, 1, :, lanes]
            o = o * lax.rsqrt(jnp.mean(o * o, axis=-1, keepdims=True) + NORM_EPS)
            o_ref[b, :, lanes] = (o * g_ref[:, lanes] * gate_ref[b, :, lanes]).astype(BF16)


def _hgrn_call(rq, ri, xf, xb, gate, lbf, lbb, rnn_g, masks, s0):
    b, l, w = rq.shape
    n_blocks = l // HGRN_ROWS
    nb = SHORT_SEQS_PER_STEP if n_blocks == 1 and b % SHORT_SEQS_PER_STEP == 0 else 1
    seq = lambda: pl.BlockSpec((nb, l, w), lambda i: (i, 0, 0))
    const = lambda a: pl.BlockSpec(a.shape, lambda i: (0,) * a.ndim)
    state_spec = pl.BlockSpec((nb, 2, N_HEADS, HEAD_DIM, HEAD_DIM), lambda i: (i, 0, 0, 0, 0))
    in_specs = [seq(), seq(), seq(), seq(), seq(), const(lbf), const(lbb), const(rnn_g), const(masks)]
    args = [rq, ri, xf, xb, gate, lbf, lbb, rnn_g, masks]
    out_specs = [seq()]
    out_shape = [jax.ShapeDtypeStruct((b, l, w), BF16)]
    if s0 is not None:
        in_specs.append(state_spec)
        args.append(s0)
    else:
        out_specs.append(state_spec)
        out_shape.append(jax.ShapeDtypeStruct((b, 2, N_HEADS, HEAD_DIM, HEAD_DIM), F32))
    return pl.pallas_call(
        functools.partial(_hgrn_kernel, n_blocks=n_blocks, has_state=s0 is not None),
        grid=(b // nb,),
        in_specs=in_specs,
        out_specs=out_specs,
        out_shape=out_shape,
        scratch_shapes=[pltpu.VMEM((nb, 2, l, w), F32)],
        compiler_params=_params(("arbitrary",)),
        name="hgrn_lat" if s0 is not None else "hgrn_ctx",
    )(*args)


def _tail_kernel(att_ref, rnn_ref, x_ref, mod_ref, wo_ref, g1_ref, b1_ref, wu_ref, cw_ref, cb_ref, wd_ref,
                 g2_ref, b2_ref, o_ref, x1_ref, xm2_ref, hid_ref, *, seq_len, blocks_per_row):
    rows = x_ref.shape[0]
    gate1, shift2, scale2, gate2 = (_mod_vector(mod_ref, k, blocks_per_row) for k in (2, 3, 4, 5))
    slabs = [slice(r, r + NORM_ROWS) for r in range(0, rows, NORM_ROWS)]
    mixes = [_dot(att_ref[sl, :], wo_ref[0:GROUP_W, :]) + _dot(rnn_ref[sl, :], wo_ref[GROUP_W:2 * GROUP_W, :])
             for sl in slabs]
    for sl, mix in zip(slabs, mixes):
        x1 = _layer_norm(DEEPNORM_ALPHA * x_ref[sl, :] + gate1 * mix, g1_ref[...], b1_ref[...])
        x1_ref[sl, :] = x1
        xm2_ref[sl, :] = (x1 * (1.0 + scale2) + shift2).astype(BF16)

    sub = lax.broadcasted_iota(jnp.int32, (8, FF_TILE), 0)

    def shifted(h, shift, edge_row):
        r = pltpu.roll(h, shift, axis=0)
        parts = []
        for s in range(rows // seq_len):
            edge = s * seq_len + (edge_row // 8) * 8
            fixed = jnp.where(sub == edge_row % 8, 0.0, r[edge:edge + 8])
            parts += [r[s * seq_len:edge], fixed, r[edge + 8:(s + 1) * seq_len]]
        return jnp.concatenate([p for p in parts if p.shape[0]], axis=0)

    def conv(h, cols, scale):
        prev = shifted(h, 1, 0)
        nxt = shifted(h, rows - 1, seq_len - 1)
        taps = [cw_ref[t:t + 1, cols] * scale for t in range(3)]
        return prev * taps[0] + h * taps[1] + nxt * taps[2] + cb_ref[:, cols] * scale

    for j in range(D_FF // FF_TILE):
        cols_a = slice(j * FF_TILE, (j + 1) * FF_TILE)
        cols_u = slice(D_FF + j * FF_TILE, D_FF + (j + 1) * FF_TILE)
        a2 = conv(_dot(xm2_ref[...], wu_ref[:, cols_a]), cols_a, -LOG2_E)
        u2 = conv(_dot(xm2_ref[...], wu_ref[:, cols_u]), cols_u, -1.0 / LOG2_E)
        hid_ref[:, cols_a] = (a2 * u2 / (1.0 + jnp.exp2(a2))).astype(BF16)
    ffns = [_dot(hid_ref[sl, :], wd_ref[...]) for sl in slabs]
    for sl, ffn in zip(slabs, ffns):
        y = DEEPNORM_ALPHA * x1_ref[sl, :] + gate2 * ffn
        o_ref[sl, :] = _layer_norm(y, g2_ref[...], b2_ref[...])


def _tail_call(att, rnn, x2d, mod, w_out, ln1_g, ln1_b, w_up, conv_w, conv_b, w_down, ln2_g, ln2_b,
               *, latent, seq_len):
    t = x2d.shape[0]
    rb = ROWS_FFN
    assert rb % seq_len == 0

    assert not latent or rb == seq_len
    row_spec = lambda w: pl.BlockSpec((rb, w), lambda i: (i, 0))
    resident = lambda a: pl.BlockSpec(a.shape, lambda i: (0, 0), pipeline_mode=pl.Buffered(1))
    return pl.pallas_call(
        functools.partial(_tail_kernel, seq_len=seq_len, blocks_per_row=1 if latent else None),
        grid=(t // rb,),
        in_specs=[row_spec(GROUP_W), row_spec(GROUP_W), row_spec(D_MODEL),
                  pl.BlockSpec((MOD_GROUP, mod.shape[1]), lambda i: (0 if latent else 1, 0)),
                  resident(w_out), resident(ln1_g), resident(ln1_b),
                  resident(w_up), resident(conv_w), resident(conv_b), resident(w_down),
                  resident(ln2_g), resident(ln2_b)],
        out_specs=row_spec(D_MODEL),
        out_shape=jax.ShapeDtypeStruct((t, D_MODEL), F32),
        scratch_shapes=[pltpu.VMEM((rb, D_MODEL), F32), pltpu.VMEM((rb, D_MODEL), BF16),
                        pltpu.VMEM((rb, D_FF), BF16)],
        compiler_params=_params(("arbitrary",)),
        name="tail_lat" if latent else "tail_ctx",
    )(att, rnn, x2d, mod, w_out, ln1_g, ln1_b, w_up, conv_w, conv_b, w_down, ln2_g, ln2_b)


def _rope_tables(seq_len):
    quarter = QK_DIM // 4
    freqs = 1.0 / (ROPE_BASE ** (np.arange(quarter, dtype=np.float64) / quarter))
    t = np.arange(seq_len)
    ang_r = (t // GRID_W)[:, None] * freqs
    ang_c = (t % GRID_W)[:, None] * freqs
    zeros = np.zeros_like(ang_r)

    def tile(parts):
        return jnp.asarray(np.tile(np.concatenate(parts, axis=-1), (1, GROUP_W // QK_DIM)).astype(np.float32))

    cos = tile([np.cos(ang_r), np.cos(ang_r), np.cos(ang_c), np.cos(ang_c)])
    sin_a = tile([-np.sin(ang_r), zeros, -np.sin(ang_c), zeros])
    sin_b = tile([zeros, np.sin(ang_r), zeros, np.sin(ang_c)])
    return cos, sin_a, sin_b


def kernel(x_prompt, x_sample, cache_k, cache_v, state_rnn, c, c_ctx, w_ada, b_ada, w_in, lambda_q1, lambda_k1, lambda_q2, lambda_k2, lb_fwd_logits, lb_bwd_logits, att_norm_g, rnn_norm_g, w_out, ln1_g, ln1_b, w_up, conv_w, conv_b, w_down, ln2_g, ln2_b):
    assert w_ada.shape[0] == DEPTH
    bp, lp, d = x_prompt.shape
    bs, ls, _ = x_sample.shape
    past = cache_k.shape[2]

    mod, w_in_b = _mod_call(c, c_ctx.reshape(1, d), w_ada[0], b_ada, w_in[0])

    lams = (lambda_q1, lambda_k1, lambda_q2, lambda_k2)
    masks = jnp.asarray(_hgrn_masks(), dtype=BF16)
    xp2d = x_prompt.reshape(bp * lp, d)
    xs2d = x_sample.reshape(bs * ls, d)

    ctx_in = _inproj_call(xp2d, mod, w_in_b, None, latent=False, seq_len=lp)
    lat_in = _inproj_call(xs2d, mod, w_in_b, _rope_tables(ls), latent=True, seq_len=ls)
    k_raw, v_raw = ctx_in[8:]
    cache = (jnp.transpose(cache_k, (0, 1, 3, 4, 5, 2)).reshape(bs, GROUP_W, past),
             cache_v.reshape(bs, past * N_HEADS, HEAD_DIM))

    def heads(proj, b, l):
        return tuple(o.reshape(b, l, GROUP_W) for o in proj[:8])

    q, k, v, rq, xf, xb, ri, rg = heads(lat_in, bs, ls)
    att_s, w_out_b, w_up_b, w_down_b = _attn_call(q, k, v, cache, lams, att_norm_g,
                                                  casts=(w_out[0], w_up[0], w_down[0]))
    rnn_s, = _hgrn_call(rq, ri, xf, xb, rg, lb_fwd_logits, lb_bwd_logits, rnn_norm_g, masks,
                        state_rnn.reshape(bs, 2, N_HEADS, HEAD_DIM, HEAD_DIM))
    q, k, v, rq, xf, xb, ri, rg = heads(ctx_in, bp, lp)
    att_p, = _attn_call(q, k, v, None, lams, att_norm_g)
    rnn_p, s_new = _hgrn_call(rq, ri, xf, xb, rg, lb_fwd_logits, lb_bwd_logits, rnn_norm_g, masks, None)

    def tail(att, rnn, x2d, b, l, latent):
        y = _tail_call(att.reshape(b * l, GROUP_W), rnn.reshape(b * l, GROUP_W), x2d, mod, w_out_b, ln1_g,
                       ln1_b, w_up_b, conv_w[0], conv_b, w_down_b, ln2_g, ln2_b, latent=latent, seq_len=l)
        return y.reshape(b, l, d)

    y_p = tail(att_p, rnn_p, xp2d, bp, lp, False)
    y_s = tail(att_s, rnn_s, xs2d, bs, ls, True)

    new_cache_k = k_raw.reshape(bp, DEPTH, lp, N_HEADS, 2, QK_DIM)
    new_cache_v = v_raw.reshape(bp, DEPTH, lp, N_HEADS, HEAD_DIM)
    new_state = s_new.reshape(bp, DEPTH, 2, N_HEADS, HEAD_DIM, HEAD_DIM)
    return (y_p, y_s, new_cache_k, new_cache_v, new_state)
```
